```python
import jax, jax.numpy as jnp
from jax import lax
import numpy as np

D_MODEL = 1024
BATCH = 1
SEQ = 16384
DEPTH = 2
DEC_BATCH = 8
DEC_SEQ = 32
PAST_LEN = 1024

CHUNK = 64
Q_BLOCK = 128
EPS = 1e-6
HD_A = 64
W_A = 3 * D_MODEL // 8
H_A = W_A // HD_A
W_B = 3 * D_MODEL // 8
H_B = 4
DV_B = W_B // H_B
DK_B = DV_B // 2
GATE_RANK_B = 16
GATE_TEMP_B = 16.0
W_C = D_MODEL - W_A - W_B
NBLK_C = 4
BW_C = W_C // NBLK_C
CONV_W = 4
RG_C = 8.0
D_MIX = W_A + W_B + W_C
IN_SIZES = (W_A, W_A, W_A, H_B * DK_B, H_B * DK_B, W_B, W_B, GATE_RANK_B, W_C, W_C)
IN_WIDTH = sum(IN_SIZES)
N_EXPERTS = 32
TOP_K = 4
D_FF = D_MODEL
SWIGLU_ALPHA = 1.702
SWIGLU_LIMIT = 7.0
MOE_BLOCK = 128

kernel_name = 'hybrid_sb_gla_rglru_moe_stream_step'


def _rmsnorm(x, g):
    xf = x.astype(jnp.float32)
    y = xf * lax.rsqrt(jnp.mean(xf * xf, axis=-1, keepdims=True) + EPS)
    return (y * g.astype(jnp.float32)).astype(x.dtype)


def _stick_breaking(q, k, v, q_pos):
    z = jnp.einsum('bqhd,bkhd->bhqk', q.astype(jnp.float32), k.astype(jnp.float32)) * (q.shape[-1] ** -0.5)
    k_pos = jnp.arange(k.shape[1])
    mask = k_pos[None, :] < q_pos[:, None]
    log_fail = jnp.where(mask, jax.nn.log_sigmoid(-z), 0.0)
    log_after = lax.cumsum(log_fail, axis=3, reverse=True) - log_fail
    w = jnp.where(mask, jnp.exp(jax.nn.log_sigmoid(z) + log_after), 0.0)
    return jnp.einsum('bhqk,bkhd->bqhd', w, v.astype(jnp.float32)).astype(q.dtype)


def _stick_breaking_blocked(q, k, v):
    B, T, H, Dh = q.shape
    nb = T // Q_BLOCK
    q_blocks = q.reshape(B, nb, Q_BLOCK, H, Dh).swapaxes(0, 1)
    pos_blocks = jnp.arange(T).reshape(nb, Q_BLOCK)
    out = lax.map(lambda qp: _stick_breaking(qp[0], k, v, qp[1]), (q_blocks, pos_blocks))
    return out.swapaxes(0, 1).reshape(B, T, H, Dh)


def _gla(q, k, v, log_g, s0):
    B, T = q.shape[:2]
    L = min(T, CHUNK)
    nc = T // L

    def to_chunks(a):
        return a.astype(jnp.float32).reshape((B, nc, L) + a.shape[2:]).swapaxes(0, 1)

    causal = jnp.tril(jnp.ones((L, L), dtype=bool))[None, :, :, None, None]

    def step(s, inp):
        qc, kc, vc, gc = inp
        b = jnp.cumsum(gc, axis=1)
        o_inter = jnp.einsum('blhk,bhkv->blhv', qc * jnp.exp(b), s)
        decay = jnp.exp(jnp.where(causal, b[:, :, None] - b[:, None, :], -jnp.inf))
        att = jnp.einsum('bthk,bshk,btshk->bhts', qc, kc, decay)
        o_intra = jnp.einsum('bhts,bshv->bthv', att, vc)
        b_last = b[:, -1]
        s_new = jnp.exp(b_last)[..., None] * s + jnp.einsum('blhk,blhv->bhkv', kc * jnp.exp(b_last[:, None] - b), vc)
        return s_new, o_inter + o_intra

    s_fin, o = lax.scan(step, s0.astype(jnp.float32),
                        (to_chunks(q * DK_B ** -0.5), to_chunks(k), to_chunks(v), to_chunks(log_g)))
    o = o.swapaxes(0, 1).reshape((B, T) + v.shape[2:])
    return o.astype(v.dtype), s_fin.astype(s0.dtype)


def _causal_conv(x, buf, w, b):
    T = x.shape[1]
    xp = jnp.concatenate([buf.astype(x.dtype), x], axis=1)
    y = b
    for j in range(CONV_W):
        y = y + xp[:, j:j + T] * w[j]
    return y, xp[:, -(CONV_W - 1):]


def _rglru(x, h0, w_a, b_a, w_x, b_x, lam):
    B, T, _ = x.shape
    xf = x.astype(jnp.float32)
    xb = xf.reshape(B, T, NBLK_C, BW_C)
    r = jax.nn.sigmoid(jnp.einsum('btnd,nde->btne', xb, w_a.astype(jnp.float32)).reshape(B, T, W_C) + b_a.astype(jnp.float32))
    i = jax.nn.sigmoid(jnp.einsum('btnd,nde->btne', xb, w_x.astype(jnp.float32)).reshape(B, T, W_C) + b_x.astype(jnp.float32))
    log_a = RG_C * r * jax.nn.log_sigmoid(lam.astype(jnp.float32))
    a = jnp.exp(log_a)
    u = jnp.sqrt(-jnp.expm1(2.0 * log_a)) * (i * xf)
    u = u.at[:, 0].add(a[:, 0] * h0.astype(jnp.float32))

    def combine(left, right):
        a1, u1 = left
        a2, u2 = right
        return a1 * a2, a2 * u1 + u2

    _, h = lax.associative_scan(combine, (a, u), axis=1)
    return h.astype(x.dtype), h[:, -1].astype(h0.dtype)


def _mixer(h, past, w_in, w_gla_gate, b_gla_gate, g_gla_norm, w_conv, b_conv,
           w_rg_a, b_rg_a, w_rg_x, b_rg_x, lru_lambda, w_out):
    k_past, v_past, s_gla, conv_buf, h_lru = past
    B, T, _ = h.shape
    split_at = [int(s) for s in np.cumsum(IN_SIZES)[:-1]]
    q_a, k_a, v_a, q_b, k_b, v_b, g_b, r_b, x_c, y_c = jnp.split(h @ w_in, split_at, axis=-1)
    q_a = q_a.reshape(B, T, H_A, HD_A)
    k_a = k_a.reshape(B, T, H_A, HD_A)
    v_a = v_a.reshape(B, T, H_A, HD_A)
    if k_past is None:
        o_a = _stick_breaking_blocked(q_a, k_a, v_a)
    else:
        n_past = k_past.shape[1]
        o_a = _stick_breaking(q_a,
                              jnp.concatenate([k_past.astype(k_a.dtype), k_a], axis=1),
                              jnp.concatenate([v_past.astype(v_a.dtype), v_a], axis=1),
                              n_past + jnp.arange(T))
    log_g = jax.nn.log_sigmoid((r_b @ w_gla_gate + b_gla_gate).astype(jnp.float32)).reshape(B, T, H_B, DK_B) / GATE_TEMP_B
    o_b, s_gla_new = _gla(q_b.reshape(B, T, H_B, DK_B), k_b.reshape(B, T, H_B, DK_B),
                          v_b.reshape(B, T, H_B, DV_B), log_g, s_gla)
    o_b = _rmsnorm(o_b, g_gla_norm) * jax.nn.silu(g_b.reshape(B, T, H_B, DV_B))
    x_conv, conv_new = _causal_conv(x_c, conv_buf, w_conv, b_conv)
    h_seq, h_new = _rglru(x_conv, h_lru, w_rg_a, b_rg_a, w_rg_x, b_rg_x, lru_lambda)
    o_c = h_seq * jax.nn.gelu(y_c)
    o = jnp.concatenate([o_a.reshape(B, T, W_A), o_b.reshape(B, T, W_B), o_c], axis=-1) @ w_out
    return o, (k_a, v_a, s_gla_new, conv_new, h_new)


def _moe(h, w_router, b_router, w_gate_up, b_gate_up, w_down, b_down):
    B, T, D = h.shape
    n_tok = B * T
    n_slot = n_tok * TOP_K
    xt = h.reshape(n_tok, D)
    logits = xt.astype(jnp.float32) @ w_router.astype(jnp.float32) + b_router.astype(jnp.float32)
    top_v, top_e = lax.top_k(logits, TOP_K)
    gates = jax.nn.softmax(top_v, axis=-1)
    flat_e = top_e.reshape(-1)
    flat_g = gates.reshape(-1)
    flat_tok = jnp.arange(n_slot, dtype=jnp.int32) // TOP_K
    order = jnp.argsort(flat_e)
    se = flat_e[order]
    counts = jnp.bincount(flat_e, length=N_EXPERTS)
    starts = jnp.cumsum(counts) - counts
    padded = (counts + MOE_BLOCK - 1) // MOE_BLOCK * MOE_BLOCK
    pad_end = jnp.cumsum(padded)
    pad_start = pad_end - padded
    dest = pad_start[se] + jnp.arange(n_slot) - starts[se]
    n_blocks = -(-(n_slot + N_EXPERTS * (MOE_BLOCK - 1)) // MOE_BLOCK)
    n_pad = n_blocks * MOE_BLOCK
    slot_tok = jnp.zeros((n_pad,), jnp.int32).at[dest].set(flat_tok[order])
    slot_w = jnp.zeros((n_pad,), jnp.float32).at[dest].set(flat_g[order])
    block_e = jnp.minimum(jnp.searchsorted(pad_end, jnp.arange(n_blocks) * MOE_BLOCK, side='right'), N_EXPERTS - 1)
    xs = xt[slot_tok].reshape(n_blocks, MOE_BLOCK, D)

    def expert_block(args):
        xb, e = args
        gu = xb @ w_gate_up[e] + b_gate_up[e]
        g = jnp.minimum(gu[:, :D_FF], SWIGLU_LIMIT)
        u = jnp.clip(gu[:, D_FF:], -SWIGLU_LIMIT, SWIGLU_LIMIT)
        act = (u + 1.0) * (g * jax.nn.sigmoid(SWIGLU_ALPHA * g))
        return act @ w_down[e] + b_down[e]

    ys = lax.map(expert_block, (xs, block_e)).reshape(n_pad, D)
    out = jnp.zeros((n_tok, D), h.dtype).at[slot_tok].add((ys * slot_w[:, None]).astype(h.dtype))
    return out.reshape(B, T, D)


def _layer(x, c, past, w_ada, b_ada, g_pre_mix, g_post_mix, g_pre_ff, g_post_ff,
           w_in, w_gla_gate, b_gla_gate, g_gla_norm, w_conv, b_conv, w_rg_a, b_rg_a,
           w_rg_x, b_rg_x, lru_lambda, w_out, w_router, b_router, w_gate_up, b_gate_up,
           w_down, b_down):
    mod = jax.nn.silu(c) @ w_ada + b_ada
    sh_m, sc_m, gt_m, sh_f, sc_f, gt_f = [m[:, None, :] for m in jnp.split(mod, 6, axis=-1)]
    h = _rmsnorm(x, g_pre_mix) * (1.0 + sc_m) + sh_m
    y, new_state = _mixer(h, past, w_in, w_gla_gate, b_gla_gate, g_gla_norm, w_conv, b_conv,
                          w_rg_a, b_rg_a, w_rg_x, b_rg_x, lru_lambda, w_out)
    x = x + gt_m * _rmsnorm(y, g_post_mix)
    h = _rmsnorm(x, g_pre_ff) * (1.0 + sc_f) + sh_f
    y = _moe(h, w_router, b_router, w_gate_up, b_gate_up, w_down, b_down)
    x = x + gt_f * _rmsnorm(y, g_post_ff)
    return x, new_state


def setup_inputs(seed: int = 0) -> dict:
    key = jax.random.key(seed)
    ks = jax.random.split(key, 40)
    D = D_MODEL

    def nrm(i, shape, s):
        return jax.random.normal(ks[i], shape, jnp.float32) * s

    a0 = jax.random.uniform(ks[33], (DEPTH, W_C), jnp.float32, 0.9, 0.999) ** (1.0 / RG_C)
    return {
        'x_prompt': nrm(0, (BATCH, SEQ, D), 1.0),
        'x_sample': nrm(1, (DEC_BATCH, DEC_SEQ, D), 1.0),
        'c_prompt': nrm(2, (BATCH, D), 1.0),
        'c_sample': nrm(3, (DEC_BATCH, D), 1.0),
        'cache_k_sb': nrm(4, (DEPTH, DEC_BATCH, PAST_LEN, H_A, HD_A), 1.0),
        'cache_v_sb': nrm(5, (DEPTH, DEC_BATCH, PAST_LEN, H_A, HD_A), 1.0),
        'state_gla': nrm(6, (DEPTH, DEC_BATCH, H_B, DK_B, DV_B), 0.5),
        'state_conv': nrm(7, (DEPTH, DEC_BATCH, CONV_W - 1, W_C), 1.0),
        'state_lru': nrm(8, (DEPTH, DEC_BATCH, W_C), 0.5),
        'w_ada': nrm(9, (DEPTH, D, 6 * D), 0.5 * D ** -0.5),
        'b_ada': nrm(10, (DEPTH, 6 * D), 0.02),
        'g_pre_mix': 1.0 + nrm(11, (DEPTH, D), 0.05),
        'g_post_mix': 1.0 + nrm(12, (DEPTH, D), 0.05),
        'g_pre_ff': 1.0 + nrm(13, (DEPTH, D), 0.05),
        'g_post_ff': 1.0 + nrm(14, (DEPTH, D), 0.05),
        'w_in': nrm(15, (DEPTH, D, IN_WIDTH), D ** -0.5),
        'w_gla_gate': nrm(16, (DEPTH, GATE_RANK_B, H_B * DK_B), GATE_RANK_B ** -0.5),
        'b_gla_gate': nrm(17, (DEPTH, H_B * DK_B), 0.02),
        'g_gla_norm': 1.0 + nrm(18, (DEPTH, DV_B), 0.05),
        'w_conv': nrm(19, (DEPTH, CONV_W, W_C), CONV_W ** -0.5),
        'b_conv': nrm(20, (DEPTH, W_C), 0.02),
        'w_rg_a': nrm(21, (DEPTH, NBLK_C, BW_C, BW_C), BW_C ** -0.5),
        'b_rg_a': nrm(22, (DEPTH, W_C), 0.02),
        'w_rg_x': nrm(23, (DEPTH, NBLK_C, BW_C, BW_C), BW_C ** -0.5),
        'b_rg_x': nrm(24, (DEPTH, W_C), 0.02),
        'lru_lambda': jnp.log(a0) - jnp.log1p(-a0),
        'w_out': nrm(25, (DEPTH, D_MIX, D), D_MIX ** -0.5),
        'w_router': nrm(26, (DEPTH, D, N_EXPERTS), D ** -0.5),
        'b_router': nrm(27, (DEPTH, N_EXPERTS), 0.01),
        'w_gate_up': nrm(28, (DEPTH, N_EXPERTS, D, 2 * D_FF), D ** -0.5),
        'b_gate_up': nrm(29, (DEPTH, N_EXPERTS, 2 * D_FF), 0.02),
        'w_down': nrm(30, (DEPTH, N_EXPERTS, D_FF, D), D_FF ** -0.5),
        'b_down': nrm(31, (DEPTH, N_EXPERTS, D), 0.02),
    }


def reference(x_prompt, x_sample, c_prompt, c_sample, cache_k_sb, cache_v_sb, state_gla,
              state_conv, state_lru, w_ada, b_ada, g_pre_mix, g_post_mix, g_pre_ff, g_post_ff,
              w_in, w_gla_gate, b_gla_gate, g_gla_norm, w_conv, b_conv, w_rg_a, b_rg_a,
              w_rg_x, b_rg_x, lru_lambda, w_out, w_router, b_router, w_gate_up, b_gate_up,
              w_down, b_down):
    xp = x_prompt
    xs = x_sample
    bp = x_prompt.shape[0]
    st_p = [[], [], [], [], []]
    st_s = [[], [], [], [], []]
    for l in range(DEPTH):
        lw = (w_ada[l], b_ada[l], g_pre_mix[l], g_post_mix[l], g_pre_ff[l], g_post_ff[l],
              w_in[l], w_gla_gate[l], b_gla_gate[l], g_gla_norm[l], w_conv[l], b_conv[l],
              w_rg_a[l], b_rg_a[l], w_rg_x[l], b_rg_x[l], lru_lambda[l], w_out[l],
              w_router[l], b_router[l], w_gate_up[l], b_gate_up[l], w_down[l], b_down[l])
        past_p = (None, None,
                  jnp.zeros((bp, H_B, DK_B, DV_B), x_prompt.dtype),
                  jnp.zeros((bp, CONV_W - 1, W_C), x_prompt.dtype),
                  jnp.zeros((bp, W_C), x_prompt.dtype))
        past_s = (cache_k_sb[l], cache_v_sb[l], state_gla[l], state_conv[l], state_lru[l])
        xp, new_p = _layer(xp, c_prompt, past_p, *lw)
        xs, new_s = _layer(xs, c_sample, past_s, *lw)
        for j in range(5):
            st_p[j].append(new_p[j])
            st_s[j].append(new_s[j])
    new_k_sb_prompt = jnp.stack(st_p[0])
    new_v_sb_prompt = jnp.stack(st_p[1])
    new_gla_prompt = jnp.stack(st_p[2])
    new_conv_prompt = jnp.stack(st_p[3])
    new_lru_prompt = jnp.stack(st_p[4])
    new_k_sb_sample = jnp.stack(st_s[0])
    new_v_sb_sample = jnp.stack(st_s[1])
    new_gla_sample = jnp.stack(st_s[2])
    new_conv_sample = jnp.stack(st_s[3])
    new_lru_sample = jnp.stack(st_s[4])
    return (xp, xs, new_k_sb_prompt, new_v_sb_prompt, new_gla_prompt, new_conv_prompt, new_lru_prompt,
            new_k_sb_sample, new_v_sb_sample, new_gla_sample, new_conv_sample, new_lru_sample)
```

```python
import functools
import math

import numpy as np
import jax
import jax.numpy as jnp
from jax import lax
from jax.experimental import pallas as pl
from jax.experimental.pallas import tpu as pltpu

F32 = jnp.float32
BF16 = jnp.bfloat16

D_MODEL = 1024
EPS = 1e-6
HD_A = 64
W_A = 384
H_B = 4
DK_B = 48
DV_B = 96
W_B = 384
GATE_RANK_B = 16
GATE_TEMP_B = 16.0
W_C = 256
NBLK_C = 4
BW_C = 64
CONV_W = 4
RG_C = 8.0
N_EXPERTS = 32
TOP_K = 4
D_FF = 1024
SWIGLU_ALPHA = 1.702
SWIGLU_LIMIT = 7.0

LANE = 128
SUBLANE = 8
VMEM_LIMIT = 56 * 1024 * 1024

HP_B = H_B * LANE
OFF_GLA = 3 * W_A
GLA_W = 4 * HP_B + LANE
OFF_LRU = OFF_GLA + GLA_W
IN_PAD = OFF_LRU + 2 * W_C

TOKEN_TILE = 256
ATTN_TILE = 256
GLA_CHUNK = 64
LRU_CHUNK = 256
MOE_BLOCK_ROWS = 256


def _cparams(sem):
    return pltpu.CompilerParams(dimension_semantics=sem, vmem_limit_bytes=VMEM_LIMIT)


def _sigmoid(x):
    return 1.0 / (1.0 + jnp.exp(-x))


def _log_sigmoid(x):
    return jnp.minimum(x, 0.0) - jnp.log(1.0 + jnp.exp(-jnp.abs(x)))


def _split_bf16(x):
    hi = x.astype(BF16)
    lo = (x - hi.astype(F32)).astype(BF16)
    return hi, lo


def _dot(a, b):
    return jnp.dot(a, b, preferred_element_type=F32)


def _dot_nt(a, b):
    return lax.dot_general(a, b, (((1,), (1,)), ((), ())), preferred_element_type=F32)


def _rms(x, g):
    return x * lax.rsqrt(jnp.mean(x * x, axis=-1, keepdims=True) + EPS) * g


def _mod_kernel(c_ref, w_ref, b_ref, o_ref):
    c = c_ref[...]
    a = (c * _sigmoid(c)).astype(BF16)
    o_ref[0] = _dot(a, w_ref[0].astype(BF16)) + b_ref[0]


def _modulation(c_all, w_ada, b_ada):
    depth, d, n = w_ada.shape
    r = c_all.shape[0]
    tn = 1536
    return pl.pallas_call(
        _mod_kernel,
        grid=(depth, n // tn),
        in_specs=[pl.BlockSpec((r, d), lambda l, j: (0, 0)),
                  pl.BlockSpec((1, d, tn), lambda l, j: (l, 0, j)),
                  pl.BlockSpec((1, 1, tn), lambda l, j: (l, 0, j))],
        out_specs=pl.BlockSpec((1, r, tn), lambda l, j: (l, 0, j)),
        out_shape=jax.ShapeDtypeStruct((depth, r, n), F32),
        compiler_params=_cparams(("arbitrary", "arbitrary")),
        name="adaln_mod",
    )(c_all, w_ada, b_ada.reshape(depth, 1, n))


def _premix_kernel(x_ref, g_ref, sc_ref, sh_ref, w_ref,
                   ka_ref, va_ref, qab_ref, kab_ref, vab_ref, gla_ref, lru_ref):
    h = _rms(x_ref[...], g_ref[...]) * (1.0 + sc_ref[0]) + sh_ref[0]
    r = _dot(h.astype(BF16), w_ref[...])
    ka = r[:, W_A:2 * W_A]
    va = r[:, 2 * W_A:3 * W_A]
    ka_ref[...] = ka
    va_ref[...] = va
    qab_ref[...] = (r[:, 0:W_A] * (HD_A ** -0.5)).astype(BF16)
    kab_ref[...] = ka.astype(BF16)
    vab_ref[...] = va.astype(BF16)
    gla_ref[...] = r[:, OFF_GLA:OFF_LRU]
    lru_ref[...] = r[:, OFF_LRU:IN_PAD]


def _premix(x2, g, sc, sh, w_in_p, seq, tm):
    n, d = x2.shape
    per_b = seq // tm
    tok = lambda i: (i, 0)
    bat = lambda i: (i // per_b, 0, 0)
    const = lambda i: (0, 0)
    outs = [((n, W_A), F32), ((n, W_A), F32), ((n, W_A), BF16), ((n, W_A), BF16), ((n, W_A), BF16),
            ((n, GLA_W), F32), ((n, 2 * W_C), F32)]
    return pl.pallas_call(
        _premix_kernel,
        grid=(n // tm,),
        in_specs=[pl.BlockSpec((tm, d), tok), pl.BlockSpec((1, d), const),
                  pl.BlockSpec((1, 1, d), bat), pl.BlockSpec((1, 1, d), bat),
                  pl.BlockSpec((d, IN_PAD), const)],
        out_specs=[pl.BlockSpec((tm, s[1]), tok) for s, _ in outs],
        out_shape=[jax.ShapeDtypeStruct(s, t) for s, t in outs],
        compiler_params=_cparams(("arbitrary",)),
        name="premix_proj",
    )(x2, g.reshape(1, d), sc, sh, w_in_p)


def _attn_kernel(q_ref, k_ref, v_ref, u_ref, o_ref, acc_ref, c_ref, *, tq, tk, off):
    i = pl.program_id(2)
    lane = lax.broadcasted_iota(jnp.int32, (1, LANE), 1)
    q = q_ref[0]
    qz = jnp.zeros_like(q)
    head_lanes = (lane < HD_A, lane >= HD_A)
    qh = tuple(jnp.where(m, q, qz) for m in head_lanes)
    acc_ref[...] = jnp.zeros_like(acc_ref)
    c_ref[...] = jnp.zeros_like(c_ref)
    row = lax.broadcasted_iota(jnp.int32, (tq, tk), 0)
    col = lax.broadcasted_iota(jnp.int32, (tq, tk), 1)
    qpos0 = off + i * tq
    n_keys = qpos0 + tq - 1
    nk = (n_keys + tk - 1) // tk
    u = u_ref[...]

    def body(jj, carry):
        j = nk - 1 - jj
        ks = pl.multiple_of(j * tk, tk)
        kb = k_ref[0, pl.ds(ks, tk), :]
        vb = v_ref[0, pl.ds(ks, tk), :]
        vz = jnp.zeros_like(vb)
        mask = (col + ks) < (row + qpos0)
        for h in range(2):
            s = _dot_nt(qh[h], kb)
            l1p = jnp.log(1.0 + jnp.exp(-jnp.abs(s)))
            lf = jnp.where(mask, -(jnp.maximum(s, 0.0) + l1p), 0.0)
            lb = jnp.minimum(s, 0.0) - l1p
            hi, lo = _split_bf16(lf)
            cr = _dot(hi, u) + _dot(lo, u)
            c = c_ref[h]
            cfull = jnp.concatenate([c] * (tk // LANE), axis=1)
            w = jnp.where(mask, jnp.exp(lb + cr[:, :tk] + cfull), 0.0)
            vh = jnp.where(head_lanes[h], vb, vz)
            acc_ref[...] += _dot(w.astype(BF16), vh)
            c_ref[h] = c + cr[:, tk:]
        return carry

    lax.fori_loop(0, nk, body, 0)
    o_ref[0] = acc_ref[...].astype(o_ref.dtype)


def _attn_umat(tk):
    jp = np.arange(tk)[:, None]
    j = np.arange(tk)[None, :]
    u = np.concatenate([(jp > j).astype(np.float32), np.ones((tk, LANE), np.float32)], axis=1)
    return jnp.asarray(u, BF16)


def _attention(q, k, v, off):
    b, tq_all, _ = q.shape
    tk_all = k.shape[1]
    tq = min(ATTN_TILE, tq_all)
    tk = ATTN_TILE
    assert tk_all % tk == 0 and tq_all % tq == 0 and off + tq_all - 1 <= tk_all
    kern = functools.partial(_attn_kernel, tq=tq, tk=tk, off=off)
    return pl.pallas_call(
        kern,
        grid=(b, W_A // LANE, tq_all // tq),
        in_specs=[pl.BlockSpec((1, tq, LANE), lambda bi, hp, i: (bi, i, hp)),
                  pl.BlockSpec((1, tk_all, LANE), lambda bi, hp, i: (bi, 0, hp)),
                  pl.BlockSpec((1, tk_all, LANE), lambda bi, hp, i: (bi, 0, hp)),
                  pl.BlockSpec((tk, tk + LANE), lambda bi, hp, i: (0, 0))],
        out_specs=pl.BlockSpec((1, tq, LANE), lambda bi, hp, i: (bi, i, hp)),
        out_shape=jax.ShapeDtypeStruct((b, tq_all, W_A), BF16),
        scratch_shapes=[pltpu.VMEM((tq, LANE), F32), pltpu.VMEM((2, tq, LANE), F32)],
        compiler_params=_cparams(("arbitrary", "arbitrary", "arbitrary")),
        name="stick_breaking_attn",
    )(q, k, v, _attn_umat(tk))


def _gla_levels(c):
    return int(math.log2(c))


def _gla_mats(c):
    t = np.arange(c)[:, None]
    s = np.arange(c)[None, :]
    mats = [(s <= t).astype(np.float32), (s > t).astype(np.float32)]
    for lv in range(_gla_levels(c)):
        m = 1 << lv
        ref = (t // (2 * m)) * (2 * m) + m - 1
        mats.append(((s > ref) & (s <= t)).astype(np.float32) - ((s > t) & (s <= ref)).astype(np.float32))
    return jnp.asarray(np.concatenate(mats, axis=0), BF16)


def _gla_kernel(in_ref, wg_ref, bg_ref, gn_ref, mall_ref, s0_ref, ob_ref, sout_ref, st_ref, *, C):
    ci = pl.program_id(1)

    @pl.when(ci == 0)
    def _():
        st_ref[...] = s0_ref[0]

    q = in_ref[:, 0:HP_B] * (DK_B ** -0.5)
    k = in_ref[:, HP_B:2 * HP_B]
    v = in_ref[:, 2 * HP_B:3 * HP_B]
    gb = in_ref[:, 3 * HP_B:4 * HP_B]
    rb = in_ref[:, 4 * HP_B:4 * HP_B + LANE]
    lg = _log_sigmoid(_dot(rb.astype(BF16), wg_ref[...]) + bg_ref[...]) * (1.0 / GATE_TEMP_B)
    hi, lo = _split_bf16(lg)
    mall = mall_ref[...]
    dall = _dot(mall, hi) + _dot(mall, lo)
    eb = jnp.exp(dall[0:C])
    elast = jnp.exp(dall[C:2 * C])
    eb_last = eb[C - 1:C, :]
    rowi = lax.broadcasted_iota(jnp.int32, (C, 1), 0)
    row = lax.broadcasted_iota(jnp.int32, (C, C), 0)
    col = lax.broadcasted_iota(jnp.int32, (C, C), 1)
    qe = (q * eb).astype(BF16)
    ke = (k * elast).astype(BF16)
    qb = q.astype(BF16)
    kb = k.astype(BF16)
    vb = v.astype(BF16)
    zero = jnp.zeros_like(q)
    lv_q, lv_k = [], []
    for lv in range(_gla_levels(C)):
        e = jnp.exp(-jnp.abs(dall[(2 + lv) * C:(3 + lv) * C]))
        second = ((rowi >> lv) & 1) == 1
        lv_q.append(jnp.where(second, q * e, zero).astype(BF16))
        lv_k.append(jnp.where(second, zero, k * e).astype(BF16))
    outs = []
    for h in range(H_B):
        sl = slice(h * LANE, (h + 1) * LANE)
        att = jnp.where(row == col, _dot_nt(qb[:, sl], kb[:, sl]), 0.0)
        for lv in range(_gla_levels(C)):
            same = (row >> (lv + 1)) == (col >> (lv + 1))
            att = att + jnp.where(same, _dot_nt(lv_q[lv][:, sl], lv_k[lv][:, sl]), 0.0)
        st = st_ref[h]
        o = _dot(att.astype(BF16), vb[:, sl]) + _dot_nt(qe[:, sl], st.astype(BF16))
        st_ref[h] = st * eb_last[:, sl] + _dot(v[:, sl].T.astype(BF16), ke[:, sl])
        ms = jnp.sum(o * o, axis=-1, keepdims=True) * (1.0 / DV_B)
        on = o * lax.rsqrt(ms + EPS) * gn_ref[:, sl]
        g = gb[:, sl]
        outs.append((on * (g * _sigmoid(g))).astype(BF16))
    ob_ref[...] = jnp.concatenate(outs, axis=1)

    @pl.when(ci == pl.num_programs(1) - 1)
    def _():
        sout_ref[0] = st_ref[...]


def _gla(gla_in, wg_p, bg_p, gn_p, s0t, seq):
    n = gla_in.shape[0]
    b = n // seq
    c = min(GLA_CHUNK, seq)
    per_b = seq // c
    mall = _gla_mats(c)
    const2 = lambda bi, ci: (0, 0)
    return pl.pallas_call(
        functools.partial(_gla_kernel, C=c),
        grid=(b, per_b),
        in_specs=[pl.BlockSpec((c, GLA_W), lambda bi, ci: (bi * per_b + ci, 0)),
                  pl.BlockSpec((LANE, HP_B), const2), pl.BlockSpec((1, HP_B), const2),
                  pl.BlockSpec((1, HP_B), const2), pl.BlockSpec(mall.shape, const2),
                  pl.BlockSpec((1, H_B, LANE, LANE), lambda bi, ci: (bi, 0, 0, 0))],
        out_specs=[pl.BlockSpec((c, HP_B), lambda bi, ci: (bi * per_b + ci, 0)),
                   pl.BlockSpec((1, H_B, LANE, LANE), lambda bi, ci: (bi, 0, 0, 0))],
        out_shape=[jax.ShapeDtypeStruct((n, HP_B), BF16),
                   jax.ShapeDtypeStruct((b, H_B, LANE, LANE), F32)],
        scratch_shapes=[pltpu.VMEM((H_B, LANE, LANE), F32)],
        compiler_params=_cparams(("arbitrary", "arbitrary")),
        name="gla_chunked",
    )(gla_in, wg_p, bg_p, gn_p, mall, s0t)


def _lru_kernel(in_ref, cw_ref, cb_ref, wa_ref, ba_ref, wx_ref, bx_ref, lam_ref, buf0_ref, h0_ref,
                oc_ref, conv_ref, hout_ref, xp_ref, hc_ref, *, C):
    ci = pl.program_id(1)

    @pl.when(ci == 0)
    def _():
        xp_ref[0:SUBLANE] = buf0_ref[0]
        hc_ref[...] = h0_ref[0]

    x = in_ref[:, 0:W_C]
    y = in_ref[:, W_C:2 * W_C]
    xp_ref[SUBLANE:SUBLANE + C] = x
    xc = cb_ref[...]
    for j in range(CONV_W):
        xc = xc + xp_ref[pl.ds(SUBLANE - (CONV_W - 1) + j, C), :] * cw_ref[j:j + 1, :]
    xcb = xc.astype(BF16)
    r = _sigmoid(_dot(xcb, wa_ref[...]) + ba_ref[...])
    gi = _sigmoid(_dot(xcb, wx_ref[...]) + bx_ref[...])
    log_a = RG_C * r * _log_sigmoid(lam_ref[...])
    a = jnp.exp(log_a)
    u = jnp.sqrt(1.0 - jnp.exp(2.0 * log_a)) * (gi * xc)
    rowi = lax.broadcasted_iota(jnp.int32, (C, 1), 0)
    d = 1
    while d < C:
        keep = rowi >= d
        a_s = pltpu.roll(a, d, axis=0)
        u_s = pltpu.roll(u, d, axis=0)
        u = jnp.where(keep, a * u_s + u, u)
        a = jnp.where(keep, a * a_s, a)
        d *= 2
    hseq = u + a * hc_ref[...]
    hc_ref[...] = hseq[C - 1:C, :]
    gelu = 0.5 * y * (1.0 + jnp.tanh(math.sqrt(2.0 / math.pi) * (y + 0.044715 * (y * y * y))))
    oc_ref[...] = (hseq * gelu).astype(BF16)
    tail = xp_ref[C:C + SUBLANE]
    xp_ref[0:SUBLANE] = tail

    @pl.when(ci == pl.num_programs(1) - 1)
    def _():
        conv_ref[0] = tail
        hout_ref[0] = hseq[C - 1:C, :]


def _lru(lru_in, cw, cb, wa_bd, ba, wx_bd, bx, lam, buf0, h0, seq):
    n = lru_in.shape[0]
    b = n // seq
    c = min(LRU_CHUNK, seq)
    per_b = seq // c
    const2 = lambda bi, ci: (0, 0)
    vec = pl.BlockSpec((1, W_C), const2)
    return pl.pallas_call(
        functools.partial(_lru_kernel, C=c),
        grid=(b, per_b),
        in_specs=[pl.BlockSpec((c, 2 * W_C), lambda bi, ci: (bi * per_b + ci, 0)),
                  pl.BlockSpec((CONV_W, W_C), const2), vec,
                  pl.BlockSpec((W_C, W_C), const2), vec, pl.BlockSpec((W_C, W_C), const2), vec, vec,
                  pl.BlockSpec((1, SUBLANE, W_C), lambda bi, ci: (bi, 0, 0)),
                  pl.BlockSpec((1, 1, W_C), lambda bi, ci: (bi, 0, 0))],
        out_specs=[pl.BlockSpec((c, W_C), lambda bi, ci: (bi * per_b + ci, 0)),
                   pl.BlockSpec((1, SUBLANE, W_C), lambda bi, ci: (bi, 0, 0)),
                   pl.BlockSpec((1, 1, W_C), lambda bi, ci: (bi, 0, 0))],
        out_shape=[jax.ShapeDtypeStruct((n, W_C), BF16),
                   jax.ShapeDtypeStruct((b, SUBLANE, W_C), F32),
                   jax.ShapeDtypeStruct((b, 1, W_C), F32)],
        scratch_shapes=[pltpu.VMEM((c + SUBLANE, W_C), F32), pltpu.VMEM((1, W_C), F32)],
        compiler_params=_cparams(("arbitrary", "arbitrary")),
        name="conv_rglru",
    )(lru_in, cw, cb, wa_bd, ba, wx_bd, bx, lam, buf0, h0)


def _postmix_kernel(x_ref, oa_ref, ob_ref, oc_ref, wa_ref, wb_ref, wc_ref, gpm_ref, gpf_ref,
                    gt_ref, sc_ref, sh_ref, wr_hi_ref, wr_lo_ref, br_ref, tri_ref, cnt0_ref,
                    xm_ref, h2_ref, ri_ref, gate_ref, cnt_ref, *, tm):
    i = pl.program_id(0)

    @pl.when(i == 0)
    def _():
        cnt_ref[...] = cnt0_ref[...]

    y = _dot(oa_ref[...], wa_ref[...]) + _dot(ob_ref[...], wb_ref[...]) + _dot(oc_ref[...], wc_ref[...])
    xm = x_ref[...] + gt_ref[0] * _rms(y, gpm_ref[...])
    xm_ref[...] = xm
    h2 = _rms(xm, gpf_ref[...]) * (1.0 + sc_ref[0]) + sh_ref[0]
    h2_ref[...] = h2
    hi, lo = _split_bf16(h2)
    wh = wr_hi_ref[...]
    logits = _dot(hi, wh) + _dot(lo, wh) + _dot(hi, wr_lo_ref[...]) + br_ref[...]
    lane = lax.broadcasted_iota(jnp.int32, (tm, LANE), 1)
    lane_f = lane.astype(F32)
    neg = jnp.float32(-jnp.inf)
    vals, hots = [], []
    for _ in range(TOP_K):
        m = jnp.max(logits, axis=-1, keepdims=True)
        idx = jnp.min(jnp.where(logits == m, lane_f, float(LANE)), axis=-1, keepdims=True)
        hot = lane_f == idx
        logits = jnp.where(hot, neg, logits)
        vals.append(m)
        hots.append(hot)
    ex = [jnp.exp(vk - vals[0]) for vk in vals]
    inv = 1.0 / (ex[0] + ex[1] + ex[2] + ex[3])
    sel = jnp.zeros((tm, LANE), F32)
    for hot in hots:
        sel = jnp.where(hot, 1.0, sel)
    cnt = cnt_ref[0:1, :]
    before = _dot(tri_ref[...], sel.astype(BF16)) + cnt
    ri = jnp.zeros((tm, LANE), jnp.int32)
    gates = jnp.zeros((tm, LANE), F32)
    for kk in range(TOP_K):
        e_k = jnp.sum(jnp.where(hots[kk], lane_f, 0.0), axis=-1, keepdims=True).astype(jnp.int32)
        r_k = jnp.sum(jnp.where(hots[kk], before, 0.0), axis=-1, keepdims=True).astype(jnp.int32)
        ri = jnp.where(lane == kk, e_k, ri)
        ri = jnp.where(lane == TOP_K + kk, r_k, ri)
        gates = jnp.where(lane == kk, ex[kk] * inv, gates)
    ri_ref[...] = ri
    gate_ref[...] = gates
    cnt_ref[...] = jnp.broadcast_to(cnt + jnp.sum(sel, axis=0, keepdims=True), (SUBLANE, LANE))


def _postmix(x2, oa, ob, oc, wo_a, wo_b, wo_c, gpm, gpf, gt, sc, sh, wr_hi, wr_lo, br, cnt0, seq, tm):
    n, d = x2.shape
    per_b = seq // tm
    tok = lambda i: (i, 0)
    bat = lambda i: (i // per_b, 0, 0)
    const = lambda i: (0, 0)
    tri = jnp.asarray(np.tril(np.ones((tm, tm), np.float32), -1), BF16)
    full = lambda a: pl.BlockSpec(a.shape, const)
    return pl.pallas_call(
        functools.partial(_postmix_kernel, tm=tm),
        grid=(n // tm,),
        in_specs=[pl.BlockSpec((tm, d), tok), pl.BlockSpec((tm, W_A), tok),
                  pl.BlockSpec((tm, HP_B), tok), pl.BlockSpec((tm, W_C), tok),
                  full(wo_a), full(wo_b), full(wo_c),
                  pl.BlockSpec((1, d), const), pl.BlockSpec((1, d), const),
                  pl.BlockSpec((1, 1, d), bat), pl.BlockSpec((1, 1, d), bat), pl.BlockSpec((1, 1, d), bat),
                  full(wr_hi), full(wr_lo), pl.BlockSpec((1, LANE), const), full(tri),
                  pl.BlockSpec((SUBLANE, LANE), const)],
        out_specs=[pl.BlockSpec((tm, d), tok), pl.BlockSpec((tm, d), tok),
                   pl.BlockSpec((tm, LANE), tok), pl.BlockSpec((tm, LANE), tok),
                   pl.BlockSpec((SUBLANE, LANE), const)],
        out_shape=[jax.ShapeDtypeStruct((n, d), F32), jax.ShapeDtypeStruct((n, d), F32),
                   jax.ShapeDtypeStruct((n, LANE), jnp.int32), jax.ShapeDtypeStruct((n, LANE), F32),
                   jax.ShapeDtypeStruct((SUBLANE, LANE), F32)],
        compiler_params=_cparams(("arbitrary",)),
        name="postmix_router",
    )(x2, oa, ob, oc, wo_a, wo_b, wo_c, gpm.reshape(1, d), gpf.reshape(1, d), gt, sc, sh,
      wr_hi, wr_lo, br, tri, cnt0)


def _dispatch_kernel(dest_ref, h_ref, xs_in_ref, xs_ref, sem, *, tm):
    del xs_in_ref
    base = pl.program_id(0) * tm

    def issue(t, carry):
        for kk in range(TOP_K):
            pltpu.make_async_copy(h_ref.at[pl.ds(base + t, 1)],
                                  xs_ref.at[pl.ds(dest_ref[0, 0, t * TOP_K + kk], 1)], sem).start()
        return carry

    lax.fori_loop(0, tm, issue, 0)

    def drain(t, carry):
        pltpu.make_async_copy(h_ref.at[pl.ds(0, 1)], xs_ref.at[pl.ds(0, 1)], sem).wait()
        return carry

    lax.fori_loop(0, tm * TOP_K, drain, 0)


def _dispatch(dest3, h2, n_pad, tm):
    n, d = h2.shape
    zeros = jnp.zeros((n_pad, d), F32)
    return pl.pallas_call(
        functools.partial(_dispatch_kernel, tm=tm),
        grid=(n // tm,),
        in_specs=[pl.BlockSpec((1, 1, tm * TOP_K), lambda i: (i, 0, 0), memory_space=pltpu.SMEM),
                  pl.BlockSpec(memory_space=pl.ANY), pl.BlockSpec(memory_space=pl.ANY)],
        out_specs=pl.BlockSpec(memory_space=pl.ANY),
        out_shape=jax.ShapeDtypeStruct((n_pad, d), F32),
        scratch_shapes=[pltpu.SemaphoreType.DMA(())],
        input_output_aliases={2: 0},
        compiler_params=_cparams(("arbitrary",)),
        name="moe_dispatch",
    )(dest3, h2, zeros)


def _expert_kernel(be_ref, nv_ref, x_ref, wgu_ref, bgu_ref, wd_ref, bd_ref, y_ref):
    g = pl.program_id(0)

    @pl.when(g < nv_ref[0])
    def _():
        gu = _dot(x_ref[...].astype(BF16), wgu_ref[0]) + bgu_ref[0]
        gt = jnp.minimum(gu[:, :D_FF], SWIGLU_LIMIT)
        up = jnp.clip(gu[:, D_FF:], -SWIGLU_LIMIT, SWIGLU_LIMIT)
        act = (up + 1.0) * (gt * _sigmoid(SWIGLU_ALPHA * gt))
        y_ref[...] = _dot(act.astype(BF16), wd_ref[0]) + bd_ref[0]

    @pl.when(g >= nv_ref[0])
    def _():
        y_ref[...] = jnp.zeros_like(y_ref)


def _experts(block_e, nvalid, xs, wgu, bgu, wd, bd):
    n_pad, d = xs.shape
    bm = MOE_BLOCK_ROWS
    nb = n_pad // bm
    grid_spec = pltpu.PrefetchScalarGridSpec(
        num_scalar_prefetch=2,
        grid=(nb,),
        in_specs=[pl.BlockSpec((bm, d), lambda g, be, nv: (g, 0)),
                  pl.BlockSpec((1, d, 2 * D_FF), lambda g, be, nv: (be[g], 0, 0)),
                  pl.BlockSpec((1, 1, 2 * D_FF), lambda g, be, nv: (be[g], 0, 0)),
                  pl.BlockSpec((1, D_FF, d), lambda g, be, nv: (be[g], 0, 0)),
                  pl.BlockSpec((1, 1, d), lambda g, be, nv: (be[g], 0, 0))],
        out_specs=pl.BlockSpec((bm, d), lambda g, be, nv: (g, 0)),
    )
    return pl.pallas_call(
        _expert_kernel,
        grid_spec=grid_spec,
        out_shape=jax.ShapeDtypeStruct((n_pad, d), F32),
        compiler_params=_cparams(("arbitrary",)),
        name="moe_experts",
    )(block_e, nvalid, xs, wgu, bgu.reshape(N_EXPERTS, 1, -1), wd, bd.reshape(N_EXPERTS, 1, -1))


def _combine_kernel(dest_ref, ys_ref, gate_ref, xm_ref, gt_ref, g_ref, o_ref, buf_ref, sem, *, tm):
    def issue(t, carry):
        for kk in range(TOP_K):
            pltpu.make_async_copy(ys_ref.at[pl.ds(dest_ref[0, 0, t * TOP_K + kk], 1)],
                                  buf_ref.at[kk, pl.ds(t, 1)], sem).start()
        return carry

    lax.fori_loop(0, tm, issue, 0)

    def drain(t, carry):
        pltpu.make_async_copy(ys_ref.at[pl.ds(0, 1)], buf_ref.at[0, pl.ds(0, 1)], sem).wait()
        return carry

    lax.fori_loop(0, tm * TOP_K, drain, 0)
    gates = gate_ref[...]
    y = gates[:, 0:1] * buf_ref[0]
    for kk in range(1, TOP_K):
        y = y + gates[:, kk:kk + 1] * buf_ref[kk]
    o_ref[...] = xm_ref[...] + gt_ref[0] * _rms(y, g_ref[...])


def _combine(dest3, ys, gates, xm, gt, g, seq, tm):
    n, d = xm.shape
    per_b = seq // tm
    tok = lambda i: (i, 0)
    return pl.pallas_call(
        functools.partial(_combine_kernel, tm=tm),
        grid=(n // tm,),
        in_specs=[pl.BlockSpec((1, 1, tm * TOP_K), lambda i: (i, 0, 0), memory_space=pltpu.SMEM),
                  pl.BlockSpec(memory_space=pl.ANY),
                  pl.BlockSpec((tm, LANE), tok), pl.BlockSpec((tm, d), tok),
                  pl.BlockSpec((1, 1, d), lambda i: (i // per_b, 0, 0)),
                  pl.BlockSpec((1, d), lambda i: (0, 0))],
        out_specs=pl.BlockSpec((tm, d), tok),
        out_shape=jax.ShapeDtypeStruct((n, d), F32),
        scratch_shapes=[pltpu.VMEM((TOP_K, tm, d), F32), pltpu.SemaphoreType.DMA(())],
        compiler_params=_cparams(("arbitrary",)),
        name="moe_combine",
    )(dest3, ys, gates, xm, gt, g.reshape(1, d))


def _pad_heads(w, width):
    lead = w.shape[:-1]
    w = w.reshape(lead + (H_B, width))
    w = jnp.pad(w, [(0, 0)] * len(lead) + [(0, 0), (0, LANE - width)])
    return w.reshape(lead + (HP_B,))


def _layer_weights(l, w_in, w_gla_gate, b_gla_gate, g_gla_norm, w_rg_a, w_rg_x, w_out, w_router, b_router):
    d = D_MODEL
    wi = w_in[l]
    o = 3 * W_A
    qb = wi[:, o:o + H_B * DK_B]
    kb = wi[:, o + 192:o + 384]
    vb = wi[:, o + 384:o + 768]
    gb = wi[:, o + 768:o + 1152]
    rb = wi[:, o + 1152:o + 1168]
    xy = wi[:, o + 1168:]
    w_in_p = jnp.concatenate(
        [wi[:, :o], _pad_heads(qb, DK_B), _pad_heads(kb, DK_B), _pad_heads(vb, DV_B), _pad_heads(gb, DV_B),
         jnp.pad(rb, ((0, 0), (0, LANE - GATE_RANK_B))), xy], axis=1).astype(BF16)
    wg_p = jnp.pad(_pad_heads(w_gla_gate[l], DK_B), ((0, LANE - GATE_RANK_B), (0, 0))).astype(BF16)
    bg_p = _pad_heads(b_gla_gate[l].reshape(1, -1), DK_B)
    gn_p = jnp.tile(jnp.pad(g_gla_norm[l], (0, LANE - DV_B)), H_B).reshape(1, HP_B)
    wa_bd = jax.scipy.linalg.block_diag(*[w_rg_a[l, i] for i in range(NBLK_C)]).astype(BF16)
    wx_bd = jax.scipy.linalg.block_diag(*[w_rg_x[l, i] for i in range(NBLK_C)]).astype(BF16)
    wo = w_out[l]
    wo_a = wo[:W_A].astype(BF16)
    wo_b = jnp.pad(wo[W_A:W_A + W_B].reshape(H_B, DV_B, d), ((0, 0), (0, LANE - DV_B), (0, 0)))
    wo_b = wo_b.reshape(HP_B, d).astype(BF16)
    wo_c = wo[W_A + W_B:].astype(BF16)
    wr = jnp.pad(w_router[l], ((0, 0), (0, LANE - N_EXPERTS)))
    wr_hi = wr.astype(BF16)
    wr_lo = (wr - wr_hi.astype(F32)).astype(BF16)
    br = jnp.pad(b_router[l], (0, LANE - N_EXPERTS), constant_values=-1e30).reshape(1, LANE)
    return dict(w_in_p=w_in_p, wg_p=wg_p, bg_p=bg_p, gn_p=gn_p, wa_bd=wa_bd, wx_bd=wx_bd,
                wo_a=wo_a, wo_b=wo_b, wo_c=wo_c, wr_hi=wr_hi, wr_lo=wr_lo, br=br)


def _pad_state(s):
    st = jnp.swapaxes(s, -1, -2)
    return jnp.pad(st, ((0, 0), (0, 0), (0, LANE - DV_B), (0, LANE - DK_B)))


def _unpad_state(st):
    return jnp.swapaxes(st[:, :, :DV_B, :DK_B], -1, -2)


def kernel(x_prompt, x_sample, c_prompt, c_sample, cache_k_sb, cache_v_sb, state_gla, state_conv, state_lru, w_ada, b_ada, g_pre_mix, g_post_mix, g_pre_ff, g_post_ff, w_in, w_gla_gate, b_gla_gate, g_gla_norm, w_conv, b_conv, w_rg_a, b_rg_a, w_rg_x, b_rg_x, lru_lambda, w_out, w_router, b_router, w_gate_up, b_gate_up, w_down, b_down):
    depth = w_ada.shape[0]
    d = D_MODEL
    groups = []
    for x, past in ((x_prompt, False), (x_sample, True)):
        b, t, _ = x.shape
        groups.append(dict(b=b, t=t, tm=min(TOKEN_TILE, t), n=b * t, past=past, x=x.reshape(b * t, d)))
    n_all = sum(g["n"] for g in groups)
    tm_moe = math.gcd(TOKEN_TILE, n_all)
    bm = MOE_BLOCK_ROWS
    nb = (n_all * TOP_K) // bm + N_EXPERTS
    n_pad = nb * bm

    nb_rows = sum(g["b"] for g in groups)
    c_all = jnp.concatenate([c_prompt, c_sample], axis=0)
    r_pad = -(-nb_rows // SUBLANE) * SUBLANE
    c_all = jnp.pad(c_all, ((0, r_pad - nb_rows), (0, 0)))
    mod = _modulation(c_all, w_ada, b_ada)

    wgu_bf = w_gate_up.astype(BF16)
    wd_bf = w_down.astype(BF16)
    st = [[[] for _ in range(5)] for _ in groups]
    for l in range(depth):
        lw = _layer_weights(l, w_in, w_gla_gate, b_gla_gate, g_gla_norm, w_rg_a, w_rg_x, w_out,
                            w_router, b_router)
        cnt = jnp.zeros((SUBLANE, LANE), F32)
        row0 = 0
        per_group = []
        for gi, g in enumerate(groups):
            b, t, tm, n = g["b"], g["t"], g["tm"], g["n"]
            m = mod[l, row0:row0 + b].reshape(b, 1, 6 * d)
            row0 += b
            sh_m, sc_m, gt_m, sh_f, sc_f, gt_f = [m[:, :, j * d:(j + 1) * d] for j in range(6)]
            ka, va, qab, kab, vab, gla_in, lru_in = _premix(g["x"], g_pre_mix[l], sc_m, sh_m, lw["w_in_p"], t, tm)
            q3 = qab.reshape(b, t, W_A)
            k3 = kab.reshape(b, t, W_A)
            v3 = vab.reshape(b, t, W_A)
            if g["past"]:
                n_past = cache_k_sb.shape[2]
                tk_all = -(-(n_past + t) // ATTN_TILE) * ATTN_TILE
                padk = ((0, 0), (0, tk_all - n_past - t), (0, 0))
                k3 = jnp.pad(jnp.concatenate([cache_k_sb[l].reshape(b, n_past, W_A).astype(BF16), k3], axis=1), padk)
                v3 = jnp.pad(jnp.concatenate([cache_v_sb[l].reshape(b, n_past, W_A).astype(BF16), v3], axis=1), padk)
                off = n_past
                s0 = state_gla[l]
                buf0 = state_conv[l]
                h0 = state_lru[l]
            else:
                off = 0
                s0 = jnp.zeros((b, H_B, DK_B, DV_B), F32)
                buf0 = jnp.zeros((b, CONV_W - 1, W_C), F32)
                h0 = jnp.zeros((b, W_C), F32)
            oa = _attention(q3, k3, v3, off).reshape(n, W_A)
            ob, s_new = _gla(gla_in, lw["wg_p"], lw["bg_p"], lw["gn_p"], _pad_state(s0), t)
            buf0p = jnp.pad(buf0, ((0, 0), (SUBLANE - (CONV_W - 1), 0), (0, 0)))
            oc, conv_new, h_new = _lru(lru_in, w_conv[l], b_conv[l].reshape(1, -1), lw["wa_bd"],
                                       b_rg_a[l].reshape(1, -1), lw["wx_bd"], b_rg_x[l].reshape(1, -1),
                                       lru_lambda[l].reshape(1, -1), buf0p, h0.reshape(b, 1, W_C), t)
            xm, h2, ri, gates, cnt = _postmix(g["x"], oa, ob, oc, lw["wo_a"], lw["wo_b"], lw["wo_c"],
                                              g_post_mix[l], g_pre_ff[l], gt_m, sc_f, sh_f,
                                              lw["wr_hi"], lw["wr_lo"], lw["br"], cnt, t, tm)
            per_group.append(dict(xm=xm, h2=h2, ri=ri, gates=gates, gt_f=gt_f))
            new = (ka.reshape(b, t, W_A // HD_A, HD_A), va.reshape(b, t, W_A // HD_A, HD_A),
                   _unpad_state(s_new), conv_new[:, SUBLANE - (CONV_W - 1):], h_new.reshape(b, W_C))
            for j in range(5):
                st[gi][j].append(new[j])

        counts = cnt[0, :N_EXPERTS].astype(jnp.int32)
        padded = (counts + bm - 1) // bm * bm
        pad_end = jnp.cumsum(padded)
        pad_start = pad_end - padded
        block_e = jnp.minimum(jnp.searchsorted(pad_end, jnp.arange(nb, dtype=jnp.int32) * bm, side="right"),
                              N_EXPERTS - 1).astype(jnp.int32)
        nvalid = (pad_end[-1] // bm).astype(jnp.int32).reshape(1)
        ri_all = jnp.concatenate([p["ri"] for p in per_group], axis=0)
        h2_all = jnp.concatenate([p["h2"] for p in per_group], axis=0)
        dest = pad_start[ri_all[:, :TOP_K]] + ri_all[:, TOP_K:2 * TOP_K]
        dest3 = dest.reshape(n_all // tm_moe, 1, tm_moe * TOP_K)
        xs = _dispatch(dest3, h2_all, n_pad, tm_moe)
        ys = _experts(block_e, nvalid, xs, wgu_bf[l], b_gate_up[l], wd_bf[l], b_down[l])
        tok0 = 0
        for gi, g in enumerate(groups):
            p = per_group[gi]
            tm, n = g["tm"], g["n"]
            dest_g = dest[tok0:tok0 + n].reshape(n // tm, 1, tm * TOP_K)
            tok0 += n
            g["x"] = _combine(dest_g, ys, p["gates"], p["xm"], p["gt_f"], g_post_ff[l], g["t"], tm)

    outs = [g["x"].reshape(g["b"], g["t"], d) for g in groups]
    for gi in range(len(groups)):
        outs.extend(jnp.stack(st[gi][j]) for j in range(5))
    return tuple(outs)
```

```python
import functools
import math

import numpy as np
import jax
import jax.numpy as jnp
from jax import lax
from jax.experimental import pallas as pl
from jax.experimental.pallas import tpu as pltpu

F32 = jnp.float32
BF16 = jnp.bfloat16

D_MODEL = 1024
EPS = 1e-6
HD_A = 64
W_A = 384
H_B = 4
DK_B = 48
DV_B = 96
W_B = 384
GATE_RANK_B = 16
GATE_TEMP_B = 16.0
W_C = 256
NBLK_C = 4
BW_C = 64
CONV_W = 4
RG_C = 8.0
N_EXPERTS = 32
TOP_K = 4
D_FF = 1024
SWIGLU_ALPHA = 1.702
SWIGLU_LIMIT = 7.0
LOG_WEIGHT_FLOOR = -120.0

LANE = 128
SUBLANE = 8
VMEM_LIMIT = 56 * 1024 * 1024

HP_B = H_B * LANE
OFF_GLA = 3 * W_A
GLA_W = 4 * HP_B + LANE
OFF_LRU = OFF_GLA + GLA_W
IN_PAD = OFF_LRU + 2 * W_C

TOKEN_TILE = 256
ATTN_TILE = 256
GLA_CHUNK = 64
LRU_CHUNK = 256
MOE_BLOCK_ROWS = 256


def _cparams(sem):
    return pltpu.CompilerParams(dimension_semantics=sem, vmem_limit_bytes=VMEM_LIMIT)


def _sigmoid(x):
    return 1.0 / (1.0 + jnp.exp(-x))


def _log_sigmoid(x):
    return jnp.minimum(x, 0.0) - jnp.log(1.0 + jnp.exp(-jnp.abs(x)))


def _split_bf16(x):
    hi = x.astype(BF16)
    lo = (x - hi.astype(F32)).astype(BF16)
    return hi, lo


def _dot(a, b):
    return jnp.dot(a, b, preferred_element_type=F32)


def _dot_nt(a, b):
    return lax.dot_general(a, b, (((1,), (1,)), ((), ())), preferred_element_type=F32)


def _rms(x, g):
    return x * lax.rsqrt(jnp.mean(x * x, axis=-1, keepdims=True) + EPS) * g


def _mod_kernel(c_ref, w_ref, b_ref, o_ref):
    c = c_ref[...]
    a = (c * _sigmoid(c)).astype(BF16)
    o_ref[0] = _dot(a, w_ref[0].astype(BF16)) + b_ref[0]


def _modulation(c_all, w_ada, b_ada):
    depth, d, n = w_ada.shape
    r = c_all.shape[0]
    tn = 1536
    return pl.pallas_call(
        _mod_kernel,
        grid=(depth, n // tn),
        in_specs=[pl.BlockSpec((r, d), lambda l, j: (0, 0)),
                  pl.BlockSpec((1, d, tn), lambda l, j: (l, 0, j)),
                  pl.BlockSpec((1, 1, tn), lambda l, j: (l, 0, j))],
        out_specs=pl.BlockSpec((1, r, tn), lambda l, j: (l, 0, j)),
        out_shape=jax.ShapeDtypeStruct((depth, r, n), F32),
        compiler_params=_cparams(("arbitrary", "arbitrary")),
        name="adaln_mod",
    )(c_all, w_ada, b_ada.reshape(depth, 1, n))


def _premix_kernel(x_ref, g_ref, sc_ref, sh_ref, w_ref,
                   ka_ref, va_ref, qab_ref, kab_ref, vab_ref, gla_ref, lru_ref):
    h = _rms(x_ref[...], g_ref[...]) * (1.0 + sc_ref[0]) + sh_ref[0]
    r = _dot(h.astype(BF16), w_ref[...])
    ka = r[:, W_A:2 * W_A]
    va = r[:, 2 * W_A:3 * W_A]
    ka_ref[...] = ka
    va_ref[...] = va
    qab_ref[...] = (r[:, 0:W_A] * (HD_A ** -0.5)).astype(BF16)
    kab_ref[...] = ka.astype(BF16)
    vab_ref[...] = va.astype(BF16)
    gla_ref[...] = r[:, OFF_GLA:OFF_LRU]
    lru_ref[...] = r[:, OFF_LRU:IN_PAD]


def _premix(x2, g, sc, sh, w_in_p, seq, tm):
    n, d = x2.shape
    per_b = seq // tm
    tok = lambda i: (i, 0)
    bat = lambda i: (i // per_b, 0, 0)
    const = lambda i: (0, 0)
    outs = [((n, W_A), F32), ((n, W_A), F32), ((n, W_A), BF16), ((n, W_A), BF16), ((n, W_A), BF16),
            ((n, GLA_W), F32), ((n, 2 * W_C), F32)]
    return pl.pallas_call(
        _premix_kernel,
        grid=(n // tm,),
        in_specs=[pl.BlockSpec((tm, d), tok), pl.BlockSpec((1, d), const),
                  pl.BlockSpec((1, 1, d), bat), pl.BlockSpec((1, 1, d), bat),
                  pl.BlockSpec((d, IN_PAD), const)],
        out_specs=[pl.BlockSpec((tm, s[1]), tok) for s, _ in outs],
        out_shape=[jax.ShapeDtypeStruct(s, t) for s, t in outs],
        compiler_params=_cparams(("arbitrary",)),
        name="premix_proj",
    )(x2, g.reshape(1, d), sc, sh, w_in_p)


def _attn_kernel(q_ref, k_ref, v_ref, u_ref, o_ref, acc_ref, c_ref, *, tq, tk, off):
    i = pl.program_id(2)
    lane = lax.broadcasted_iota(jnp.int32, (1, LANE), 1)
    q = q_ref[0]
    qz = jnp.zeros_like(q)
    head_lanes = (lane < HD_A, lane >= HD_A)
    qh = tuple(jnp.where(m, q, qz) for m in head_lanes)
    acc_ref[...] = jnp.zeros_like(acc_ref)
    c_ref[...] = jnp.zeros_like(c_ref)
    row = lax.broadcasted_iota(jnp.int32, (tq, tk), 0)
    col = lax.broadcasted_iota(jnp.int32, (tq, tk), 1)
    qpos0 = off + i * tq
    n_keys = qpos0 + tq - 1
    nk = (n_keys + tk - 1) // tk
    u = u_ref[...]

    def block(j, masked):
        ks = pl.multiple_of(j * tk, tk)
        kb = k_ref[0, pl.ds(ks, tk), :]
        vb = v_ref[0, pl.ds(ks, tk), :]
        vz = jnp.zeros_like(vb)
        mask = (col + ks) < (row + qpos0)
        for h in range(2):
            s = _dot_nt(qh[h], kb)
            l1p = jnp.log(1.0 + jnp.exp(-jnp.abs(s)))
            lf = -(jnp.maximum(s, 0.0) + l1p)
            if masked:
                lf = jnp.where(mask, lf, 0.0)
            lb = jnp.minimum(s, 0.0) - l1p
            hi, lo = _split_bf16(lf)
            cr = _dot(hi, u) + _dot(lo, u)
            c = c_ref[h]
            cfull = jnp.concatenate([c] * (tk // LANE), axis=1)
            w = jnp.exp(lb + cr[:, :tk] + cfull)
            if masked:
                w = jnp.where(mask, w, 0.0)
            vh = jnp.where(head_lanes[h], vb, vz)
            acc_ref[...] += _dot(w.astype(BF16), vh)
            c_ref[h] = c + cr[:, tk:]
        return jnp.max(c_ref[...]) > LOG_WEIGHT_FLOOR

    go = block(nk - 1, True)

    def body(carry):
        jj, _ = carry
        return jj + 1, block(nk - 1 - jj, False)

    lax.while_loop(lambda carry: jnp.logical_and(carry[0] < nk, carry[1]), body, (jnp.int32(1), go))
    o_ref[0] = acc_ref[...].astype(o_ref.dtype)


def _attn_umat(tk):
    jp = np.arange(tk)[:, None]
    j = np.arange(tk)[None, :]
    u = np.concatenate([(jp > j).astype(np.float32), np.ones((tk, LANE), np.float32)], axis=1)
    return jnp.asarray(u, BF16)


def _attention(q, k, v, off):
    b, tq_all, _ = q.shape
    tk_all = k.shape[1]
    tq = min(ATTN_TILE, tq_all)
    tk = ATTN_TILE
    assert tk_all % tk == 0 and tq_all % tq == 0 and off + tq_all - 1 <= tk_all
    assert off % tk == 0 and tk % tq == 0
    kern = functools.partial(_attn_kernel, tq=tq, tk=tk, off=off)
    return pl.pallas_call(
        kern,
        grid=(b, W_A // LANE, tq_all // tq),
        in_specs=[pl.BlockSpec((1, tq, LANE), lambda bi, hp, i: (bi, i, hp)),
                  pl.BlockSpec((1, tk_all, LANE), lambda bi, hp, i: (bi, 0, hp)),
                  pl.BlockSpec((1, tk_all, LANE), lambda bi, hp, i: (bi, 0, hp)),
                  pl.BlockSpec((tk, tk + LANE), lambda bi, hp, i: (0, 0))],
        out_specs=pl.BlockSpec((1, tq, LANE), lambda bi, hp, i: (bi, i, hp)),
        out_shape=jax.ShapeDtypeStruct((b, tq_all, W_A), BF16),
        scratch_shapes=[pltpu.VMEM((tq, LANE), F32), pltpu.VMEM((2, tq, LANE), F32)],
        compiler_params=_cparams(("arbitrary", "arbitrary", "arbitrary")),
        name="stick_breaking_attn",
    )(q, k, v, _attn_umat(tk))


def _gla_levels(c):
    return int(math.log2(c))


def _gla_mats(c):
    t = np.arange(c)[:, None]
    s = np.arange(c)[None, :]
    mats = [(s <= t).astype(np.float32), (s > t).astype(np.float32)]
    for lv in range(_gla_levels(c)):
        m = 1 << lv
        ref = (t // (2 * m)) * (2 * m) + m - 1
        mats.append(((s > ref) & (s <= t)).astype(np.float32) - ((s > t) & (s <= ref)).astype(np.float32))
    return jnp.asarray(np.concatenate(mats, axis=0), BF16)


def _gla_kernel(in_ref, wg_ref, bg_ref, gn_ref, mall_ref, s0_ref, ob_ref, sout_ref, st_ref, *, C):
    ci = pl.program_id(1)

    @pl.when(ci == 0)
    def _():
        st_ref[...] = s0_ref[0]

    q = in_ref[:, 0:HP_B] * (DK_B ** -0.5)
    k = in_ref[:, HP_B:2 * HP_B]
    v = in_ref[:, 2 * HP_B:3 * HP_B]
    gb = in_ref[:, 3 * HP_B:4 * HP_B]
    rb = in_ref[:, 4 * HP_B:4 * HP_B + LANE]
    lg = _log_sigmoid(_dot(rb.astype(BF16), wg_ref[...]) + bg_ref[...]) * (1.0 / GATE_TEMP_B)
    hi, lo = _split_bf16(lg)
    mall = mall_ref[...]
    dall = _dot(mall, hi) + _dot(mall, lo)
    eb = jnp.exp(dall[0:C])
    elast = jnp.exp(dall[C:2 * C])
    eb_last = eb[C - 1:C, :]
    rowi = lax.broadcasted_iota(jnp.int32, (C, 1), 0)
    row = lax.broadcasted_iota(jnp.int32, (C, C), 0)
    col = lax.broadcasted_iota(jnp.int32, (C, C), 1)
    qe = (q * eb).astype(BF16)
    ke = (k * elast).astype(BF16)
    qb = q.astype(BF16)
    kb = k.astype(BF16)
    vb = v.astype(BF16)
    zero = jnp.zeros_like(q)
    lv_q, lv_k = [], []
    for lv in range(_gla_levels(C)):
        e = jnp.exp(-jnp.abs(dall[(2 + lv) * C:(3 + lv) * C]))
        second = ((rowi >> lv) & 1) == 1
        lv_q.append(jnp.where(second, q * e, zero).astype(BF16))
        lv_k.append(jnp.where(second, zero, k * e).astype(BF16))
    outs = []
    for h in range(H_B):
        sl = slice(h * LANE, (h + 1) * LANE)
        att = jnp.where(row == col, _dot_nt(qb[:, sl], kb[:, sl]), 0.0)
        for lv in range(_gla_levels(C)):
            same = (row >> (lv + 1)) == (col >> (lv + 1))
            att = att + jnp.where(same, _dot_nt(lv_q[lv][:, sl], lv_k[lv][:, sl]), 0.0)
        st = st_ref[h]
        o = _dot(att.astype(BF16), vb[:, sl]) + _dot_nt(qe[:, sl], st.astype(BF16))
        st_ref[h] = st * eb_last[:, sl] + _dot(v[:, sl].T.astype(BF16), ke[:, sl])
        ms = jnp.sum(o * o, axis=-1, keepdims=True) * (1.0 / DV_B)
        on = o * lax.rsqrt(ms + EPS) * gn_ref[:, sl]
        g = gb[:, sl]
        outs.append((on * (g * _sigmoid(g))).astype(BF16))
    ob_ref[...] = jnp.concatenate(outs, axis=1)

    @pl.when(ci == pl.num_programs(1) - 1)
    def _():
        sout_ref[0] = st_ref[...]


def _gla(gla_in, wg_p, bg_p, gn_p, s0t, seq):
    n = gla_in.shape[0]
    b = n // seq
    c = min(GLA_CHUNK, seq)
    per_b = seq // c
    mall = _gla_mats(c)
    const2 = lambda bi, ci: (0, 0)
    return pl.pallas_call(
        functools.partial(_gla_kernel, C=c),
        grid=(b, per_b),
        in_specs=[pl.BlockSpec((c, GLA_W), lambda bi, ci: (bi * per_b + ci, 0)),
                  pl.BlockSpec((LANE, HP_B), const2), pl.BlockSpec((1, HP_B), const2),
                  pl.BlockSpec((1, HP_B), const2), pl.BlockSpec(mall.shape, const2),
                  pl.BlockSpec((1, H_B, LANE, LANE), lambda bi, ci: (bi, 0, 0, 0))],
        out_specs=[pl.BlockSpec((c, HP_B), lambda bi, ci: (bi * per_b + ci, 0)),
                   pl.BlockSpec((1, H_B, LANE, LANE), lambda bi, ci: (bi, 0, 0, 0))],
        out_shape=[jax.ShapeDtypeStruct((n, HP_B), BF16),
                   jax.ShapeDtypeStruct((b, H_B, LANE, LANE), F32)],
        scratch_shapes=[pltpu.VMEM((H_B, LANE, LANE), F32)],
        compiler_params=_cparams(("arbitrary", "arbitrary")),
        name="gla_chunked",
    )(gla_in, wg_p, bg_p, gn_p, mall, s0t)


def _lru_kernel(in_ref, cw_ref, cb_ref, wa_ref, ba_ref, wx_ref, bx_ref, lam_ref, buf0_ref, h0_ref,
                oc_ref, conv_ref, hout_ref, xp_ref, hc_ref, *, C):
    ci = pl.program_id(1)

    @pl.when(ci == 0)
    def _():
        xp_ref[0:SUBLANE] = buf0_ref[0]
        hc_ref[...] = h0_ref[0]

    x = in_ref[:, 0:W_C]
    y = in_ref[:, W_C:2 * W_C]
    xp_ref[SUBLANE:SUBLANE + C] = x
    xc = cb_ref[...]
    for j in range(CONV_W):
        xc = xc + xp_ref[pl.ds(SUBLANE - (CONV_W - 1) + j, C), :] * cw_ref[j:j + 1, :]
    xcb = xc.astype(BF16)
    r = _sigmoid(_dot(xcb, wa_ref[...]) + ba_ref[...])
    gi = _sigmoid(_dot(xcb, wx_ref[...]) + bx_ref[...])
    log_a = RG_C * r * _log_sigmoid(lam_ref[...])
    a = jnp.exp(log_a)
    u = jnp.sqrt(1.0 - jnp.exp(2.0 * log_a)) * (gi * xc)
    rowi = lax.broadcasted_iota(jnp.int32, (C, 1), 0)
    d = 1
    while d < C:
        keep = rowi >= d
        a_s = pltpu.roll(a, d, axis=0)
        u_s = pltpu.roll(u, d, axis=0)
        u = jnp.where(keep, a * u_s + u, u)
        a = jnp.where(keep, a * a_s, a)
        d *= 2
    hseq = u + a * hc_ref[...]
    hc_ref[...] = hseq[C - 1:C, :]
    gelu = 0.5 * y * (1.0 + jnp.tanh(math.sqrt(2.0 / math.pi) * (y + 0.044715 * (y * y * y))))
    oc_ref[...] = (hseq * gelu).astype(BF16)
    tail = xp_ref[C:C + SUBLANE]
    xp_ref[0:SUBLANE] = tail

    @pl.when(ci == pl.num_programs(1) - 1)
    def _():
        conv_ref[0] = tail
        hout_ref[0] = hseq[C - 1:C, :]


def _lru(lru_in, cw, cb, wa_bd, ba, wx_bd, bx, lam, buf0, h0, seq):
    n = lru_in.shape[0]
    b = n // seq
    c = min(LRU_CHUNK, seq)
    per_b = seq // c
    const2 = lambda bi, ci: (0, 0)
    vec = pl.BlockSpec((1, W_C), const2)
    return pl.pallas_call(
        functools.partial(_lru_kernel, C=c),
        grid=(b, per_b),
        in_specs=[pl.BlockSpec((c, 2 * W_C), lambda bi, ci: (bi * per_b + ci, 0)),
                  pl.BlockSpec((CONV_W, W_C), const2), vec,
                  pl.BlockSpec((W_C, W_C), const2), vec, pl.BlockSpec((W_C, W_C), const2), vec, vec,
                  pl.BlockSpec((1, SUBLANE, W_C), lambda bi, ci: (bi, 0, 0)),
                  pl.BlockSpec((1, 1, W_C), lambda bi, ci: (bi, 0, 0))],
        out_specs=[pl.BlockSpec((c, W_C), lambda bi, ci: (bi * per_b + ci, 0)),
                   pl.BlockSpec((1, SUBLANE, W_C), lambda bi, ci: (bi, 0, 0)),
                   pl.BlockSpec((1, 1, W_C), lambda bi, ci: (bi, 0, 0))],
        out_shape=[jax.ShapeDtypeStruct((n, W_C), BF16),
                   jax.ShapeDtypeStruct((b, SUBLANE, W_C), F32),
                   jax.ShapeDtypeStruct((b, 1, W_C), F32)],
        scratch_shapes=[pltpu.VMEM((c + SUBLANE, W_C), F32), pltpu.VMEM((1, W_C), F32)],
        compiler_params=_cparams(("arbitrary", "arbitrary")),
        name="conv_rglru",
    )(lru_in, cw, cb, wa_bd, ba, wx_bd, bx, lam, buf0, h0)


def _postmix_kernel(x_ref, oa_ref, ob_ref, oc_ref, wa_ref, wb_ref, wc_ref, gpm_ref, gpf_ref,
                    gt_ref, sc_ref, sh_ref, wr_hi_ref, wr_lo_ref, br_ref, tri_ref, cnt0_ref,
                    xm_ref, h2_ref, ri_ref, gate_ref, cnt_ref, *, tm):
    i = pl.program_id(0)

    @pl.when(i == 0)
    def _():
        cnt_ref[...] = cnt0_ref[...]

    y = _dot(oa_ref[...], wa_ref[...]) + _dot(ob_ref[...], wb_ref[...]) + _dot(oc_ref[...], wc_ref[...])
    xm = x_ref[...] + gt_ref[0] * _rms(y, gpm_ref[...])
    xm_ref[...] = xm
    h2 = _rms(xm, gpf_ref[...]) * (1.0 + sc_ref[0]) + sh_ref[0]
    h2_ref[...] = h2
    hi, lo = _split_bf16(h2)
    wh = wr_hi_ref[...]
    logits = _dot(hi, wh) + _dot(lo, wh) + _dot(hi, wr_lo_ref[...]) + br_ref[...]
    lane = lax.broadcasted_iota(jnp.int32, (tm, LANE), 1)
    lane_f = lane.astype(F32)
    neg = jnp.float32(-jnp.inf)
    vals, hots = [], []
    for _ in range(TOP_K):
        m = jnp.max(logits, axis=-1, keepdims=True)
        idx = jnp.min(jnp.where(logits == m, lane_f, float(LANE)), axis=-1, keepdims=True)
        hot = lane_f == idx
        logits = jnp.where(hot, neg, logits)
        vals.append(m)
        hots.append(hot)
    ex = [jnp.exp(vk - vals[0]) for vk in vals]
    inv = 1.0 / (ex[0] + ex[1] + ex[2] + ex[3])
    sel = jnp.zeros((tm, LANE), F32)
    for hot in hots:
        sel = jnp.where(hot, 1.0, sel)
    cnt = cnt_ref[0:1, :]
    before = _dot(tri_ref[...], sel.astype(BF16)) + cnt
    ri = jnp.zeros((tm, LANE), jnp.int32)
    gates = jnp.zeros((tm, LANE), F32)
    for kk in range(TOP_K):
        e_k = jnp.sum(jnp.where(hots[kk], lane_f, 0.0), axis=-1, keepdims=True).astype(jnp.int32)
        r_k = jnp.sum(jnp.where(hots[kk], before, 0.0), axis=-1, keepdims=True).astype(jnp.int32)
        ri = jnp.where(lane == kk, e_k, ri)
        ri = jnp.where(lane == TOP_K + kk, r_k, ri)
        gates = jnp.where(lane == kk, ex[kk] * inv, gates)
    ri_ref[...] = ri
    gate_ref[...] = gates
    cnt_ref[...] = jnp.broadcast_to(cnt + jnp.sum(sel, axis=0, keepdims=True), (SUBLANE, LANE))


def _postmix(x2, oa, ob, oc, wo_a, wo_b, wo_c, gpm, gpf, gt, sc, sh, wr_hi, wr_lo, br, cnt0, seq, tm):
    n, d = x2.shape
    per_b = seq // tm
    tok = lambda i: (i, 0)
    bat = lambda i: (i // per_b, 0, 0)
    const = lambda i: (0, 0)
    tri = jnp.asarray(np.tril(np.ones((tm, tm), np.float32), -1), BF16)
    full = lambda a: pl.BlockSpec(a.shape, const)
    return pl.pallas_call(
        functools.partial(_postmix_kernel, tm=tm),
        grid=(n // tm,),
        in_specs=[pl.BlockSpec((tm, d), tok), pl.BlockSpec((tm, W_A), tok),
                  pl.BlockSpec((tm, HP_B), tok), pl.BlockSpec((tm, W_C), tok),
                  full(wo_a), full(wo_b), full(wo_c),
                  pl.BlockSpec((1, d), const), pl.BlockSpec((1, d), const),
                  pl.BlockSpec((1, 1, d), bat), pl.BlockSpec((1, 1, d), bat), pl.BlockSpec((1, 1, d), bat),
                  full(wr_hi), full(wr_lo), pl.BlockSpec((1, LANE), const), full(tri),
                  pl.BlockSpec((SUBLANE, LANE), const)],
        out_specs=[pl.BlockSpec((tm, d), tok), pl.BlockSpec((tm, d), tok),
                   pl.BlockSpec((tm, LANE), tok), pl.BlockSpec((tm, LANE), tok),
                   pl.BlockSpec((SUBLANE, LANE), const)],
        out_shape=[jax.ShapeDtypeStruct((n, d), F32), jax.ShapeDtypeStruct((n, d), F32),
                   jax.ShapeDtypeStruct((n, LANE), jnp.int32), jax.ShapeDtypeStruct((n, LANE), F32),
                   jax.ShapeDtypeStruct((SUBLANE, LANE), F32)],
        compiler_params=_cparams(("arbitrary",)),
        name="postmix_router",
    )(x2, oa, ob, oc, wo_a, wo_b, wo_c, gpm.reshape(1, d), gpf.reshape(1, d), gt, sc, sh,
      wr_hi, wr_lo, br, tri, cnt0)


def _dispatch_kernel(dest_ref, h_ref, xs_in_ref, xs_ref, sem, *, tm):
    del xs_in_ref

    def issue(t, carry):
        for kk in range(TOP_K):
            pltpu.make_async_copy(h_ref.at[pl.ds(t, 1)],
                                  xs_ref.at[pl.ds(dest_ref[0, 0, t * TOP_K + kk], 1)], sem).start()
        return carry

    lax.fori_loop(0, tm, issue, 0)

    def drain(t, carry):
        pltpu.make_async_copy(h_ref.at[pl.ds(0, 1)], xs_ref.at[pl.ds(0, 1)], sem).wait()
        return carry

    lax.fori_loop(0, tm * TOP_K, drain, 0)


def _dispatch(dest3, h2, n_pad, tm):
    n, d = h2.shape
    zeros = jnp.zeros((n_pad, d), F32)
    return pl.pallas_call(
        functools.partial(_dispatch_kernel, tm=tm),
        grid=(n // tm,),
        in_specs=[pl.BlockSpec((1, 1, tm * TOP_K), lambda i: (i, 0, 0), memory_space=pltpu.SMEM),
                  pl.BlockSpec((tm, d), lambda i: (i, 0)), pl.BlockSpec(memory_space=pl.ANY)],
        out_specs=pl.BlockSpec(memory_space=pl.ANY),
        out_shape=jax.ShapeDtypeStruct((n_pad, d), F32),
        scratch_shapes=[pltpu.SemaphoreType.DMA(())],
        input_output_aliases={2: 0},
        compiler_params=_cparams(("arbitrary",)),
        name="moe_dispatch",
    )(dest3, h2, zeros)


def _expert_kernel(be_ref, nv_ref, x_ref, wgu_ref, bgu_ref, wd_ref, bd_ref, y_ref, wgu_bf, wd_bf):
    g = pl.program_id(0)

    @pl.when(jnp.logical_or(g == 0, be_ref[g] != be_ref[jnp.maximum(g - 1, 0)]))
    def _():
        wgu_bf[...] = wgu_ref[0].astype(BF16)
        wd_bf[...] = wd_ref[0].astype(BF16)

    @pl.when(g < nv_ref[0])
    def _():
        gu = _dot(x_ref[...].astype(BF16), wgu_bf[...]) + bgu_ref[0]
        gt = jnp.minimum(gu[:, :D_FF], SWIGLU_LIMIT)
        up = jnp.clip(gu[:, D_FF:], -SWIGLU_LIMIT, SWIGLU_LIMIT)
        act = (up + 1.0) * (gt * _sigmoid(SWIGLU_ALPHA * gt))
        y_ref[...] = _dot(act.astype(BF16), wd_bf[...]) + bd_ref[0]

    @pl.when(g >= nv_ref[0])
    def _():
        y_ref[...] = jnp.zeros_like(y_ref)


def _experts(block_e, nvalid, xs, wgu, bgu, wd, bd):
    n_pad, d = xs.shape
    bm = MOE_BLOCK_ROWS
    nb = n_pad // bm
    grid_spec = pltpu.PrefetchScalarGridSpec(
        num_scalar_prefetch=2,
        grid=(nb,),
        in_specs=[pl.BlockSpec((bm, d), lambda g, be, nv: (g, 0)),
                  pl.BlockSpec((1, d, 2 * D_FF), lambda g, be, nv: (be[g], 0, 0)),
                  pl.BlockSpec((1, 1, 2 * D_FF), lambda g, be, nv: (be[g], 0, 0)),
                  pl.BlockSpec((1, D_FF, d), lambda g, be, nv: (be[g], 0, 0)),
                  pl.BlockSpec((1, 1, d), lambda g, be, nv: (be[g], 0, 0))],
        out_specs=pl.BlockSpec((bm, d), lambda g, be, nv: (g, 0)),
        scratch_shapes=[pltpu.VMEM((d, 2 * D_FF), BF16), pltpu.VMEM((D_FF, d), BF16)],
    )
    return pl.pallas_call(
        _expert_kernel,
        grid_spec=grid_spec,
        out_shape=jax.ShapeDtypeStruct((n_pad, d), F32),
        compiler_params=_cparams(("arbitrary",)),
        name="moe_experts",
    )(block_e, nvalid, xs, wgu, bgu.reshape(N_EXPERTS, 1, -1), wd, bd.reshape(N_EXPERTS, 1, -1))


def _combine_kernel(dest_ref, ys_ref, gate_ref, xm_ref, gt_ref, g_ref, o_ref, buf_ref, sem, *, tm):
    def issue(t, carry):
        for kk in range(TOP_K):
            pltpu.make_async_copy(ys_ref.at[pl.ds(dest_ref[0, 0, t * TOP_K + kk], 1)],
                                  buf_ref.at[kk, pl.ds(t, 1)], sem).start()
        return carry

    lax.fori_loop(0, tm, issue, 0)

    def drain(t, carry):
        pltpu.make_async_copy(ys_ref.at[pl.ds(0, 1)], buf_ref.at[0, pl.ds(0, 1)], sem).wait()
        return carry

    lax.fori_loop(0, tm * TOP_K, drain, 0)
    gates = gate_ref[...]
    y = gates[:, 0:1] * buf_ref[0]
    for kk in range(1, TOP_K):
        y = y + gates[:, kk:kk + 1] * buf_ref[kk]
    o_ref[...] = xm_ref[...] + gt_ref[0] * _rms(y, g_ref[...])


def _combine(dest3, ys, gates, xm, gt, g, seq, tm):
    n, d = xm.shape
    per_b = seq // tm
    tok = lambda i: (i, 0)
    return pl.pallas_call(
        functools.partial(_combine_kernel, tm=tm),
        grid=(n // tm,),
        in_specs=[pl.BlockSpec((1, 1, tm * TOP_K), lambda i: (i, 0, 0), memory_space=pltpu.SMEM),
                  pl.BlockSpec(memory_space=pl.ANY),
                  pl.BlockSpec((tm, LANE), tok), pl.BlockSpec((tm, d), tok),
                  pl.BlockSpec((1, 1, d), lambda i: (i // per_b, 0, 0)),
                  pl.BlockSpec((1, d), lambda i: (0, 0))],
        out_specs=pl.BlockSpec((tm, d), tok),
        out_shape=jax.ShapeDtypeStruct((n, d), F32),
        scratch_shapes=[pltpu.VMEM((TOP_K, tm, d), F32), pltpu.SemaphoreType.DMA(())],
        compiler_params=_cparams(("arbitrary",)),
        name="moe_combine",
    )(dest3, ys, gates, xm, gt, g.reshape(1, d))


def _pad_heads(w, width):
    lead = w.shape[:-1]
    w = w.reshape(lead + (H_B, width))
    w = jnp.pad(w, [(0, 0)] * len(lead) + [(0, 0), (0, LANE - width)])
    return w.reshape(lead + (HP_B,))


def _layer_weights(l, w_in, w_gla_gate, b_gla_gate, g_gla_norm, w_rg_a, w_rg_x, w_out, w_router, b_router):
    d = D_MODEL
    wi = w_in[l]
    o = 3 * W_A
    qb = wi[:, o:o + H_B * DK_B]
    kb = wi[:, o + 192:o + 384]
    vb = wi[:, o + 384:o + 768]
    gb = wi[:, o + 768:o + 1152]
    rb = wi[:, o + 1152:o + 1168]
    xy = wi[:, o + 1168:]
    w_in_p = jnp.concatenate(
        [wi[:, :o], _pad_heads(qb, DK_B), _pad_heads(kb, DK_B), _pad_heads(vb, DV_B), _pad_heads(gb, DV_B),
         jnp.pad(rb, ((0, 0), (0, LANE - GATE_RANK_B))), xy], axis=1).astype(BF16)
    wg_p = jnp.pad(_pad_heads(w_gla_gate[l], DK_B), ((0, LANE - GATE_RANK_B), (0, 0))).astype(BF16)
    bg_p = _pad_heads(b_gla_gate[l].reshape(1, -1), DK_B)
    gn_p = jnp.tile(jnp.pad(g_gla_norm[l], (0, LANE - DV_B)), H_B).reshape(1, HP_B)
    wa_bd = jax.scipy.linalg.block_diag(*[w_rg_a[l, i] for i in range(NBLK_C)]).astype(BF16)
    wx_bd = jax.scipy.linalg.block_diag(*[w_rg_x[l, i] for i in range(NBLK_C)]).astype(BF16)
    wo = w_out[l]
    wo_a = wo[:W_A].astype(BF16)
    wo_b = jnp.pad(wo[W_A:W_A + W_B].reshape(H_B, DV_B, d), ((0, 0), (0, LANE - DV_B), (0, 0)))
    wo_b = wo_b.reshape(HP_B, d).astype(BF16)
    wo_c = wo[W_A + W_B:].astype(BF16)
    wr = jnp.pad(w_router[l], ((0, 0), (0, LANE - N_EXPERTS)))
    wr_hi = wr.astype(BF16)
    wr_lo = (wr - wr_hi.astype(F32)).astype(BF16)
    br = jnp.pad(b_router[l], (0, LANE - N_EXPERTS), constant_values=-1e30).reshape(1, LANE)
    return dict(w_in_p=w_in_p, wg_p=wg_p, bg_p=bg_p, gn_p=gn_p, wa_bd=wa_bd, wx_bd=wx_bd,
                wo_a=wo_a, wo_b=wo_b, wo_c=wo_c, wr_hi=wr_hi, wr_lo=wr_lo, br=br)


def _pad_state(s):
    st = jnp.swapaxes(s, -1, -2)
    return jnp.pad(st, ((0, 0), (0, 0), (0, LANE - DV_B), (0, LANE - DK_B)))


def _unpad_state(st):
    return jnp.swapaxes(st[:, :, :DV_B, :DK_B], -1, -2)


def kernel(x_prompt, x_sample, c_prompt, c_sample, cache_k_sb, cache_v_sb, state_gla, state_conv, state_lru, w_ada, b_ada, g_pre_mix, g_post_mix, g_pre_ff, g_post_ff, w_in, w_gla_gate, b_gla_gate, g_gla_norm, w_conv, b_conv, w_rg_a, b_rg_a, w_rg_x, b_rg_x, lru_lambda, w_out, w_router, b_router, w_gate_up, b_gate_up, w_down, b_down):
    depth = w_ada.shape[0]
    d = D_MODEL
    groups = []
    for x, past in ((x_prompt, False), (x_sample, True)):
        b, t, _ = x.shape
        groups.append(dict(b=b, t=t, tm=min(TOKEN_TILE, t), n=b * t, past=past, x=x.reshape(b * t, d)))
    n_all = sum(g["n"] for g in groups)
    tm_moe = math.gcd(TOKEN_TILE, n_all)
    bm = MOE_BLOCK_ROWS
    nb = (n_all * TOP_K) // bm + N_EXPERTS
    n_pad = nb * bm

    nb_rows = sum(g["b"] for g in groups)
    c_all = jnp.concatenate([c_prompt, c_sample], axis=0)
    r_pad = -(-nb_rows // SUBLANE) * SUBLANE
    c_all = jnp.pad(c_all, ((0, r_pad - nb_rows), (0, 0)))
    mod = _modulation(c_all, w_ada, b_ada)

    st =[[[] for _ in range(5)] for _ in groups]
    for l in range(depth):
        lw = _layer_weights(l, w_in, w_gla_gate, b_gla_gate, g_gla_norm, w_rg_a, w_rg_x, w_out,
                            w_router, b_router)
        cnt = jnp.zeros((SUBLANE, LANE), F32)
        row0 = 0
        per_group = []
        for gi, g in enumerate(groups):
            b, t, tm, n = g["b"], g["t"], g["tm"], g["n"]
            m = mod[l, row0:row0 + b].reshape(b, 1, 6 * d)
            row0 += b
            sh_m, sc_m, gt_m, sh_f, sc_f, gt_f = [m[:, :, j * d:(j + 1) * d] for j in range(6)]
            ka, va, qab, kab, vab, gla_in, lru_in = _premix(g["x"], g_pre_mix[l], sc_m, sh_m, lw["w_in_p"], t, tm)
            q3 = qab.reshape(b, t, W_A)
            k3 = kab.reshape(b, t, W_A)
            v3 = vab.reshape(b, t, W_A)
            if g["past"]:
                n_past = cache_k_sb.shape[2]
                tk_all = -(-(n_past + t) // ATTN_TILE) * ATTN_TILE
                padk = ((0, 0), (0, tk_all - n_past - t), (0, 0))
                k3 = jnp.pad(jnp.concatenate([cache_k_sb[l].reshape(b, n_past, W_A).astype(BF16), k3], axis=1), padk)
                v3 = jnp.pad(jnp.concatenate([cache_v_sb[l].reshape(b, n_past, W_A).astype(BF16), v3], axis=1), padk)
                off = n_past
                s0 = state_gla[l]
                buf0 = state_conv[l]
                h0 = state_lru[l]
            else:
                off = 0
                s0 = jnp.zeros((b, H_B, DK_B, DV_B), F32)
                buf0 = jnp.zeros((b, CONV_W - 1, W_C), F32)
                h0 = jnp.zeros((b, W_C), F32)
            oa = _attention(q3, k3, v3, off).reshape(n, W_A)
            ob, s_new = _gla(gla_in, lw["wg_p"], lw["bg_p"], lw["gn_p"], _pad_state(s0), t)
            buf0p = jnp.pad(buf0, ((0, 0), (SUBLANE - (CONV_W - 1), 0), (0, 0)))
            oc, conv_new, h_new = _lru(lru_in, w_conv[l], b_conv[l].reshape(1, -1), lw["wa_bd"],
                                       b_rg_a[l].reshape(1, -1), lw["wx_bd"], b_rg_x[l].reshape(1, -1),
                                       lru_lambda[l].reshape(1, -1), buf0p, h0.reshape(b, 1, W_C), t)
            xm, h2, ri, gates, cnt = _postmix(g["x"], oa, ob, oc, lw["wo_a"], lw["wo_b"], lw["wo_c"],
                                              g_post_mix[l], g_pre_ff[l], gt_m, sc_f, sh_f,
                                              lw["wr_hi"], lw["wr_lo"], lw["br"], cnt, t, tm)
            per_group.append(dict(xm=xm, h2=h2, ri=ri, gates=gates, gt_f=gt_f))
            new = (ka.reshape(b, t, W_A // HD_A, HD_A), va.reshape(b, t, W_A // HD_A, HD_A),
                   _unpad_state(s_new), conv_new[:, SUBLANE - (CONV_W - 1):], h_new.reshape(b, W_C))
            for j in range(5):
                st[gi][j].append(new[j])

        counts = cnt[0, :N_EXPERTS].astype(jnp.int32)
        padded = (counts + bm - 1) // bm * bm
        pad_end = jnp.cumsum(padded)
        pad_start = pad_end - padded
        block_start = jnp.arange(nb, dtype=jnp.int32) * bm
        block_e = jnp.minimum(jnp.sum((pad_end[None, :] <= block_start[:, None]).astype(jnp.int32), axis=1),
                              N_EXPERTS - 1)
        nvalid = (pad_end[-1] // bm).astype(jnp.int32).reshape(1)
        ri_all = jnp.concatenate([p["ri"] for p in per_group], axis=0)
        h2_all = jnp.concatenate([p["h2"] for p in per_group], axis=0)
        dest = pad_start[ri_all[:, :TOP_K]] + ri_all[:, TOP_K:2 * TOP_K]
        dest3 = dest.reshape(n_all // tm_moe, 1, tm_moe * TOP_K)
        xs = _dispatch(dest3, h2_all, n_pad, tm_moe)
        ys = _experts(block_e, nvalid, xs, w_gate_up[l], b_gate_up[l], w_down[l], b_down[l])
        tok0 = 0
        for gi, g in enumerate(groups):
            p = per_group[gi]
            tm, n = g["tm"], g["n"]
            dest_g = dest[tok0:tok0 + n].reshape(n // tm, 1, tm * TOP_K)
            tok0 += n
            g["x"] = _combine(dest_g, ys, p["gates"], p["xm"], p["gt_f"], g_post_ff[l], g["t"], tm)

    outs = [g["x"].reshape(g["b"], g["t"], d) for g in groups]
    for gi in range(len(groups)):
        outs.extend(jnp.stack(st[gi][j]) for j in range(5))
    return tuple(outs)
```

```python
import functools
import math

import numpy as np
import jax
import jax.numpy as jnp
from jax import lax
from jax.experimental import pallas as pl
from jax.experimental.pallas import tpu as pltpu

F32 = jnp.float32
BF16 = jnp.bfloat16

D_MODEL = 1024
EPS = 1e-6
HD_A = 64
W_A = 384
H_B = 4
DK_B = 48
DV_B = 96
W_B = 384
GATE_RANK_B = 16
GATE_TEMP_B = 16.0
W_C = 256
NBLK_C = 4
BW_C = 64
CONV_W = 4
RG_C = 8.0
N_EXPERTS = 32
TOP_K = 4
D_FF = 1024
SWIGLU_ALPHA = 1.702
SWIGLU_LIMIT = 7.0
LOG_WEIGHT_FLOOR = -120.0

LANE = 128
SUBLANE = 8
VMEM_LIMIT = 56 * 1024 * 1024

HP_B = H_B * LANE
OFF_GLA = 3 * W_A
GLA_W = 4 * HP_B + LANE
OFF_LRU = OFF_GLA + GLA_W
IN_PAD = OFF_LRU + 2 * W_C

TOKEN_TILE = 256
ATTN_TILE = 256
GLA_CHUNK = 64
GLA_CHUNKS_PER_STEP = 4
LRU_CHUNK = 256
MOE_BLOCK_ROWS = 256


def _cparams(sem):
    return pltpu.CompilerParams(dimension_semantics=sem, vmem_limit_bytes=VMEM_LIMIT)


def _sigmoid(x):
    return 1.0 / (1.0 + jnp.exp(-x))


def _log_sigmoid(x):
    return jnp.minimum(x, 0.0) - jnp.log(1.0 + jnp.exp(-jnp.abs(x)))


def _split_bf16(x):
    hi = x.astype(BF16)
    lo = (x - hi.astype(F32)).astype(BF16)
    return hi, lo


def _dot(a, b):
    return jnp.dot(a, b, preferred_element_type=F32)


def _dot_nt(a, b):
    return lax.dot_general(a, b, (((1,), (1,)), ((), ())), preferred_element_type=F32)


def _rms(x, g):
    return x * lax.rsqrt(jnp.mean(x * x, axis=-1, keepdims=True) + EPS) * g


def _mod_kernel(c_ref, w_ref, b_ref, o_ref):
    c = c_ref[...]
    a = (c * _sigmoid(c)).astype(BF16)
    o_ref[0] = _dot(a, w_ref[0].astype(BF16)) + b_ref[0]


def _modulation(c_all, w_ada, b_ada):
    depth, d, n = w_ada.shape
    r = c_all.shape[0]
    tn = 1536
    return pl.pallas_call(
        _mod_kernel,
        grid=(depth, n // tn),
        in_specs=[pl.BlockSpec((r, d), lambda l, j: (0, 0)),
                  pl.BlockSpec((1, d, tn), lambda l, j: (l, 0, j)),
                  pl.BlockSpec((1, 1, tn), lambda l, j: (l, 0, j))],
        out_specs=pl.BlockSpec((1, r, tn), lambda l, j: (l, 0, j)),
        out_shape=jax.ShapeDtypeStruct((depth, r, n), F32),
        compiler_params=_cparams(("arbitrary", "arbitrary")),
        name="adaln_mod",
    )(c_all, w_ada, b_ada.reshape(depth, 1, n))


def _premix_kernel(x_ref, g_ref, sc_ref, sh_ref, w_ref,
                   ka_ref, va_ref, qab_ref, kab_ref, vab_ref, gla_ref, lru_ref):
    h = _rms(x_ref[...], g_ref[...]) * (1.0 + sc_ref[0]) + sh_ref[0]
    r = _dot(h.astype(BF16), w_ref[...])
    ka = r[:, W_A:2 * W_A]
    va = r[:, 2 * W_A:3 * W_A]
    ka_ref[...] = ka
    va_ref[...] = va
    qab_ref[...] = (r[:, 0:W_A] * (HD_A ** -0.5)).astype(BF16)
    kab_ref[...] = ka.astype(BF16)
    vab_ref[...] = va.astype(BF16)
    gla_ref[...] = r[:, OFF_GLA:OFF_LRU]
    lru_ref[...] = r[:, OFF_LRU:IN_PAD]


def _premix(x2, g, sc, sh, w_in_p, seq, tm):
    n, d = x2.shape
    per_b = seq // tm
    tok = lambda i: (i, 0)
    bat = lambda i: (i // per_b, 0, 0)
    const = lambda i: (0, 0)
    outs = [((n, W_A), F32), ((n, W_A), F32), ((n, W_A), BF16), ((n, W_A), BF16), ((n, W_A), BF16),
            ((n, GLA_W), F32), ((n, 2 * W_C), F32)]
    return pl.pallas_call(
        _premix_kernel,
        grid=(n // tm,),
        in_specs=[pl.BlockSpec((tm, d), tok), pl.BlockSpec((1, d), const),
                  pl.BlockSpec((1, 1, d), bat), pl.BlockSpec((1, 1, d), bat),
                  pl.BlockSpec((d, IN_PAD), const)],
        out_specs=[pl.BlockSpec((tm, s[1]), tok) for s, _ in outs],
        out_shape=[jax.ShapeDtypeStruct(s, t) for s, t in outs],
        compiler_params=_cparams(("arbitrary",)),
        name="premix_proj",
    )(x2, g.reshape(1, d), sc, sh, w_in_p)


def _attn_kernel(q_ref, k_ref, v_ref, u_ref, o_ref, acc_ref, c_ref, *, tq, tk, off):
    i = pl.program_id(2)
    lane = lax.broadcasted_iota(jnp.int32, (1, LANE), 1)
    q = q_ref[0]
    qz = jnp.zeros_like(q)
    head_lanes = (lane < HD_A, lane >= HD_A)
    qh = tuple(jnp.where(m, q, qz) for m in head_lanes)
    acc_ref[...] = jnp.zeros_like(acc_ref)
    c_ref[...] = jnp.zeros_like(c_ref)
    row = lax.broadcasted_iota(jnp.int32, (tq, tk), 0)
    col = lax.broadcasted_iota(jnp.int32, (tq, tk), 1)
    qpos0 = off + i * tq
    n_keys = qpos0 + tq - 1
    nk = (n_keys + tk - 1) // tk
    u = u_ref[...]

    def block(j, masked):
        ks = pl.multiple_of(j * tk, tk)
        kb = k_ref[0, pl.ds(ks, tk), :]
        vb = v_ref[0, pl.ds(ks, tk), :]
        vz = jnp.zeros_like(vb)
        mask = (col + ks) < (row + qpos0)
        for h in range(2):
            s = _dot_nt(qh[h], kb)
            l1p = jnp.log(1.0 + jnp.exp(-jnp.abs(s)))
            lf = -(jnp.maximum(s, 0.0) + l1p)
            if masked:
                lf = jnp.where(mask, lf, 0.0)
            lb = jnp.minimum(s, 0.0) - l1p
            hi, lo = _split_bf16(lf)
            cr = _dot(hi, u) + _dot(lo, u)
            c = c_ref[h]
            cfull = jnp.concatenate([c] * (tk // LANE), axis=1)
            w = jnp.exp(lb + cr[:, :tk] + cfull)
            if masked:
                w = jnp.where(mask, w, 0.0)
            vh = jnp.where(head_lanes[h], vb, vz)
            acc_ref[...] += _dot(w.astype(BF16), vh)
            c_ref[h] = c + cr[:, tk:]
        return jnp.max(c_ref[...]) > LOG_WEIGHT_FLOOR

    go = block(nk - 1, True)

    def body(carry):
        jj, _ = carry
        return jj + 1, block(nk - 1 - jj, False)

    lax.while_loop(lambda carry: jnp.logical_and(carry[0] < nk, carry[1]), body, (jnp.int32(1), go))
    o_ref[0] = acc_ref[...].astype(o_ref.dtype)


def _attn_umat(tk):
    jp = np.arange(tk)[:, None]
    j = np.arange(tk)[None, :]
    u = np.concatenate([(jp > j).astype(np.float32), np.ones((tk, LANE), np.float32)], axis=1)
    return jnp.asarray(u, BF16)


def _attention(q, k, v, off):
    b, tq_all, _ = q.shape
    tk_all = k.shape[1]
    tq = min(ATTN_TILE, tq_all)
    tk = ATTN_TILE
    assert tk_all % tk == 0 and tq_all % tq == 0 and off + tq_all - 1 <= tk_all
    assert off % tk == 0 and tk % tq == 0
    kern = functools.partial(_attn_kernel, tq=tq, tk=tk, off=off)
    return pl.pallas_call(
        kern,
        grid=(b, W_A // LANE, tq_all // tq),
        in_specs=[pl.BlockSpec((1, tq, LANE), lambda bi, hp, i: (bi, i, hp)),
                  pl.BlockSpec((1, tk_all, LANE), lambda bi, hp, i: (bi, 0, hp)),
                  pl.BlockSpec((1, tk_all, LANE), lambda bi, hp, i: (bi, 0, hp)),
                  pl.BlockSpec((tk, tk + LANE), lambda bi, hp, i: (0, 0))],
        out_specs=pl.BlockSpec((1, tq, LANE), lambda bi, hp, i: (bi, i, hp)),
        out_shape=jax.ShapeDtypeStruct((b, tq_all, W_A), BF16),
        scratch_shapes=[pltpu.VMEM((tq, LANE), F32), pltpu.VMEM((2, tq, LANE), F32)],
        compiler_params=_cparams(("arbitrary", "arbitrary", "arbitrary")),
        name="stick_breaking_attn",
    )(q, k, v, _attn_umat(tk))


def _gla_levels(c):
    return int(math.log2(c))


def _gla_mats(c):
    t = np.arange(c)[:, None]
    s = np.arange(c)[None, :]
    mats = [(s <= t).astype(np.float32), (s > t).astype(np.float32)]
    for lv in range(_gla_levels(c)):
        m = 1 << lv
        ref = (t // (2 * m)) * (2 * m) + m - 1
        mats.append(((s > ref) & (s <= t)).astype(np.float32) - ((s > t) & (s <= ref)).astype(np.float32))
    return jnp.asarray(np.concatenate(mats, axis=0), BF16)


def _gla_kernel(in_ref, wg_ref, bg_ref, gn_ref, mall_ref, s0_ref, ob_ref, sout_ref, st_ref, *, C, G):
    ci = pl.program_id(1)

    @pl.when(ci == 0)
    def _():
        st_ref[...] = s0_ref[0]

    for sub in range(G):
        _gla_chunk(in_ref, wg_ref, bg_ref, gn_ref, mall_ref, ob_ref, st_ref, C, sub * C)

    @pl.when(ci == pl.num_programs(1) - 1)
    def _():
        sout_ref[0] = st_ref[...]


def _gla_chunk(in_ref, wg_ref, bg_ref, gn_ref, mall_ref, ob_ref, st_ref, C, r0):
    rows = slice(r0, r0 + C)
    q = in_ref[rows, 0:HP_B] * (DK_B ** -0.5)
    k = in_ref[rows, HP_B:2 * HP_B]
    v = in_ref[rows, 2 * HP_B:3 * HP_B]
    gb = in_ref[rows, 3 * HP_B:4 * HP_B]
    rb = in_ref[rows, 4 * HP_B:4 * HP_B + LANE]
    lg = _log_sigmoid(_dot(rb.astype(BF16), wg_ref[...]) + bg_ref[...]) * (1.0 / GATE_TEMP_B)
    hi, lo = _split_bf16(lg)
    mall = mall_ref[...]
    dall = _dot(mall, hi) + _dot(mall, lo)
    eb = jnp.exp(dall[0:C])
    elast = jnp.exp(dall[C:2 * C])
    eb_last = eb[C - 1:C, :]
    rowi = lax.broadcasted_iota(jnp.int32, (C, 1), 0)
    row = lax.broadcasted_iota(jnp.int32, (C, C), 0)
    col = lax.broadcasted_iota(jnp.int32, (C, C), 1)
    qe = (q * eb).astype(BF16)
    ke = (k * elast).astype(BF16)
    qb = q.astype(BF16)
    kb = k.astype(BF16)
    vb = v.astype(BF16)
    zero = jnp.zeros_like(q)
    lv_q, lv_k = [], []
    for lv in range(_gla_levels(C)):
        e = jnp.exp(-jnp.abs(dall[(2 + lv) * C:(3 + lv) * C]))
        second = ((rowi >> lv) & 1) == 1
        lv_q.append(jnp.where(second, q * e, zero).astype(BF16))
        lv_k.append(jnp.where(second, zero, k * e).astype(BF16))
    outs = []
    for h in range(H_B):
        sl = slice(h * LANE, (h + 1) * LANE)
        att = jnp.where(row == col, _dot_nt(qb[:, sl], kb[:, sl]), 0.0)
        for lv in range(_gla_levels(C)):
            same = (row >> (lv + 1)) == (col >> (lv + 1))
            att = att + jnp.where(same, _dot_nt(lv_q[lv][:, sl], lv_k[lv][:, sl]), 0.0)
        st = st_ref[h]
        o = _dot(att.astype(BF16), vb[:, sl]) + _dot_nt(qe[:, sl], st.astype(BF16))
        st_ref[h] = st * eb_last[:, sl] + _dot(v[:, sl].T.astype(BF16), ke[:, sl])
        ms = jnp.sum(o * o, axis=-1, keepdims=True) * (1.0 / DV_B)
        on = o * lax.rsqrt(ms + EPS) * gn_ref[:, sl]
        g = gb[:, sl]
        outs.append((on * (g * _sigmoid(g))).astype(BF16))
    ob_ref[rows, :] = jnp.concatenate(outs, axis=1)


def _gla(gla_in, wg_p, bg_p, gn_p, s0t, seq):
    n = gla_in.shape[0]
    b = n // seq
    c = min(GLA_CHUNK, seq)
    g = math.gcd(GLA_CHUNKS_PER_STEP, seq // c)
    per_b = seq // (c * g)
    mall = _gla_mats(c)
    const2 = lambda bi, ci: (0, 0)
    return pl.pallas_call(
        functools.partial(_gla_kernel, C=c, G=g),
        grid=(b, per_b),
        in_specs=[pl.BlockSpec((c * g, GLA_W), lambda bi, ci: (bi * per_b + ci, 0)),
                  pl.BlockSpec((LANE, HP_B), const2), pl.BlockSpec((1, HP_B), const2),
                  pl.BlockSpec((1, HP_B), const2), pl.BlockSpec(mall.shape, const2),
                  pl.BlockSpec((1, H_B, LANE, LANE), lambda bi, ci: (bi, 0, 0, 0))],
        out_specs=[pl.BlockSpec((c * g, HP_B), lambda bi, ci: (bi * per_b + ci, 0)),
                   pl.BlockSpec((1, H_B, LANE, LANE), lambda bi, ci: (bi, 0, 0, 0))],
        out_shape=[jax.ShapeDtypeStruct((n, HP_B), BF16),
                   jax.ShapeDtypeStruct((b, H_B, LANE, LANE), F32)],
        scratch_shapes=[pltpu.VMEM((H_B, LANE, LANE), F32)],
        compiler_params=_cparams(("arbitrary", "arbitrary")),
        name="gla_chunked",
    )(gla_in, wg_p, bg_p, gn_p, mall, s0t)


def _lru_kernel(in_ref, cw_ref, cb_ref, wa_ref, ba_ref, wx_ref, bx_ref, lam_ref, buf0_ref, h0_ref,
                oc_ref, conv_ref, hout_ref, xp_ref, hc_ref, *, C):
    ci = pl.program_id(1)

    @pl.when(ci == 0)
    def _():
        xp_ref[0:SUBLANE] = buf0_ref[0]
        hc_ref[...] = h0_ref[0]

    x = in_ref[:, 0:W_C]
    y = in_ref[:, W_C:2 * W_C]
    xp_ref[SUBLANE:SUBLANE + C] = x
    xc = cb_ref[...]
    for j in range(CONV_W):
        xc = xc + xp_ref[pl.ds(SUBLANE - (CONV_W - 1) + j, C), :] * cw_ref[j:j + 1, :]
    xcb = xc.astype(BF16)
    r = _sigmoid(_dot(xcb, wa_ref[...]) + ba_ref[...])
    gi = _sigmoid(_dot(xcb, wx_ref[...]) + bx_ref[...])
    log_a = RG_C * r * _log_sigmoid(lam_ref[...])
    a = jnp.exp(log_a)
    u = jnp.sqrt(1.0 - jnp.exp(2.0 * log_a)) * (gi * xc)
    rowi = lax.broadcasted_iota(jnp.int32, (C, 1), 0)
    d = 1
    while d < C:
        keep = rowi >= d
        a_s = pltpu.roll(a, d, axis=0)
        u_s = pltpu.roll(u, d, axis=0)
        u = jnp.where(keep, a * u_s + u, u)
        a = jnp.where(keep, a * a_s, a)
        d *= 2
    hseq = u + a * hc_ref[...]
    hc_ref[...] = hseq[C - 1:C, :]
    gelu = 0.5 * y * (1.0 + jnp.tanh(math.sqrt(2.0 / math.pi) * (y + 0.044715 * (y * y * y))))
    oc_ref[...] = (hseq * gelu).astype(BF16)
    tail = xp_ref[C:C + SUBLANE]
    xp_ref[0:SUBLANE] = tail

    @pl.when(ci == pl.num_programs(1) - 1)
    def _():
        conv_ref[0] = tail
        hout_ref[0] = hseq[C - 1:C, :]


def _lru(lru_in, cw, cb, wa_bd, ba, wx_bd, bx, lam, buf0, h0, seq):
    n = lru_in.shape[0]
    b = n // seq
    c = min(LRU_CHUNK, seq)
    per_b = seq // c
    const2 = lambda bi, ci: (0, 0)
    vec = pl.BlockSpec((1, W_C), const2)
    return pl.pallas_call(
        functools.partial(_lru_kernel, C=c),
        grid=(b, per_b),
        in_specs=[pl.BlockSpec((c, 2 * W_C), lambda bi, ci: (bi * per_b + ci, 0)),
                  pl.BlockSpec((CONV_W, W_C), const2), vec,
                  pl.BlockSpec((W_C, W_C), const2), vec, pl.BlockSpec((W_C, W_C), const2), vec, vec,
                  pl.BlockSpec((1, SUBLANE, W_C), lambda bi, ci: (bi, 0, 0)),
                  pl.BlockSpec((1, 1, W_C), lambda bi, ci: (bi, 0, 0))],
        out_specs=[pl.BlockSpec((c, W_C), lambda bi, ci: (bi * per_b + ci, 0)),
                   pl.BlockSpec((1, SUBLANE, W_C), lambda bi, ci: (bi, 0, 0)),
                   pl.BlockSpec((1, 1, W_C), lambda bi, ci: (bi, 0, 0))],
        out_shape=[jax.ShapeDtypeStruct((n, W_C), BF16),
                   jax.ShapeDtypeStruct((b, SUBLANE, W_C), F32),
                   jax.ShapeDtypeStruct((b, 1, W_C), F32)],
        scratch_shapes=[pltpu.VMEM((c + SUBLANE, W_C), F32), pltpu.VMEM((1, W_C), F32)],
        compiler_params=_cparams(("arbitrary", "arbitrary")),
        name="conv_rglru",
    )(lru_in, cw, cb, wa_bd, ba, wx_bd, bx, lam, buf0, h0)


def _postmix_kernel(x_ref, oa_ref, ob_ref, oc_ref, wa_ref, wb_ref, wc_ref, gpm_ref, gpf_ref,
                    gt_ref, sc_ref, sh_ref, wr_hi_ref, wr_lo_ref, br_ref, tri_ref, ustrict_ref,
                    xm_ref, h2_ref, route_ref, stat_ref, *, tm):
    y =_dot(oa_ref[...], wa_ref[...]) + _dot(ob_ref[...], wb_ref[...]) + _dot(oc_ref[...], wc_ref[...])
    xm = x_ref[...] + gt_ref[0] * _rms(y, gpm_ref[...])
    xm_ref[...] = xm
    h2 = _rms(xm, gpf_ref[...]) * (1.0 + sc_ref[0]) + sh_ref[0]
    h2_ref[...] = h2
    hi, lo = _split_bf16(h2)
    wh = wr_hi_ref[...]
    logits = _dot(hi, wh) + _dot(lo, wh) + _dot(hi, wr_lo_ref[...]) + br_ref[...]
    lane = lax.broadcasted_iota(jnp.int32, (tm, LANE), 1)
    lane_f = lane.astype(F32)
    neg = jnp.float32(-jnp.inf)
    vals, hots = [], []
    for _ in range(TOP_K):
        m = jnp.max(logits, axis=-1, keepdims=True)
        idx = jnp.min(jnp.where(logits == m, lane_f, float(LANE)), axis=-1, keepdims=True)
        hot = lane_f == idx
        logits = jnp.where(hot, neg, logits)
        vals.append(m)
        hots.append(hot)
    ex = [jnp.exp(vk - vals[0]) for vk in vals]
    inv = 1.0 / (ex[0] + ex[1] + ex[2] + ex[3])
    sel = jnp.zeros((tm, LANE), F32)
    for hot in hots:
        sel = jnp.where(hot, 1.0, sel)
    selb = sel.astype(BF16)
    tile_cnt = jnp.sum(sel, axis=0, keepdims=True)
    groups8 = jnp.floor((tile_cnt + (SUBLANE - 1.0)) * (1.0 / SUBLANE))
    g8b = jnp.broadcast_to(groups8, (SUBLANE, LANE)).astype(BF16)
    loc_start = _dot(g8b, ustrict_ref[...])[0:1] * float(SUBLANE)
    local = _dot(tri_ref[...], selb) + loc_start
    route = jnp.zeros((tm, LANE), F32)
    for kk in range(TOP_K):
        p_k = jnp.sum(jnp.where(hots[kk], local, 0.0), axis=-1, keepdims=True)
        route = jnp.where(lane == kk, ex[kk] * inv, route)
        route = jnp.where(lane == TOP_K + kk, p_k, route)
    route_ref[...] = route
    srow = lax.broadcasted_iota(jnp.int32, (SUBLANE, LANE), 0)
    stat_ref[...] = jnp.where(srow == 0, groups8 * float(SUBLANE), jnp.where(srow == 1, loc_start, 0.0))


def _postmix(x2, oa, ob, oc, wo_a, wo_b, wo_c, gpm, gpf, gt, sc, sh, wr_hi, wr_lo, br, seq, tm):
    n, d = x2.shape
    per_b = seq // tm
    tok = lambda i: (i, 0)
    bat = lambda i: (i // per_b, 0, 0)
    const = lambda i: (0, 0)
    tri = jnp.asarray(np.tril(np.ones((tm, tm), np.float32), -1), BF16)
    ustrict = jnp.asarray(np.triu(np.ones((LANE, LANE), np.float32), 1), BF16)
    full = lambda a: pl.BlockSpec(a.shape, const)
    nt = n // tm
    return pl.pallas_call(
        functools.partial(_postmix_kernel, tm=tm),
        grid=(nt,),
        in_specs=[pl.BlockSpec((tm, d), tok), pl.BlockSpec((tm, W_A), tok),
                  pl.BlockSpec((tm, HP_B), tok), pl.BlockSpec((tm, W_C), tok),
                  full(wo_a), full(wo_b), full(wo_c),
                  pl.BlockSpec((1, d), const), pl.BlockSpec((1, d), const),
                  pl.BlockSpec((1, 1, d), bat), pl.BlockSpec((1, 1, d), bat), pl.BlockSpec((1, 1, d), bat),
                  full(wr_hi), full(wr_lo), pl.BlockSpec((1, LANE), const), full(tri), full(ustrict)],
        out_specs=[pl.BlockSpec((tm, d), tok), pl.BlockSpec((tm, d), tok),
                   pl.BlockSpec((tm, LANE), tok), pl.BlockSpec((SUBLANE, LANE), tok)],
        out_shape=[jax.ShapeDtypeStruct((n, d), F32), jax.ShapeDtypeStruct((n, d), F32),
                   jax.ShapeDtypeStruct((n, LANE), F32), jax.ShapeDtypeStruct((nt * SUBLANE, LANE), F32)],
        compiler_params=_cparams(("arbitrary",)),
        name="postmix_router",
    )(x2, oa, ob, oc, wo_a, wo_b, wo_c, gpm.reshape(1, d), gpf.reshape(1, d), gt, sc, sh,
      wr_hi, wr_lo, br, tri, ustrict)


def _run_copies(tab_ref, tm, make_copy, wait):
    sizes = [1 << b for b in range(int(math.log2(tm)), int(math.log2(SUBLANE)) - 1, -1)]

    def per_expert(e, carry):
        n = tab_ref[0, 0, e]
        loc = tab_ref[0, 0, N_EXPERTS + e]
        dst = tab_ref[0, 0, 2 * N_EXPERTS + e]
        for sz in sizes:
            done = n & ~(2 * sz - 1)

            @pl.when((n & sz) != 0)
            def _():
                cp = make_copy(pl.multiple_of(loc + done, SUBLANE), pl.multiple_of(dst + done, SUBLANE), sz)
                if wait:
                    cp.wait()
                else:
                    cp.start()
        return carry

    lax.fori_loop(0, N_EXPERTS, per_expert, 0)


def _local_rows(tm):
    return tm * TOP_K + N_EXPERTS * SUBLANE


def _slot_onehot(route_ref, tm):
    nloc = _local_rows(tm)
    pos = lax.broadcasted_iota(jnp.int32, (tm, nloc), 1).astype(F32)
    route = route_ref[...]

    def build(values):
        m = jnp.zeros((tm, nloc), F32)
        for kk in range(TOP_K):
            m = jnp.where(route[:, TOP_K + kk:TOP_K + kk + 1] == pos, values[kk], m)
        return m

    return route, build


def _dispatch_kernel(tab_ref, route_ref, h_ref, xs_in_ref, xs_ref, sorted_ref, sem, *, tm):
    del xs_in_ref
    _, build = _slot_onehot(route_ref, tm)
    perm = build([1.0] * TOP_K).astype(BF16)
    sorted_ref[...] = lax.dot_general(perm, h_ref[...].astype(BF16), (((0,), (0,)), ((), ())),
                                      preferred_element_type=F32)
    make = lambda loc, dst, sz: pltpu.make_async_copy(sorted_ref.at[pl.ds(loc, sz)], xs_ref.at[pl.ds(dst, sz)], sem)
    _run_copies(tab_ref, tm, make, wait=False)
    _run_copies(tab_ref, tm, make, wait=True)


def _dispatch(tab, route, h2, xs, tm):
    n, d = h2.shape
    tok = lambda i: (i, 0)
    return pl.pallas_call(
        functools.partial(_dispatch_kernel, tm=tm),
        grid=(n // tm,),
        in_specs=[pl.BlockSpec((1, 1, LANE), lambda i: (i, 0, 0), memory_space=pltpu.SMEM),
                  pl.BlockSpec((tm, LANE), tok), pl.BlockSpec((tm, d), tok),
                  pl.BlockSpec(memory_space=pl.ANY)],
        out_specs=pl.BlockSpec(memory_space=pl.ANY),
        out_shape=jax.ShapeDtypeStruct(xs.shape, F32),
        scratch_shapes=[pltpu.VMEM((_local_rows(tm), d), F32), pltpu.SemaphoreType.DMA(())],
        input_output_aliases={3: 0},
        compiler_params=_cparams(("arbitrary",)),
        name="moe_dispatch",
    )(tab, route, h2, xs)


def _expert_kernel(be_ref, nv_ref, x_ref, wgu_ref, bgu_ref, wd_ref, bd_ref, y_ref, wgu_bf, wd_bf):
    g = pl.program_id(0)

    @pl.when(jnp.logical_or(g == 0, be_ref[g] != be_ref[jnp.maximum(g - 1, 0)]))
    def _():
        wgu_bf[...] = wgu_ref[0, 0].astype(BF16)
        wd_bf[...] = wd_ref[0, 0].astype(BF16)

    @pl.when(g < nv_ref[0])
    def _():
        gu = _dot(x_ref[...].astype(BF16), wgu_bf[...]) + bgu_ref[0, 0]
        gt = jnp.minimum(gu[:, :D_FF], SWIGLU_LIMIT)
        up = jnp.clip(gu[:, D_FF:], -SWIGLU_LIMIT, SWIGLU_LIMIT)
        act = (up + 1.0) * (gt * _sigmoid(SWIGLU_ALPHA * gt))
        y_ref[...] = _dot(act.astype(BF16), wd_bf[...]) + bd_ref[0, 0]

    @pl.when(g >= nv_ref[0])
    def _():
        y_ref[...] = jnp.zeros_like(y_ref)


def _experts(l, block_e, nvalid, xs, wgu, bgu, wd, bd):
    n_pad, d = xs.shape
    depth = wgu.shape[0]
    bm = MOE_BLOCK_ROWS
    nb = n_pad // bm
    grid_spec = pltpu.PrefetchScalarGridSpec(
        num_scalar_prefetch=2,
        grid=(nb,),
        in_specs=[pl.BlockSpec((bm, d), lambda g, be, nv: (g, 0)),
                  pl.BlockSpec((1, 1, d, 2 * D_FF), lambda g, be, nv: (l, be[g], 0, 0)),
                  pl.BlockSpec((1, 1, 1, 2 * D_FF), lambda g, be, nv: (l, be[g], 0, 0)),
                  pl.BlockSpec((1, 1, D_FF, d), lambda g, be, nv: (l, be[g], 0, 0)),
                  pl.BlockSpec((1, 1, 1, d), lambda g, be, nv: (l, be[g], 0, 0))],
        out_specs=pl.BlockSpec((bm, d), lambda g, be, nv: (g, 0)),
        scratch_shapes=[pltpu.VMEM((d, 2 * D_FF), BF16), pltpu.VMEM((D_FF, d), BF16)],
    )
    return pl.pallas_call(
        _expert_kernel,
        grid_spec=grid_spec,
        out_shape=jax.ShapeDtypeStruct((n_pad, d), F32),
        compiler_params=_cparams(("arbitrary",)),
        name="moe_experts",
    )(block_e, nvalid, xs, wgu, bgu.reshape(depth, N_EXPERTS, 1, -1), wd, bd.reshape(depth, N_EXPERTS, 1, -1))


def _combine_kernel(tab_ref, route_ref, ys_ref, xm_ref, gt_ref, g_ref, o_ref, buf_ref, sem, *, tm):
    make = lambda loc, dst, sz: pltpu.make_async_copy(ys_ref.at[pl.ds(dst, sz)], buf_ref.at[pl.ds(loc, sz)], sem)
    buf_ref[...] = jnp.zeros_like(buf_ref)
    _run_copies(tab_ref, tm, make, wait=False)
    route, build = _slot_onehot(route_ref, tm)
    gate_m = build([route[:, kk:kk + 1] for kk in range(TOP_K)])
    g_hi, g_lo = _split_bf16(gate_m)
    _run_copies(tab_ref, tm, make, wait=True)
    b_hi, b_lo = _split_bf16(buf_ref[...])
    y = _dot(g_hi, b_hi) + _dot(g_lo, b_hi) + _dot(g_hi, b_lo)
    o_ref[...] = xm_ref[...] + gt_ref[0] * _rms(y, g_ref[...])


def _combine(tab, route, ys, xm, gt, g, seq, tm):
    n, d = xm.shape
    per_b = seq // tm
    tok = lambda i: (i, 0)
    return pl.pallas_call(
        functools.partial(_combine_kernel, tm=tm),
        grid=(n // tm,),
        in_specs=[pl.BlockSpec((1, 1, LANE), lambda i: (i, 0, 0), memory_space=pltpu.SMEM),
                  pl.BlockSpec((tm, LANE), tok), pl.BlockSpec(memory_space=pl.ANY),
                  pl.BlockSpec((tm, d), tok),
                  pl.BlockSpec((1, 1, d), lambda i: (i // per_b, 0, 0)),
                  pl.BlockSpec((1, d), lambda i: (0, 0))],
        out_specs=pl.BlockSpec((tm, d), tok),
        out_shape=jax.ShapeDtypeStruct((n, d), F32),
        scratch_shapes=[pltpu.VMEM((_local_rows(tm), d), F32), pltpu.SemaphoreType.DMA(())],
        compiler_params=_cparams(("arbitrary",)),
        name="moe_combine",
    )(tab, route, ys, xm, gt, g.reshape(1, d))


def _pad_heads(w, width):
    lead = w.shape[:-1]
    w = w.reshape(lead + (H_B, width))
    w = jnp.pad(w, [(0, 0)] * len(lead) + [(0, 0), (0, LANE - width)])
    return w.reshape(lead + (HP_B,))


def _layer_weights(l, w_in, w_gla_gate, b_gla_gate, g_gla_norm, w_rg_a, w_rg_x, w_out, w_router, b_router):
    d = D_MODEL
    wi = w_in[l]
    o = 3 * W_A
    qb = wi[:, o:o + H_B * DK_B]
    kb = wi[:, o + 192:o + 384]
    vb = wi[:, o + 384:o + 768]
    gb = wi[:, o + 768:o + 1152]
    rb = wi[:, o + 1152:o + 1168]
    xy = wi[:, o + 1168:]
    w_in_p = jnp.concatenate(
        [wi[:, :o], _pad_heads(qb, DK_B), _pad_heads(kb, DK_B), _pad_heads(vb, DV_B), _pad_heads(gb, DV_B),
         jnp.pad(rb, ((0, 0), (0, LANE - GATE_RANK_B))), xy], axis=1).astype(BF16)
    wg_p = jnp.pad(_pad_heads(w_gla_gate[l], DK_B), ((0, LANE - GATE_RANK_B), (0, 0))).astype(BF16)
    bg_p = _pad_heads(b_gla_gate[l].reshape(1, -1), DK_B)
    gn_p = jnp.tile(jnp.pad(g_gla_norm[l], (0, LANE - DV_B)), H_B).reshape(1, HP_B)
    wa_bd = jax.scipy.linalg.block_diag(*[w_rg_a[l, i] for i in range(NBLK_C)]).astype(BF16)
    wx_bd = jax.scipy.linalg.block_diag(*[w_rg_x[l, i] for i in range(NBLK_C)]).astype(BF16)
    wo = w_out[l]
    wo_a = wo[:W_A].astype(BF16)
    wo_b = jnp.pad(wo[W_A:W_A + W_B].reshape(H_B, DV_B, d), ((0, 0), (0, LANE - DV_B), (0, 0)))
    wo_b = wo_b.reshape(HP_B, d).astype(BF16)
    wo_c = wo[W_A + W_B:].astype(BF16)
    wr = jnp.pad(w_router[l], ((0, 0), (0, LANE - N_EXPERTS)))
    wr_hi = wr.astype(BF16)
    wr_lo = (wr - wr_hi.astype(F32)).astype(BF16)
    br = jnp.pad(b_router[l], (0, LANE - N_EXPERTS), constant_values=-1e30).reshape(1, LANE)
    return dict(w_in_p=w_in_p, wg_p=wg_p, bg_p=bg_p, gn_p=gn_p, wa_bd=wa_bd, wx_bd=wx_bd,
                wo_a=wo_a, wo_b=wo_b, wo_c=wo_c, wr_hi=wr_hi, wr_lo=wr_lo, br=br)


def _pad_state(s):
    st = jnp.swapaxes(s, -1, -2)
    return jnp.pad(st, ((0, 0), (0, 0), (0, LANE - DV_B), (0, LANE - DK_B)))


def _unpad_state(st):
    return jnp.swapaxes(st[:, :, :DV_B, :DK_B], -1, -2)


def kernel(x_prompt, x_sample, c_prompt, c_sample, cache_k_sb, cache_v_sb, state_gla, state_conv, state_lru, w_ada, b_ada, g_pre_mix, g_post_mix, g_pre_ff, g_post_ff, w_in, w_gla_gate, b_gla_gate, g_gla_norm, w_conv, b_conv, w_rg_a, b_rg_a, w_rg_x, b_rg_x, lru_lambda, w_out, w_router, b_router, w_gate_up, b_gate_up, w_down, b_down):
    depth = w_ada.shape[0]
    d = D_MODEL
    groups = []
    for x, past in ((x_prompt, False), (x_sample, True)):
        b, t, _ = x.shape
        groups.append(dict(b=b, t=t, tm=min(TOKEN_TILE, t), n=b * t, past=past, x=x.reshape(b * t, d)))
    n_all = sum(g["n"] for g in groups)
    bm = MOE_BLOCK_ROWS
    n_tiles = sum(g["n"] // g["tm"] for g in groups)
    nb = -(-(n_all * TOP_K + n_tiles * N_EXPERTS * (SUBLANE - 1)) // bm) + N_EXPERTS
    n_pad = nb * bm

    nb_rows = sum(g["b"] for g in groups)
    c_all = jnp.concatenate([c_prompt, c_sample], axis=0)
    r_pad = -(-nb_rows // SUBLANE) * SUBLANE
    c_all = jnp.pad(c_all, ((0, r_pad - nb_rows), (0, 0)))
    mod = _modulation(c_all, w_ada, b_ada)

    st =[[[] for _ in range(5)] for _ in groups]
    for l in range(depth):
        lw = _layer_weights(l, w_in, w_gla_gate, b_gla_gate, g_gla_norm, w_rg_a, w_rg_x, w_out,
                            w_router, b_router)
        row0 = 0
        per_group = []
        for gi, g in enumerate(groups):
            b, t, tm, n = g["b"], g["t"], g["tm"], g["n"]
            m = mod[l, row0:row0 + b].reshape(b, 1, 6 * d)
            row0 += b
            sh_m, sc_m, gt_m, sh_f, sc_f, gt_f = [m[:, :, j * d:(j + 1) * d] for j in range(6)]
            ka, va, qab, kab, vab, gla_in, lru_in = _premix(g["x"], g_pre_mix[l], sc_m, sh_m, lw["w_in_p"], t, tm)
            q3 = qab.reshape(b, t, W_A)
            k3 = kab.reshape(b, t, W_A)
            v3 = vab.reshape(b, t, W_A)
            if g["past"]:
                n_past = cache_k_sb.shape[2]
                tk_all = -(-(n_past + t) // ATTN_TILE) * ATTN_TILE
                padk = ((0, 0), (0, tk_all - n_past - t), (0, 0))
                k3 = jnp.pad(jnp.concatenate([cache_k_sb[l].reshape(b, n_past, W_A).astype(BF16), k3], axis=1), padk)
                v3 = jnp.pad(jnp.concatenate([cache_v_sb[l].reshape(b, n_past, W_A).astype(BF16), v3], axis=1), padk)
                off = n_past
                s0 = state_gla[l]
                buf0 = state_conv[l]
                h0 = state_lru[l]
            else:
                off = 0
                s0 = jnp.zeros((b, H_B, DK_B, DV_B), F32)
                buf0 = jnp.zeros((b, CONV_W - 1, W_C), F32)
                h0 = jnp.zeros((b, W_C), F32)
            oa = _attention(q3, k3, v3, off).reshape(n, W_A)
            ob, s_new = _gla(gla_in, lw["wg_p"], lw["bg_p"], lw["gn_p"], _pad_state(s0), t)
            buf0p = jnp.pad(buf0, ((0, 0), (SUBLANE - (CONV_W - 1), 0), (0, 0)))
            oc, conv_new, h_new = _lru(lru_in, w_conv[l], b_conv[l].reshape(1, -1), lw["wa_bd"],
                                       b_rg_a[l].reshape(1, -1), lw["wx_bd"], b_rg_x[l].reshape(1, -1),
                                       lru_lambda[l].reshape(1, -1), buf0p, h0.reshape(b, 1, W_C), t)
            xm, h2, route, stat = _postmix(g["x"], oa, ob, oc, lw["wo_a"], lw["wo_b"], lw["wo_c"],
                                           g_post_mix[l], g_pre_ff[l], gt_m, sc_f, sh_f,
                                           lw["wr_hi"], lw["wr_lo"], lw["br"], t, tm)
            per_group.append(dict(xm=xm, h2=h2, route=route, stat=stat, gt_f=gt_f))
            new = (ka.reshape(b, t, W_A // HD_A, HD_A), va.reshape(b, t, W_A // HD_A, HD_A),
                   _unpad_state(s_new), conv_new[:, SUBLANE - (CONV_W - 1):], h_new.reshape(b, W_C))
            for j in range(5):
                st[gi][j].append(new[j])

        stat = jnp.concatenate([p["stat"].reshape(-1, SUBLANE, LANE)[:, :2, :N_EXPERTS] for p in per_group],
                               axis=0).astype(jnp.int32)
        run_rows, run_loc = stat[:, 0], stat[:, 1]
        before = jnp.cumsum(run_rows, axis=0) - run_rows
        counts = jnp.sum(run_rows, axis=0)
        padded = (counts + bm - 1) // bm * bm
        pad_end = jnp.cumsum(padded)
        pad_start = pad_end - padded
        block_start = jnp.arange(nb, dtype=jnp.int32) * bm
        block_e = jnp.minimum(jnp.sum((pad_end[None, :] <= block_start[:, None]).astype(jnp.int32), axis=1),
                              N_EXPERTS - 1)
        nvalid = (pad_end[-1] // bm).astype(jnp.int32).reshape(1)
        tab = jnp.concatenate([run_rows, run_loc, pad_start[None, :] + before, jnp.zeros_like(run_rows)],
                              axis=1).reshape(-1, 1, LANE)
        tile0 = 0
        for gi, g in enumerate(groups):
            nt = g["n"] // g["tm"]
            per_group[gi]["tab"] = tab[tile0:tile0 + nt]
            tile0 += nt
        xs = jnp.zeros((n_pad, d), F32)
        for gi, g in enumerate(groups):
            p = per_group[gi]
            xs = _dispatch(p["tab"], p["route"], p["h2"], xs, g["tm"])
        ys = _experts(l, block_e, nvalid, xs, w_gate_up, b_gate_up, w_down, b_down)
        for gi, g in enumerate(groups):
            p = per_group[gi]
            g["x"] = _combine(p["tab"], p["route"], ys, p["xm"], p["gt_f"], g_post_ff[l], g["t"], g["tm"])

    outs = [g["x"].reshape(g["b"], g["t"], d) for g in groups]
    for gi in range(len(groups)):
        outs.extend(jnp.stack(st[gi][j]) for j in range(5))
    return tuple(outs)
```

```python
import functools
import math

import numpy as np
import jax
import jax.numpy as jnp
from jax import lax
from jax.experimental import pallas as pl
from jax.experimental.pallas import tpu as pltpu

F32 = jnp.float32
BF16 = jnp.bfloat16

D_MODEL = 1024
EPS = 1e-6
HD_A = 64
W_A = 384
H_B = 4
DK_B = 48
DV_B = 96
W_B = 384
GATE_RANK_B = 16
GATE_TEMP_B = 16.0
W_C = 256
NBLK_C = 4
BW_C = 64
CONV_W = 4
RG_C = 8.0
N_EXPERTS = 32
TOP_K = 4
D_FF = 1024
SWIGLU_ALPHA = 1.702
SWIGLU_LIMIT = 7.0
LOG_WEIGHT_FLOOR = -120.0

LANE = 128
SUBLANE = 8
VMEM_LIMIT = 56 * 1024 * 1024

HP_B = H_B * LANE
OFF_GLA = 3 * W_A
GLA_W = 4 * HP_B + LANE
OFF_LRU = OFF_GLA + GLA_W
IN_PAD = OFF_LRU + 2 * W_C

TOKEN_TILE = 256
ATTN_TILE = 256
GLA_CHUNK = 64
GLA_CHUNKS_PER_STEP = 4
LRU_CHUNK = 256
MOE_BLOCK_ROWS = 512


def _cparams(sem):
    return pltpu.CompilerParams(dimension_semantics=sem, vmem_limit_bytes=VMEM_LIMIT)


def _sigmoid(x):
    return 1.0 / (1.0 + jnp.exp(-x))


def _log_sigmoid(x):
    return jnp.minimum(x, 0.0) - jnp.log(1.0 + jnp.exp(-jnp.abs(x)))


def _split_bf16(x):
    hi = x.astype(BF16)
    lo = (x - hi.astype(F32)).astype(BF16)
    return hi, lo


def _dot(a, b):
    return jnp.dot(a, b, preferred_element_type=F32)


def _dot_nt(a, b):
    return lax.dot_general(a, b, (((1,), (1,)), ((), ())), preferred_element_type=F32)


def _rms(x, g):
    return x * lax.rsqrt(jnp.mean(x * x, axis=-1, keepdims=True) + EPS) * g


def _mod_kernel(c_ref, w_ref, b_ref, o_ref):
    c = c_ref[...]
    a = (c * _sigmoid(c)).astype(BF16)
    o_ref[0] = _dot(a, w_ref[0].astype(BF16)) + b_ref[0]


def _modulation(c_all, w_ada, b_ada):
    depth, d, n = w_ada.shape
    r = c_all.shape[0]
    tn = 1536
    return pl.pallas_call(
        _mod_kernel,
        grid=(depth, n // tn),
        in_specs=[pl.BlockSpec((r, d), lambda l, j: (0, 0)),
                  pl.BlockSpec((1, d, tn), lambda l, j: (l, 0, j)),
                  pl.BlockSpec((1, 1, tn), lambda l, j: (l, 0, j))],
        out_specs=pl.BlockSpec((1, r, tn), lambda l, j: (l, 0, j)),
        out_shape=jax.ShapeDtypeStruct((depth, r, n), F32),
        compiler_params=_cparams(("arbitrary", "arbitrary")),
        name="adaln_mod",
    )(c_all, w_ada, b_ada.reshape(depth, 1, n))


def _premix_kernel(x_ref, g_ref, sc_ref, sh_ref, w_ref,
                   ka_ref, va_ref, qab_ref, kab_ref, vab_ref, gla_ref, lru_ref):
    h = _rms(x_ref[...], g_ref[...]) * (1.0 + sc_ref[0]) + sh_ref[0]
    r = _dot(h.astype(BF16), w_ref[...])
    ka = r[:, W_A:2 * W_A]
    va = r[:, 2 * W_A:3 * W_A]
    ka_ref[...] = ka
    va_ref[...] = va
    qab_ref[...] = (r[:, 0:W_A] * (HD_A ** -0.5)).astype(BF16)
    kab_ref[...] = ka.astype(BF16)
    vab_ref[...] = va.astype(BF16)
    gla_ref[...] = r[:, OFF_GLA:OFF_LRU]
    lru_ref[...] = r[:, OFF_LRU:IN_PAD]


def _premix(x2, g, sc, sh, w_in_p, seq, tm):
    n, d = x2.shape
    per_b = seq // tm
    tok = lambda i: (i, 0)
    bat = lambda i: (i // per_b, 0, 0)
    const = lambda i: (0, 0)
    outs = [((n, W_A), F32), ((n, W_A), F32), ((n, W_A), BF16), ((n, W_A), BF16), ((n, W_A), BF16),
            ((n, GLA_W), F32), ((n, 2 * W_C), F32)]
    return pl.pallas_call(
        _premix_kernel,
        grid=(n // tm,),
        in_specs=[pl.BlockSpec((tm, d), tok), pl.BlockSpec((1, d), const),
                  pl.BlockSpec((1, 1, d), bat), pl.BlockSpec((1, 1, d), bat),
                  pl.BlockSpec((d, IN_PAD), const)],
        out_specs=[pl.BlockSpec((tm, s[1]), tok) for s, _ in outs],
        out_shape=[jax.ShapeDtypeStruct(s, t) for s, t in outs],
        compiler_params=_cparams(("arbitrary",)),
        name="premix_proj",
    )(x2, g.reshape(1, d), sc, sh, w_in_p)


def _attn_kernel(q_ref, k_ref, v_ref, u_ref, o_ref, acc_ref, c_ref, *, tq, tk, off):
    i = pl.program_id(2)
    lane = lax.broadcasted_iota(jnp.int32, (1, LANE), 1)
    q = q_ref[0]
    qz = jnp.zeros_like(q)
    head_lanes = (lane < HD_A, lane >= HD_A)
    qh = tuple(jnp.where(m, q, qz) for m in head_lanes)
    acc_ref[...] = jnp.zeros_like(acc_ref)
    c_ref[...] = jnp.zeros_like(c_ref)
    row = lax.broadcasted_iota(jnp.int32, (tq, tk), 0)
    col = lax.broadcasted_iota(jnp.int32, (tq, tk), 1)
    qpos0 = off + i * tq
    n_keys = qpos0 + tq - 1
    nk = (n_keys + tk - 1) // tk
    u = u_ref[...]

    def block(j, masked):
        ks = pl.multiple_of(j * tk, tk)
        kb = k_ref[0, pl.ds(ks, tk), :]
        vb = v_ref[0, pl.ds(ks, tk), :]
        vz = jnp.zeros_like(vb)
        mask = (col + ks) < (row + qpos0)
        for h in range(2):
            s = _dot_nt(qh[h], kb)
            l1p = jnp.log(1.0 + jnp.exp(-jnp.abs(s)))
            lf = -(jnp.maximum(s, 0.0) + l1p)
            if masked:
                lf = jnp.where(mask, lf, 0.0)
            lb = jnp.minimum(s, 0.0) - l1p
            hi, lo = _split_bf16(lf)
            cr = _dot(hi, u) + _dot(lo, u)
            c = c_ref[h]
            cfull = jnp.concatenate([c] * (tk // LANE), axis=1)
            w = jnp.exp(lb + cr[:, :tk] + cfull)
            if masked:
                w = jnp.where(mask, w, 0.0)
            vh = jnp.where(head_lanes[h], vb, vz)
            acc_ref[...] += _dot(w.astype(BF16), vh)
            c_ref[h] = c + cr[:, tk:]
        return jnp.max(c_ref[...]) > LOG_WEIGHT_FLOOR

    go = block(nk - 1, True)

    def body(carry):
        jj, _ = carry
        return jj + 1, block(nk - 1 - jj, False)

    lax.while_loop(lambda carry: jnp.logical_and(carry[0] < nk, carry[1]), body, (jnp.int32(1), go))
    o_ref[0] = acc_ref[...].astype(o_ref.dtype)


def _attn_umat(tk):
    jp = np.arange(tk)[:, None]
    j = np.arange(tk)[None, :]
    u = np.concatenate([(jp > j).astype(np.float32), np.ones((tk, LANE), np.float32)], axis=1)
    return jnp.asarray(u, BF16)


def _attention(q, k, v, off):
    b, tq_all, _ = q.shape
    tk_all = k.shape[1]
    tq = min(ATTN_TILE, tq_all)
    tk = ATTN_TILE
    assert tk_all % tk == 0 and tq_all % tq == 0 and off + tq_all - 1 <= tk_all
    assert off % tk == 0 and tk % tq == 0
    kern = functools.partial(_attn_kernel, tq=tq, tk=tk, off=off)
    return pl.pallas_call(
        kern,
        grid=(b, W_A // LANE, tq_all // tq),
        in_specs=[pl.BlockSpec((1, tq, LANE), lambda bi, hp, i: (bi, i, hp)),
                  pl.BlockSpec((1, tk_all, LANE), lambda bi, hp, i: (bi, 0, hp)),
                  pl.BlockSpec((1, tk_all, LANE), lambda bi, hp, i: (bi, 0, hp)),
                  pl.BlockSpec((tk, tk + LANE), lambda bi, hp, i: (0, 0))],
        out_specs=pl.BlockSpec((1, tq, LANE), lambda bi, hp, i: (bi, i, hp)),
        out_shape=jax.ShapeDtypeStruct((b, tq_all, W_A), BF16),
        scratch_shapes=[pltpu.VMEM((tq, LANE), F32), pltpu.VMEM((2, tq, LANE), F32)],
        compiler_params=_cparams(("arbitrary", "arbitrary", "arbitrary")),
        name="stick_breaking_attn",
    )(q, k, v, _attn_umat(tk))


def _gla_levels(c):
    return int(math.log2(c))


def _gla_mats(c):
    t = np.arange(c)[:, None]
    s = np.arange(c)[None, :]
    mats = [(s <= t).astype(np.float32), (s > t).astype(np.float32)]
    for lv in range(_gla_levels(c)):
        m = 1 << lv
        ref = (t // (2 * m)) * (2 * m) + m - 1
        mats.append(((s > ref) & (s <= t)).astype(np.float32) - ((s > t) & (s <= ref)).astype(np.float32))
    return jnp.asarray(np.concatenate(mats, axis=0), BF16)


def _gla_kernel(in_ref, wg_ref, bg_ref, gn_ref, mall_ref, s0_ref, ob_ref, sout_ref, st_ref, *, C, G):
    ci = pl.program_id(1)

    @pl.when(ci == 0)
    def _():
        st_ref[...] = s0_ref[0]

    for sub in range(G):
        _gla_chunk(in_ref, wg_ref, bg_ref, gn_ref, mall_ref, ob_ref, st_ref, C, sub * C)

    @pl.when(ci == pl.num_programs(1) - 1)
    def _():
        sout_ref[0] = st_ref[...]


def _gla_chunk(in_ref, wg_ref, bg_ref, gn_ref, mall_ref, ob_ref, st_ref, C, r0):
    rows = slice(r0, r0 + C)
    q = in_ref[rows, 0:HP_B] * (DK_B ** -0.5)
    k = in_ref[rows, HP_B:2 * HP_B]
    v = in_ref[rows, 2 * HP_B:3 * HP_B]
    gb = in_ref[rows, 3 * HP_B:4 * HP_B]
    rb = in_ref[rows, 4 * HP_B:4 * HP_B + LANE]
    lg = _log_sigmoid(_dot(rb.astype(BF16), wg_ref[...]) + bg_ref[...]) * (1.0 / GATE_TEMP_B)
    hi, lo = _split_bf16(lg)
    mall = mall_ref[...]
    dall = _dot(mall, hi) + _dot(mall, lo)
    eb = jnp.exp(dall[0:C])
    elast = jnp.exp(dall[C:2 * C])
    eb_last = eb[C - 1:C, :]
    rowi = lax.broadcasted_iota(jnp.int32, (C, 1), 0)
    row = lax.broadcasted_iota(jnp.int32, (C, C), 0)
    col = lax.broadcasted_iota(jnp.int32, (C, C), 1)
    qe = (q * eb).astype(BF16)
    ke = (k * elast).astype(BF16)
    qb = q.astype(BF16)
    kb = k.astype(BF16)
    vb = v.astype(BF16)
    zero = jnp.zeros_like(q)
    lv_q, lv_k = [], []
    for lv in range(_gla_levels(C)):
        e = jnp.exp(-jnp.abs(dall[(2 + lv) * C:(3 + lv) * C]))
        second = ((rowi >> lv) & 1) == 1
        lv_q.append(jnp.where(second, q * e, zero).astype(BF16))
        lv_k.append(jnp.where(second, zero, k * e).astype(BF16))
    outs = []
    for h in range(H_B):
        sl = slice(h * LANE, (h + 1) * LANE)
        att = jnp.where(row == col, _dot_nt(qb[:, sl], kb[:, sl]), 0.0)
        for lv in range(_gla_levels(C)):
            same = (row >> (lv + 1)) == (col >> (lv + 1))
            att = att + jnp.where(same, _dot_nt(lv_q[lv][:, sl], lv_k[lv][:, sl]), 0.0)
        st = st_ref[h]
        o = _dot(att.astype(BF16), vb[:, sl]) + _dot_nt(qe[:, sl], st.astype(BF16))
        st_ref[h] = st * eb_last[:, sl] + _dot(v[:, sl].T.astype(BF16), ke[:, sl])
        ms = jnp.sum(o * o, axis=-1, keepdims=True) * (1.0 / DV_B)
        on = o * lax.rsqrt(ms + EPS) * gn_ref[:, sl]
        g = gb[:, sl]
        outs.append((on * (g * _sigmoid(g))).astype(BF16))
    ob_ref[rows, :] = jnp.concatenate(outs, axis=1)


def _gla(gla_in, wg_p, bg_p, gn_p, s0t, seq):
    n = gla_in.shape[0]
    b = n // seq
    c = min(GLA_CHUNK, seq)
    g = math.gcd(GLA_CHUNKS_PER_STEP, seq // c)
    per_b = seq // (c * g)
    mall = _gla_mats(c)
    const2 = lambda bi, ci: (0, 0)
    return pl.pallas_call(
        functools.partial(_gla_kernel, C=c, G=g),
        grid=(b, per_b),
        in_specs=[pl.BlockSpec((c * g, GLA_W), lambda bi, ci: (bi * per_b + ci, 0)),
                  pl.BlockSpec((LANE, HP_B), const2), pl.BlockSpec((1, HP_B), const2),
                  pl.BlockSpec((1, HP_B), const2), pl.BlockSpec(mall.shape, const2),
                  pl.BlockSpec((1, H_B, LANE, LANE), lambda bi, ci: (bi, 0, 0, 0))],
        out_specs=[pl.BlockSpec((c * g, HP_B), lambda bi, ci: (bi * per_b + ci, 0)),
                   pl.BlockSpec((1, H_B, LANE, LANE), lambda bi, ci: (bi, 0, 0, 0))],
        out_shape=[jax.ShapeDtypeStruct((n, HP_B), BF16),
                   jax.ShapeDtypeStruct((b, H_B, LANE, LANE), F32)],
        scratch_shapes=[pltpu.VMEM((H_B, LANE, LANE), F32)],
        compiler_params=_cparams(("arbitrary", "arbitrary")),
        name="gla_chunked",
    )(gla_in, wg_p, bg_p, gn_p, mall, s0t)


def _lru_kernel(in_ref, cw_ref, cb_ref, wa_ref, ba_ref, wx_ref, bx_ref, lam_ref, buf0_ref, h0_ref,
                oc_ref, conv_ref, hout_ref, xp_ref, hc_ref, *, C):
    ci = pl.program_id(1)

    @pl.when(ci == 0)
    def _():
        xp_ref[0:SUBLANE] = buf0_ref[0]
        hc_ref[...] = h0_ref[0]

    x = in_ref[:, 0:W_C]
    y = in_ref[:, W_C:2 * W_C]
    xp_ref[SUBLANE:SUBLANE + C] = x
    xc = cb_ref[...]
    for j in range(CONV_W):
        xc = xc + xp_ref[pl.ds(SUBLANE - (CONV_W - 1) + j, C), :] * cw_ref[j:j + 1, :]
    xcb = xc.astype(BF16)
    r = _sigmoid(_dot(xcb, wa_ref[...]) + ba_ref[...])
    gi = _sigmoid(_dot(xcb, wx_ref[...]) + bx_ref[...])
    log_a = RG_C * r * _log_sigmoid(lam_ref[...])
    a = jnp.exp(log_a)
    u = jnp.sqrt(1.0 - jnp.exp(2.0 * log_a)) * (gi * xc)
    rowi = lax.broadcasted_iota(jnp.int32, (C, 1), 0)
    d = 1
    while d < C:
        keep = rowi >= d
        a_s = pltpu.roll(a, d, axis=0)
        u_s = pltpu.roll(u, d, axis=0)
        u = jnp.where(keep, a * u_s + u, u)
        a = jnp.where(keep, a * a_s, a)
        d *= 2
    hseq = u + a * hc_ref[...]
    hc_ref[...] = hseq[C - 1:C, :]
    gelu = 0.5 * y * (1.0 + jnp.tanh(math.sqrt(2.0 / math.pi) * (y + 0.044715 * (y * y * y))))
    oc_ref[...] = (hseq * gelu).astype(BF16)
    tail = xp_ref[C:C + SUBLANE]
    xp_ref[0:SUBLANE] = tail

    @pl.when(ci == pl.num_programs(1) - 1)
    def _():
        conv_ref[0] = tail
        hout_ref[0] = hseq[C - 1:C, :]


def _lru(lru_in, cw, cb, wa_bd, ba, wx_bd, bx, lam, buf0, h0, seq):
    n = lru_in.shape[0]
    b = n // seq
    c = min(LRU_CHUNK, seq)
    per_b = seq // c
    const2 = lambda bi, ci: (0, 0)
    vec = pl.BlockSpec((1, W_C), const2)
    return pl.pallas_call(
        functools.partial(_lru_kernel, C=c),
        grid=(b, per_b),
        in_specs=[pl.BlockSpec((c, 2 * W_C), lambda bi, ci: (bi * per_b + ci, 0)),
                  pl.BlockSpec((CONV_W, W_C), const2), vec,
                  pl.BlockSpec((W_C, W_C), const2), vec, pl.BlockSpec((W_C, W_C), const2), vec, vec,
                  pl.BlockSpec((1, SUBLANE, W_C), lambda bi, ci: (bi, 0, 0)),
                  pl.BlockSpec((1, 1, W_C), lambda bi, ci: (bi, 0, 0))],
        out_specs=[pl.BlockSpec((c, W_C), lambda bi, ci: (bi * per_b + ci, 0)),
                   pl.BlockSpec((1, SUBLANE, W_C), lambda bi, ci: (bi, 0, 0)),
                   pl.BlockSpec((1, 1, W_C), lambda bi, ci: (bi, 0, 0))],
        out_shape=[jax.ShapeDtypeStruct((n, W_C), BF16),
                   jax.ShapeDtypeStruct((b, SUBLANE, W_C), F32),
                   jax.ShapeDtypeStruct((b, 1, W_C), F32)],
        scratch_shapes=[pltpu.VMEM((c + SUBLANE, W_C), F32), pltpu.VMEM((1, W_C), F32)],
        compiler_params=_cparams(("arbitrary", "arbitrary")),
        name="conv_rglru",
    )(lru_in, cw, cb, wa_bd, ba, wx_bd, bx, lam, buf0, h0)


def _postmix_kernel(x_ref, oa_ref, ob_ref, oc_ref, wa_ref, wb_ref, wc_ref, gpm_ref, gpf_ref,
                    gt_ref, sc_ref, sh_ref, wr_hi_ref, wr_lo_ref, br_ref, tri_ref, ustrict_ref,
                    xm_ref, h2_ref, route_ref, stat_ref, *, tm):
    y =_dot(oa_ref[...], wa_ref[...]) + _dot(ob_ref[...], wb_ref[...]) + _dot(oc_ref[...], wc_ref[...])
    xm = x_ref[...] + gt_ref[0] * _rms(y, gpm_ref[...])
    xm_ref[...] = xm
    h2 = _rms(xm, gpf_ref[...]) * (1.0 + sc_ref[0]) + sh_ref[0]
    h2_ref[...] = h2
    hi, lo = _split_bf16(h2)
    wh = wr_hi_ref[...]
    logits = _dot(hi, wh) + _dot(lo, wh) + _dot(hi, wr_lo_ref[...]) + br_ref[...]
    lane = lax.broadcasted_iota(jnp.int32, (tm, LANE), 1)
    lane_f = lane.astype(F32)
    neg = jnp.float32(-jnp.inf)
    vals, hots = [], []
    for _ in range(TOP_K):
        m = jnp.max(logits, axis=-1, keepdims=True)
        idx = jnp.min(jnp.where(logits == m, lane_f, float(LANE)), axis=-1, keepdims=True)
        hot = lane_f == idx
        logits = jnp.where(hot, neg, logits)
        vals.append(m)
        hots.append(hot)
    ex = [jnp.exp(vk - vals[0]) for vk in vals]
    inv = 1.0 / (ex[0] + ex[1] + ex[2] + ex[3])
    sel = jnp.zeros((tm, LANE), F32)
    for hot in hots:
        sel = jnp.where(hot, 1.0, sel)
    selb = sel.astype(BF16)
    tile_cnt = jnp.sum(sel, axis=0, keepdims=True)
    groups8 = jnp.floor((tile_cnt + (SUBLANE - 1.0)) * (1.0 / SUBLANE))
    g8b = jnp.broadcast_to(groups8, (SUBLANE, LANE)).astype(BF16)
    loc_start = _dot(g8b, ustrict_ref[...])[0:1] * float(SUBLANE)
    local = _dot(tri_ref[...], selb) + loc_start
    route = jnp.zeros((tm, LANE), F32)
    for kk in range(TOP_K):
        p_k = jnp.sum(jnp.where(hots[kk], local, 0.0), axis=-1, keepdims=True)
        route = jnp.where(lane == kk, ex[kk] * inv, route)
        route = jnp.where(lane == TOP_K + kk, p_k, route)
    route_ref[...] = route
    srow = lax.broadcasted_iota(jnp.int32, (SUBLANE, LANE), 0)
    stat_ref[...] = jnp.where(srow == 0, groups8 * float(SUBLANE), jnp.where(srow == 1, loc_start, 0.0))


def _postmix(x2, oa, ob, oc, wo_a, wo_b, wo_c, gpm, gpf, gt, sc, sh, wr_hi, wr_lo, br, seq, tm):
    n, d = x2.shape
    per_b = seq // tm
    tok = lambda i: (i, 0)
    bat = lambda i: (i // per_b, 0, 0)
    const = lambda i: (0, 0)
    tri = jnp.asarray(np.tril(np.ones((tm, tm), np.float32), -1), BF16)
    ustrict = jnp.asarray(np.triu(np.ones((LANE, LANE), np.float32), 1), BF16)
    full = lambda a: pl.BlockSpec(a.shape, const)
    nt = n // tm
    return pl.pallas_call(
        functools.partial(_postmix_kernel, tm=tm),
        grid=(nt,),
        in_specs=[pl.BlockSpec((tm, d), tok), pl.BlockSpec((tm, W_A), tok),
                  pl.BlockSpec((tm, HP_B), tok), pl.BlockSpec((tm, W_C), tok),
                  full(wo_a), full(wo_b), full(wo_c),
                  pl.BlockSpec((1, d), const), pl.BlockSpec((1, d), const),
                  pl.BlockSpec((1, 1, d), bat), pl.BlockSpec((1, 1, d), bat), pl.BlockSpec((1, 1, d), bat),
                  full(wr_hi), full(wr_lo), pl.BlockSpec((1, LANE), const), full(tri), full(ustrict)],
        out_specs=[pl.BlockSpec((tm, d), tok), pl.BlockSpec((tm, d), tok),
                   pl.BlockSpec((tm, LANE), tok), pl.BlockSpec((SUBLANE, LANE), tok)],
        out_shape=[jax.ShapeDtypeStruct((n, d), F32), jax.ShapeDtypeStruct((n, d), F32),
                   jax.ShapeDtypeStruct((n, LANE), F32), jax.ShapeDtypeStruct((nt * SUBLANE, LANE), F32)],
        compiler_params=_cparams(("arbitrary",)),
        name="postmix_router",
    )(x2, oa, ob, oc, wo_a, wo_b, wo_c, gpm.reshape(1, d), gpf.reshape(1, d), gt, sc, sh,
      wr_hi, wr_lo, br, tri, ustrict)


def _run_copies(tab_ref, tm, make_copy, wait, tile=0):
    sizes = [1 << b for b in range(int(math.log2(tm)), int(math.log2(SUBLANE)) - 1, -1)]

    def per_expert(e, carry):
        n = tab_ref[tile, 0, e]
        loc = tab_ref[tile, 0, N_EXPERTS + e]
        dst = tab_ref[tile, 0, 2 * N_EXPERTS + e]
        for sz in sizes:
            done = n & ~(2 * sz - 1)

            @pl.when((n & sz) != 0)
            def _():
                cp = make_copy(pl.multiple_of(loc + done, SUBLANE), pl.multiple_of(dst + done, SUBLANE), sz)
                if wait:
                    cp.wait()
                else:
                    cp.start()
        return carry

    lax.fori_loop(0, N_EXPERTS, per_expert, 0)


def _local_rows(tm):
    return tm * TOP_K + N_EXPERTS * SUBLANE


def _slot_onehot(route_ref, tm):
    nloc = _local_rows(tm)
    pos = lax.broadcasted_iota(jnp.int32, (tm, nloc), 1).astype(F32)
    route = route_ref[...]

    def build(values):
        m = jnp.zeros((tm, nloc), F32)
        for kk in range(TOP_K):
            m = jnp.where(route[:, TOP_K + kk:TOP_K + kk + 1] == pos, values[kk], m)
        return m

    return route, build


def _dispatch_kernel(tab_ref, tabp_ref, tail_ref, route_ref, h_ref, *rest, tm, first, later_tile):
    i = pl.program_id(0)
    slot = i % 2
    if first:
        later_ref, xs_ref, sorted_ref, sem, zero_ref = rest
        bm = MOE_BLOCK_ROWS

        @pl.when(i == 0)
        def _():
            zero_ref[...] = jnp.zeros_like(zero_ref)
            fill = lambda loc, dst, sz: pltpu.make_async_copy(zero_ref.at[pl.ds(0, sz)],
                                                              xs_ref.at[pl.ds(dst, sz)], sem.at[0])
            n_later, later_tm = later_ref.shape[0], later_tile

            def block_fill(wait):
                def body(g, carry):
                    cp = fill(0, pl.multiple_of(g * bm, bm), bm)
                    cp.wait() if wait else cp.start()
                    return carry
                lax.fori_loop(tail_ref[0, 0, 3 * N_EXPERTS], xs_ref.shape[0] // bm, body, 0)

            for wait in (False, True):
                _run_copies(tail_ref, bm, fill, wait)
                block_fill(wait)
                for ti in range(n_later):
                    _run_copies(later_ref, later_tm, fill, wait, tile=ti)
    else:
        _, xs_ref, sorted_ref, sem = rest
    _, build = _slot_onehot(route_ref, tm)
    perm = build([1.0] * TOP_K).astype(BF16)
    sorted_ref[slot] = lax.dot_general(perm, h_ref[...].astype(BF16), (((0,), (0,)), ((), ())),
                                       preferred_element_type=F32)

    def runs(s):
        return lambda loc, dst, sz: pltpu.make_async_copy(sorted_ref.at[s, pl.ds(loc, sz)],
                                                          xs_ref.at[pl.ds(dst, sz)], sem.at[s])

    _run_copies(tab_ref, tm, runs(slot), wait=False)

    @pl.when(i > 0)
    def _():
        _run_copies(tabp_ref, tm, runs(1 - slot), wait=True)

    @pl.when(i == pl.num_programs(0) - 1)
    def _():
        _run_copies(tab_ref, tm, runs(slot), wait=True)


def _dispatch(tab, tail, route, h2, xs, n_pad, tm, later=None, later_tm=None):
    n, d = h2.shape
    tok = lambda i: (i, 0)
    first = xs is None
    in_specs = [pl.BlockSpec((1, 1, LANE), lambda i: (i, 0, 0), memory_space=pltpu.SMEM),
                pl.BlockSpec((1, 1, LANE), lambda i: (jnp.maximum(i - 1, 0), 0, 0), memory_space=pltpu.SMEM),
                pl.BlockSpec((1, 1, LANE), lambda i: (0, 0, 0), memory_space=pltpu.SMEM),
                pl.BlockSpec((tm, LANE), tok), pl.BlockSpec((tm, d), tok)]
    scratch = [pltpu.VMEM((2, _local_rows(tm), d), F32), pltpu.SemaphoreType.DMA((2,))]
    args = [tab, tab, tail, route, h2]
    if first:
        scratch.append(pltpu.VMEM((MOE_BLOCK_ROWS, d), F32))
        in_specs.append(pl.BlockSpec(later.shape, lambda i: (0, 0, 0), memory_space=pltpu.SMEM))
        args.append(later)
    else:
        in_specs.append(pl.BlockSpec(memory_space=pl.ANY))
        args.append(xs)
    return pl.pallas_call(
        functools.partial(_dispatch_kernel, tm=tm, first=first, later_tile=later_tm),
        grid=(n // tm,),
        in_specs=in_specs,
        out_specs=pl.BlockSpec(memory_space=pl.ANY),
        out_shape=jax.ShapeDtypeStruct((n_pad, d), F32),
        scratch_shapes=scratch,
        input_output_aliases={} if first else {5: 0},
        compiler_params=_cparams(("arbitrary",)),
        name="moe_dispatch",
    )(*args)


def _expert_kernel(be_ref, nv_ref, x_ref, wgu_ref, bgu_ref, wd_ref, bd_ref, y_ref, wgu_bf, wd_bf):
    g = pl.program_id(0)

    @pl.when(jnp.logical_or(g == 0, be_ref[g] != be_ref[jnp.maximum(g - 1, 0)]))
    def _():
        wgu_bf[...] = wgu_ref[0, 0].astype(BF16)
        wd_bf[...] = wd_ref[0, 0].astype(BF16)

    @pl.when(g < nv_ref[0])
    def _():
        gu = _dot(x_ref[...].astype(BF16), wgu_bf[...]) + bgu_ref[0, 0]
        gt = jnp.minimum(gu[:, :D_FF], SWIGLU_LIMIT)
        up = jnp.clip(gu[:, D_FF:], -SWIGLU_LIMIT, SWIGLU_LIMIT)
        act = (up + 1.0) * (gt * _sigmoid(SWIGLU_ALPHA * gt))
        y_ref[...] = _dot(act.astype(BF16), wd_bf[...]) + bd_ref[0, 0]

    @pl.when(g >= nv_ref[0])
    def _():
        y_ref[...] = jnp.zeros_like(y_ref)


def _experts(l, block_e, nvalid, xs, wgu, bgu, wd, bd):
    n_pad, d = xs.shape
    depth = wgu.shape[0]
    bm = MOE_BLOCK_ROWS
    nb = n_pad // bm
    grid_spec = pltpu.PrefetchScalarGridSpec(
        num_scalar_prefetch=2,
        grid=(nb,),
        in_specs=[pl.BlockSpec((bm, d), lambda g, be, nv: (g, 0)),
                  pl.BlockSpec((1, 1, d, 2 * D_FF), lambda g, be, nv: (l, be[g], 0, 0)),
                  pl.BlockSpec((1, 1, 1, 2 * D_FF), lambda g, be, nv: (l, be[g], 0, 0)),
                  pl.BlockSpec((1, 1, D_FF, d), lambda g, be, nv: (l, be[g], 0, 0)),
                  pl.BlockSpec((1, 1, 1, d), lambda g, be, nv: (l, be[g], 0, 0))],
        out_specs=pl.BlockSpec((bm, d), lambda g, be, nv: (g, 0)),
        scratch_shapes=[pltpu.VMEM((d, 2 * D_FF), BF16), pltpu.VMEM((D_FF, d), BF16)],
    )
    return pl.pallas_call(
        _expert_kernel,
        grid_spec=grid_spec,
        out_shape=jax.ShapeDtypeStruct((n_pad, d), F32),
        compiler_params=_cparams(("arbitrary",)),
        name="moe_experts",
    )(block_e, nvalid, xs, wgu, bgu.reshape(depth, N_EXPERTS, 1, -1), wd, bd.reshape(depth, N_EXPERTS, 1, -1))


def _combine_kernel(tab_ref, tabn_ref, route_ref, ys_ref, xm_ref, gt_ref, g_ref, o_ref, buf_ref, sem, *, tm):
    i = pl.program_id(0)
    slot = i % 2

    def runs(s):
        return lambda loc, dst, sz: pltpu.make_async_copy(ys_ref.at[pl.ds(dst, sz)],
                                                          buf_ref.at[s, pl.ds(loc, sz)], sem.at[s])

    @pl.when(i == 0)
    def _():
        buf_ref[...] = jnp.zeros_like(buf_ref)
        _run_copies(tab_ref, tm, runs(0), wait=False)

    @pl.when(i + 1 < pl.num_programs(0))
    def _():
        _run_copies(tabn_ref, tm, runs(1 - slot), wait=False)

    route, build = _slot_onehot(route_ref, tm)
    gate_m = build([route[:, kk:kk + 1] for kk in range(TOP_K)])
    g_hi, g_lo = _split_bf16(gate_m)
    _run_copies(tab_ref, tm, runs(slot), wait=True)
    b_hi, b_lo = _split_bf16(buf_ref[slot])
    y = _dot(g_hi, b_hi) + _dot(g_lo, b_hi) + _dot(g_hi, b_lo)
    o_ref[...] = xm_ref[...] + gt_ref[0] * _rms(y, g_ref[...])


def _combine(tab, route, ys, xm, gt, g, seq, tm):
    n, d = xm.shape
    per_b = seq // tm
    nt = n // tm
    tok = lambda i: (i, 0)
    return pl.pallas_call(
        functools.partial(_combine_kernel, tm=tm),
        grid=(nt,),
        in_specs=[pl.BlockSpec((1, 1, LANE), lambda i: (i, 0, 0), memory_space=pltpu.SMEM),
                  pl.BlockSpec((1, 1, LANE), lambda i: (jnp.minimum(i + 1, nt - 1), 0, 0),
                               memory_space=pltpu.SMEM),
                  pl.BlockSpec((tm, LANE), tok), pl.BlockSpec(memory_space=pl.ANY),
                  pl.BlockSpec((tm, d), tok),
                  pl.BlockSpec((1, 1, d), lambda i: (i // per_b, 0, 0)),
                  pl.BlockSpec((1, d), lambda i: (0, 0))],
        out_specs=pl.BlockSpec((tm, d), tok),
        out_shape=jax.ShapeDtypeStruct((n, d), F32),
        scratch_shapes=[pltpu.VMEM((2, _local_rows(tm), d), F32), pltpu.SemaphoreType.DMA((2,))],
        compiler_params=_cparams(("arbitrary",)),
        name="moe_combine",
    )(tab, tab, route, ys, xm, gt, g.reshape(1, d))


def _pad_heads(w, width):
    lead = w.shape[:-1]
    w = w.reshape(lead + (H_B, width))
    w = jnp.pad(w, [(0, 0)] * len(lead) + [(0, 0), (0, LANE - width)])
    return w.reshape(lead + (HP_B,))


def _layer_weights(l, w_in, w_gla_gate, b_gla_gate, g_gla_norm, w_rg_a, w_rg_x, w_out, w_router, b_router):
    d = D_MODEL
    wi = w_in[l]
    o = 3 * W_A
    qb = wi[:, o:o + H_B * DK_B]
    kb = wi[:, o + 192:o + 384]
    vb = wi[:, o + 384:o + 768]
    gb = wi[:, o + 768:o + 1152]
    rb = wi[:, o + 1152:o + 1168]
    xy = wi[:, o + 1168:]
    w_in_p = jnp.concatenate(
        [wi[:, :o], _pad_heads(qb, DK_B), _pad_heads(kb, DK_B), _pad_heads(vb, DV_B), _pad_heads(gb, DV_B),
         jnp.pad(rb, ((0, 0), (0, LANE - GATE_RANK_B))), xy], axis=1).astype(BF16)
    wg_p = jnp.pad(_pad_heads(w_gla_gate[l], DK_B), ((0, LANE - GATE_RANK_B), (0, 0))).astype(BF16)
    bg_p = _pad_heads(b_gla_gate[l].reshape(1, -1), DK_B)
    gn_p = jnp.tile(jnp.pad(g_gla_norm[l], (0, LANE - DV_B)), H_B).reshape(1, HP_B)
    wa_bd = jax.scipy.linalg.block_diag(*[w_rg_a[l, i] for i in range(NBLK_C)]).astype(BF16)
    wx_bd = jax.scipy.linalg.block_diag(*[w_rg_x[l, i] for i in range(NBLK_C)]).astype(BF16)
    wo = w_out[l]
    wo_a = wo[:W_A].astype(BF16)
    wo_b = jnp.pad(wo[W_A:W_A + W_B].reshape(H_B, DV_B, d), ((0, 0), (0, LANE - DV_B), (0, 0)))
    wo_b = wo_b.reshape(HP_B, d).astype(BF16)
    wo_c = wo[W_A + W_B:].astype(BF16)
    wr = jnp.pad(w_router[l], ((0, 0), (0, LANE - N_EXPERTS)))
    wr_hi = wr.astype(BF16)
    wr_lo = (wr - wr_hi.astype(F32)).astype(BF16)
    br = jnp.pad(b_router[l], (0, LANE - N_EXPERTS), constant_values=-1e30).reshape(1, LANE)
    return dict(w_in_p=w_in_p, wg_p=wg_p, bg_p=bg_p, gn_p=gn_p, wa_bd=wa_bd, wx_bd=wx_bd,
                wo_a=wo_a, wo_b=wo_b, wo_c=wo_c, wr_hi=wr_hi, wr_lo=wr_lo, br=br)


def _pad_state(s):
    st = jnp.swapaxes(s, -1, -2)
    return jnp.pad(st, ((0, 0), (0, 0), (0, LANE - DV_B), (0, LANE - DK_B)))


def _unpad_state(st):
    return jnp.swapaxes(st[:, :, :DV_B, :DK_B], -1, -2)


def kernel(x_prompt, x_sample, c_prompt, c_sample, cache_k_sb, cache_v_sb, state_gla, state_conv, state_lru, w_ada, b_ada, g_pre_mix, g_post_mix, g_pre_ff, g_post_ff, w_in, w_gla_gate, b_gla_gate, g_gla_norm, w_conv, b_conv, w_rg_a, b_rg_a, w_rg_x, b_rg_x, lru_lambda, w_out, w_router, b_router, w_gate_up, b_gate_up, w_down, b_down):
    depth = w_ada.shape[0]
    d = D_MODEL
    groups = []
    for x, past in ((x_prompt, False), (x_sample, True)):
        b, t, _ = x.shape
        groups.append(dict(b=b, t=t, tm=min(TOKEN_TILE, t), n=b * t, past=past, x=x.reshape(b * t, d)))
    n_all = sum(g["n"] for g in groups)
    bm = MOE_BLOCK_ROWS
    n_tiles = sum(g["n"] // g["tm"] for g in groups)
    nb = -(-(n_all * TOP_K + n_tiles * N_EXPERTS * (SUBLANE - 1)) // bm) + N_EXPERTS
    n_pad = nb * bm

    nb_rows = sum(g["b"] for g in groups)
    c_all = jnp.concatenate([c_prompt, c_sample], axis=0)
    r_pad = -(-nb_rows // SUBLANE) * SUBLANE
    c_all = jnp.pad(c_all, ((0, r_pad - nb_rows), (0, 0)))
    mod = _modulation(c_all, w_ada, b_ada)

    st =[[[] for _ in range(5)] for _ in groups]
    for l in range(depth):
        lw = _layer_weights(l, w_in, w_gla_gate, b_gla_gate, g_gla_norm, w_rg_a, w_rg_x, w_out,
                            w_router, b_router)
        row0 = 0
        per_group = []
        for gi, g in enumerate(groups):
            b, t, tm, n = g["b"], g["t"], g["tm"], g["n"]
            m = mod[l, row0:row0 + b].reshape(b, 1, 6 * d)
            row0 += b
            sh_m, sc_m, gt_m, sh_f, sc_f, gt_f = [m[:, :, j * d:(j + 1) * d] for j in range(6)]
            ka, va, qab, kab, vab, gla_in, lru_in = _premix(g["x"], g_pre_mix[l], sc_m, sh_m, lw["w_in_p"], t, tm)
            q3 = qab.reshape(b, t, W_A)
            k3 = kab.reshape(b, t, W_A)
            v3 = vab.reshape(b, t, W_A)
            if g["past"]:
                n_past = cache_k_sb.shape[2]
                tk_all = -(-(n_past + t) // ATTN_TILE) * ATTN_TILE
                padk = ((0, 0), (0, tk_all - n_past - t), (0, 0))
                k3 = jnp.pad(jnp.concatenate([cache_k_sb[l].reshape(b, n_past, W_A).astype(BF16), k3], axis=1), padk)
                v3 = jnp.pad(jnp.concatenate([cache_v_sb[l].reshape(b, n_past, W_A).astype(BF16), v3], axis=1), padk)
                off = n_past
                s0 = state_gla[l]
                buf0 = state_conv[l]
                h0 = state_lru[l]
            else:
                off = 0
                s0 = jnp.zeros((b, H_B, DK_B, DV_B), F32)
                buf0 = jnp.zeros((b, CONV_W - 1, W_C), F32)
                h0 = jnp.zeros((b, W_C), F32)
            oa = _attention(q3, k3, v3, off).reshape(n, W_A)
            ob, s_new = _gla(gla_in, lw["wg_p"], lw["bg_p"], lw["gn_p"], _pad_state(s0), t)
            buf0p = jnp.pad(buf0, ((0, 0), (SUBLANE - (CONV_W - 1), 0), (0, 0)))
            oc, conv_new, h_new = _lru(lru_in, w_conv[l], b_conv[l].reshape(1, -1), lw["wa_bd"],
                                       b_rg_a[l].reshape(1, -1), lw["wx_bd"], b_rg_x[l].reshape(1, -1),
                                       lru_lambda[l].reshape(1, -1), buf0p, h0.reshape(b, 1, W_C), t)
            xm, h2, route, stat = _postmix(g["x"], oa, ob, oc, lw["wo_a"], lw["wo_b"], lw["wo_c"],
                                           g_post_mix[l], g_pre_ff[l], gt_m, sc_f, sh_f,
                                           lw["wr_hi"], lw["wr_lo"], lw["br"], t, tm)
            per_group.append(dict(xm=xm, h2=h2, route=route, stat=stat, gt_f=gt_f))
            new = (ka.reshape(b, t, W_A // HD_A, HD_A), va.reshape(b, t, W_A // HD_A, HD_A),
                   _unpad_state(s_new), conv_new[:, SUBLANE - (CONV_W - 1):], h_new.reshape(b, W_C))
            for j in range(5):
                st[gi][j].append(new[j])

        stat = jnp.concatenate([p["stat"].reshape(-1, SUBLANE, LANE)[:, :2, :N_EXPERTS] for p in per_group],
                               axis=0).astype(jnp.int32)
        run_rows, run_loc = stat[:, 0], stat[:, 1]
        before = jnp.cumsum(run_rows, axis=0) - run_rows
        counts = jnp.sum(run_rows, axis=0)
        padded = (counts + bm - 1) // bm * bm
        pad_end = jnp.cumsum(padded)
        pad_start = pad_end - padded
        block_start = jnp.arange(nb, dtype=jnp.int32) * bm
        block_e = jnp.minimum(jnp.sum((pad_end[None, :] <= block_start[:, None]).astype(jnp.int32), axis=1),
                              N_EXPERTS - 1)
        nvalid = (pad_end[-1] // bm).astype(jnp.int32).reshape(1)
        tab = jnp.concatenate([run_rows, run_loc, pad_start[None, :] + before, jnp.zeros_like(run_rows)],
                              axis=1).reshape(-1, 1, LANE)
        tile0 = 0
        for gi, g in enumerate(groups):
            nt = g["n"] // g["tm"]
            per_group[gi]["tab"] = tab[tile0:tile0 + nt]
            tile0 += nt
        zeros32 = jnp.zeros_like(counts)
        tail = jnp.concatenate([padded - counts, zeros32, pad_start + counts,
                                jnp.broadcast_to(nvalid, (N_EXPERTS,))]).reshape(1, 1, LANE)
        first, second = per_group
        xs = _dispatch(first["tab"], tail, first["route"], first["h2"], None, n_pad, groups[0]["tm"],
                       later=second["tab"], later_tm=groups[1]["tm"])
        xs = _dispatch(second["tab"], tail, second["route"], second["h2"], xs, n_pad, groups[1]["tm"])
        ys = _experts(l, block_e, nvalid, xs, w_gate_up, b_gate_up, w_down, b_down)
        for gi, g in enumerate(groups):
            p = per_group[gi]
            g["x"] = _combine(p["tab"], p["route"], ys, p["xm"], p["gt_f"], g_post_ff[l], g["t"], g["tm"])

    outs = [g["x"].reshape(g["b"], g["t"], d) for g in groups]
    for gi in range(len(groups)):
        outs.extend(jnp.stack(st[gi][j]) for j in range(5))
    return tuple(outs)
```

```python
import functools
import math

import numpy as np
import jax
import jax.numpy as jnp
from jax import lax
from jax.experimental import pallas as pl
from jax.experimental.pallas import tpu as pltpu

F32 = jnp.float32
BF16 = jnp.bfloat16

D_MODEL = 1024
EPS = 1e-6
HD_A = 64
W_A = 384
H_B = 4
DK_B = 48
DV_B = 96
W_B = 384
GATE_RANK_B = 16
GATE_TEMP_B = 16.0
W_C = 256
NBLK_C = 4
BW_C = 64
CONV_W = 4
RG_C = 8.0
N_EXPERTS = 32
TOP_K = 4
D_FF = 1024
SWIGLU_ALPHA = 1.702
SWIGLU_LIMIT = 7.0
LOG_WEIGHT_FLOOR = -120.0

LANE = 128
SUBLANE = 8
VMEM_LIMIT = 56 * 1024 * 1024

HP_B = H_B * LANE
OFF_GLA = 3 * W_A
GLA_W = 4 * HP_B + LANE
OFF_LRU = OFF_GLA + GLA_W
IN_PAD = OFF_LRU + 2 * W_C

TOKEN_TILE = 256
ATTN_TILE = 256
ATTN_KEY_TILE = 256
ATTN_SUBTILES = 4
GLA_CHUNK = 64
GLA_CHUNKS_PER_STEP = 4
LRU_CHUNK = 256
MOE_BLOCK_ROWS = 512


def _cparams(sem):
    return pltpu.CompilerParams(dimension_semantics=sem, vmem_limit_bytes=VMEM_LIMIT)


def _sigmoid(x):
    return 1.0 / (1.0 + jnp.exp(-x))


def _log_sigmoid(x):
    return jnp.minimum(x, 0.0) - jnp.log(1.0 + jnp.exp(-jnp.abs(x)))


def _split_bf16(x):
    hi = x.astype(BF16)
    lo = (x - hi.astype(F32)).astype(BF16)
    return hi, lo


def _dot(a, b):
    return jnp.dot(a, b, preferred_element_type=F32)


def _dot_nt(a, b):
    return lax.dot_general(a, b, (((1,), (1,)), ((), ())), preferred_element_type=F32)


def _rms(x, g):
    return x * lax.rsqrt(jnp.mean(x * x, axis=-1, keepdims=True) + EPS) * g


def _mod_kernel(c_ref, w_ref, b_ref, o_ref):
    c = c_ref[...]
    a = (c * _sigmoid(c)).astype(BF16)
    o_ref[0] = _dot(a, w_ref[0].astype(BF16)) + b_ref[0]


def _modulation(c_all, w_ada, b_ada):
    depth, d, n = w_ada.shape
    r = c_all.shape[0]
    tn = 1536
    return pl.pallas_call(
        _mod_kernel,
        grid=(depth, n // tn),
        in_specs=[pl.BlockSpec((r, d), lambda l, j: (0, 0)),
                  pl.BlockSpec((1, d, tn), lambda l, j: (l, 0, j)),
                  pl.BlockSpec((1, 1, tn), lambda l, j: (l, 0, j))],
        out_specs=pl.BlockSpec((1, r, tn), lambda l, j: (l, 0, j)),
        out_shape=jax.ShapeDtypeStruct((depth, r, n), F32),
        compiler_params=_cparams(("arbitrary", "arbitrary")),
        name="adaln_mod",
    )(c_all, w_ada, b_ada.reshape(depth, 1, n))


def _premix_kernel(x_ref, g_ref, sc_ref, sh_ref, w_ref,
                   ka_ref, va_ref, qab_ref, kab_ref, vab_ref, gla_ref, lru_ref):
    h = _rms(x_ref[...], g_ref[...]) * (1.0 + sc_ref[0]) + sh_ref[0]
    r = _dot(h.astype(BF16), w_ref[...])
    ka = r[:, W_A:2 * W_A]
    va = r[:, 2 * W_A:3 * W_A]
    ka_ref[...] = ka
    va_ref[...] = va
    qab_ref[...] = (r[:, 0:W_A] * (HD_A ** -0.5)).astype(BF16)
    kab_ref[...] = ka.astype(BF16)
    vab_ref[...] = va.astype(BF16)
    gla_ref[...] = r[:, OFF_GLA:OFF_LRU]
    lru_ref[...] = r[:, OFF_LRU:IN_PAD]


def _premix(x2, g, sc, sh, w_in_p, seq, tm):
    n, d = x2.shape
    per_b = seq // tm
    tok = lambda i: (i, 0)
    bat = lambda i: (i // per_b, 0, 0)
    const = lambda i: (0, 0)
    outs = [((n, W_A), F32), ((n, W_A), F32), ((n, W_A), BF16), ((n, W_A), BF16), ((n, W_A), BF16),
            ((n, GLA_W), F32), ((n, 2 * W_C), F32)]
    return pl.pallas_call(
        _premix_kernel,
        grid=(n // tm,),
        in_specs=[pl.BlockSpec((tm, d), tok), pl.BlockSpec((1, d), const),
                  pl.BlockSpec((1, 1, d), bat), pl.BlockSpec((1, 1, d), bat),
                  pl.BlockSpec((d, IN_PAD), const)],
        out_specs=[pl.BlockSpec((tm, s[1]), tok) for s, _ in outs],
        out_shape=[jax.ShapeDtypeStruct(s, t) for s, t in outs],
        compiler_params=_cparams(("arbitrary",)),
        name="premix_proj",
    )(x2, g.reshape(1, d), sc, sh, w_in_p)


def _attn_kernel(q_ref, k_ref, v_ref, *rest, tq, tk, off, has_past, nsub):
    if has_past:
        kp_ref, vp_ref, u_ref, o_ref, acc_ref, c_ref = rest
    else:
        u_ref, o_ref, acc_ref, c_ref = rest
    i = pl.program_id(2)
    lane = lax.broadcasted_iota(jnp.int32, (1, LANE), 1)
    head_lanes = (lane < HD_A, lane >= HD_A)
    acc_ref[...] = jnp.zeros_like(acc_ref)
    c_ref[...] = jnp.zeros_like(c_ref)
    row = lax.broadcasted_iota(jnp.int32, (tq, tk), 0)
    col = lax.broadcasted_iota(jnp.int32, (tq, tk), 1)
    u = u_ref[...]
    qh, qpos0, nk = [], [], []
    for a in range(nsub):
        q = q_ref[0, a * tq:(a + 1) * tq, :]
        qz = jnp.zeros_like(q)
        qh.append(tuple(jnp.where(m, q, qz) for m in head_lanes))
        qpos0.append(off + (i * nsub + a) * tq)
        nk.append((qpos0[a] + tq - 1 + tk - 1) // tk)

    def step(jj, masked):
        cmax = jnp.float32(-jnp.inf)
        for a in range(nsub):
            j = nk[a] - 1 - jj
            live = j >= 0
            ks = pl.multiple_of(jnp.maximum(j, 0) * tk, tk)
            if has_past and not masked:
                kb = kp_ref[0, pl.ds(ks, tk), :].astype(BF16)
                vb = vp_ref[0, pl.ds(ks, tk), :].astype(BF16)
            else:
                kb = k_ref[0, pl.ds(pl.multiple_of(ks - off, tk), tk), :]
                vb = v_ref[0, pl.ds(pl.multiple_of(ks - off, tk), tk), :]
            vz = jnp.zeros_like(vb)
            mask = (col + ks) < (row + qpos0[a])
            for h in range(2):
                s = _dot_nt(qh[a][h], kb)
                l1p = jnp.log(1.0 + jnp.exp(-jnp.abs(s)))
                lf = -(jnp.maximum(s, 0.0) + l1p)
                lf = jnp.where(mask, lf, 0.0) if masked else jnp.where(live, lf, 0.0)
                lb = jnp.minimum(s, 0.0) - l1p
                hi, lo = _split_bf16(lf)
                cr = _dot(hi, u) + _dot(lo, u)
                c = c_ref[a, h]
                cfull = jnp.concatenate([c] * (tk // LANE), axis=1)
                w = jnp.exp(lb + cr[:, :tk] + cfull)
                w = jnp.where(mask, w, 0.0) if masked else jnp.where(live, w, 0.0)
                vh = jnp.where(head_lanes[h], vb, vz)
                acc_ref[a] += _dot(w.astype(BF16), vh)
                c_ref[a, h] = c + cr[:, tk:]
            cmax = jnp.maximum(cmax, jnp.where(j > 0, jnp.max(c_ref[a]), -jnp.inf))
        return cmax > LOG_WEIGHT_FLOOR

    n_masked = max(1, tq // tk)
    for jj in range(n_masked):
        go = step(jj, True)

    def body(carry):
        jj, _ = carry
        return jj + 1, step(jj, False)

    lax.while_loop(lambda carry: jnp.logical_and(carry[0] < nk[-1], carry[1]), body, (jnp.int32(n_masked), go))
    for a in range(nsub):
        o_ref[0, a * tq:(a + 1) * tq, :] = acc_ref[a].astype(o_ref.dtype)


def _attn_umat(tk):
    jp = np.arange(tk)[:, None]
    j = np.arange(tk)[None, :]
    u = np.concatenate([(jp > j).astype(np.float32), np.ones((tk, LANE), np.float32)], axis=1)
    return jnp.asarray(u, BF16)


def _attention(q, k, v, kp=None, vp=None):
    b, tq_all, _ = q.shape
    tc = k.shape[1]
    has_past = kp is not None
    off = kp.shape[1] if has_past else 0
    tq = min(ATTN_TILE, tq_all)
    tk = ATTN_KEY_TILE
    assert tc % tk == 0 and tq_all % tq == 0 and tq_all <= tc
    assert off % tk == 0 and (tk % tq == 0 or tq % tk == 0)
    assert not has_past or (tq == tq_all and tq_all <= tk)
    nsub = math.gcd(ATTN_SUBTILES, tq_all // tq)
    kern = functools.partial(_attn_kernel, tq=tq, tk=tk, off=off, has_past=has_past, nsub=nsub)
    cur = pl.BlockSpec((1, tc, LANE), lambda bi, hp, i: (bi, 0, hp))
    in_specs = [pl.BlockSpec((1, nsub * tq, LANE), lambda bi, hp, i: (bi, i, hp)), cur, cur]
    args = [q, k, v]
    if has_past:
        past = pl.BlockSpec((1, off, LANE), lambda bi, hp, i: (bi, 0, hp))
        in_specs += [past, past]
        args += [kp, vp]
    in_specs.append(pl.BlockSpec((tk, tk + LANE), lambda bi, hp, i: (0, 0)))
    args.append(_attn_umat(tk))
    return pl.pallas_call(
        kern,
        grid=(b, W_A // LANE, tq_all // (nsub * tq)),
        in_specs=in_specs,
        out_specs=pl.BlockSpec((1, nsub * tq, LANE), lambda bi, hp, i: (bi, i, hp)),
        out_shape=jax.ShapeDtypeStruct((b, tq_all, W_A), BF16),
        scratch_shapes=[pltpu.VMEM((nsub, tq, LANE), F32), pltpu.VMEM((nsub, 2, tq, LANE), F32)],
        compiler_params=_cparams(("arbitrary", "arbitrary", "arbitrary")),
        name="stick_breaking_attn",
    )(*args)


def _gla_levels(c):
    return int(math.log2(c))


def _gla_mats(c):
    t = np.arange(c)[:, None]
    s = np.arange(c)[None, :]
    mats = [(s <= t).astype(np.float32), (s > t).astype(np.float32)]
    for lv in range(_gla_levels(c)):
        m = 1 << lv
        ref = (t // (2 * m)) * (2 * m) + m - 1
        mats.append(((s > ref) & (s <= t)).astype(np.float32) - ((s > t) & (s <= ref)).astype(np.float32))
    m = np.concatenate(mats, axis=0)
    return jnp.asarray(np.concatenate([m, m], axis=1), BF16)


def _gla_kernel(in_ref, wg_ref, bg_ref, gn_ref, mall_ref, s0_ref, ob_ref, sout_ref, st_ref, *, C, G):
    ci = pl.program_id(1)

    @pl.when(ci == 0)
    def _():
        st_ref[...] = s0_ref[0]

    for sub in range(G):
        _gla_chunk(in_ref, wg_ref, bg_ref, gn_ref, mall_ref, ob_ref, st_ref, C, sub * C)

    @pl.when(ci == pl.num_programs(1) - 1)
    def _():
        sout_ref[0] = st_ref[...]


def _gla_chunk(in_ref, wg_ref, bg_ref, gn_ref, mall_ref, ob_ref, st_ref, C, r0):
    rows = slice(r0, r0 + C)
    q = in_ref[rows, 0:HP_B] * (DK_B ** -0.5)
    k = in_ref[rows, HP_B:2 * HP_B]
    v = in_ref[rows, 2 * HP_B:3 * HP_B]
    gb = in_ref[rows, 3 * HP_B:4 * HP_B]
    rb = in_ref[rows, 4 * HP_B:4 * HP_B + LANE]
    lg = _log_sigmoid(_dot(rb.astype(BF16), wg_ref[...]) + bg_ref[...]) * (1.0 / GATE_TEMP_B)
    hi, lo = _split_bf16(lg)
    dall = _dot(mall_ref[...], jnp.concatenate([hi, lo], axis=0))
    eb = jnp.exp(dall[0:C])
    elast = jnp.exp(dall[C:2 * C])
    eb_last = eb[C - 1:C, :]
    rowi = lax.broadcasted_iota(jnp.int32, (C, 1), 0)
    row = lax.broadcasted_iota(jnp.int32, (C, C), 0)
    col = lax.broadcasted_iota(jnp.int32, (C, C), 1)
    qe = (q * eb).astype(BF16)
    ke = (k * elast).astype(BF16)
    qb = q.astype(BF16)
    kb = k.astype(BF16)
    vb = v.astype(BF16)
    zero = jnp.zeros_like(q)
    lv_q, lv_k = [], []
    for lv in range(_gla_levels(C)):
        e = jnp.exp(-jnp.abs(dall[(2 + lv) * C:(3 + lv) * C]))
        second = ((rowi >> lv) & 1) == 1
        lv_q.append(jnp.where(second, q * e, zero).astype(BF16))
        lv_k.append(jnp.where(second, zero, k * e).astype(BF16))
    outs = []
    for h in range(H_B):
        sl = slice(h * LANE, (h + 1) * LANE)
        att = jnp.where(row == col, _dot_nt(qb[:, sl], kb[:, sl]), 0.0)
        for lv in range(_gla_levels(C)):
            same = (row >> (lv + 1)) == (col >> (lv + 1))
            att = att + jnp.where(same, _dot_nt(lv_q[lv][:, sl], lv_k[lv][:, sl]), 0.0)
        st = st_ref[h]
        o = _dot(att.astype(BF16), vb[:, sl]) + _dot_nt(qe[:, sl], st.astype(BF16))
        st_ref[h] = st * eb_last[:, sl] + _dot(v[:, sl].T.astype(BF16), ke[:, sl])
        ms = jnp.sum(o * o, axis=-1, keepdims=True) * (1.0 / DV_B)
        on = o * lax.rsqrt(ms + EPS) * gn_ref[:, sl]
        g = gb[:, sl]
        outs.append((on * (g * _sigmoid(g))).astype(BF16))
    ob_ref[rows, :] = jnp.concatenate(outs, axis=1)


def _gla(gla_in, wg_p, bg_p, gn_p, s0t, seq):
    n = gla_in.shape[0]
    b = n // seq
    c = min(GLA_CHUNK, seq)
    g = math.gcd(GLA_CHUNKS_PER_STEP, seq // c)
    per_b = seq // (c * g)
    mall = _gla_mats(c)
    const2 = lambda bi, ci: (0, 0)
    return pl.pallas_call(
        functools.partial(_gla_kernel, C=c, G=g),
        grid=(b, per_b),
        in_specs=[pl.BlockSpec((c * g, GLA_W), lambda bi, ci: (bi * per_b + ci, 0)),
                  pl.BlockSpec((LANE, HP_B), const2), pl.BlockSpec((1, HP_B), const2),
                  pl.BlockSpec((1, HP_B), const2), pl.BlockSpec(mall.shape, const2),
                  pl.BlockSpec((1, H_B, LANE, LANE), lambda bi, ci: (bi, 0, 0, 0))],
        out_specs=[pl.BlockSpec((c * g, HP_B), lambda bi, ci: (bi * per_b + ci, 0)),
                   pl.BlockSpec((1, H_B, LANE, LANE), lambda bi, ci: (bi, 0, 0, 0))],
        out_shape=[jax.ShapeDtypeStruct((n, HP_B), BF16),
                   jax.ShapeDtypeStruct((b, H_B, LANE, LANE), F32)],
        scratch_shapes=[pltpu.VMEM((H_B, LANE, LANE), F32)],
        compiler_params=_cparams(("arbitrary", "arbitrary")),
        name="gla_chunked",
    )(gla_in, wg_p, bg_p, gn_p, mall, s0t)


def _lru_kernel(in_ref, cw_ref, cb_ref, wa_ref, ba_ref, wx_ref, bx_ref, lam_ref, buf0_ref, h0_ref,
                oc_ref, conv_ref, hout_ref, xp_ref, hc_ref, *, C):
    ci = pl.program_id(1)

    @pl.when(ci == 0)
    def _():
        xp_ref[0:SUBLANE] = buf0_ref[0]
        hc_ref[...] = h0_ref[0]

    x = in_ref[:, 0:W_C]
    y = in_ref[:, W_C:2 * W_C]
    xp_ref[SUBLANE:SUBLANE + C] = x
    xc = cb_ref[...]
    for j in range(CONV_W):
        xc = xc + xp_ref[pl.ds(SUBLANE - (CONV_W - 1) + j, C), :] * cw_ref[j:j + 1, :]
    xcb = xc.astype(BF16)
    r = _sigmoid(_dot(xcb, wa_ref[...]) + ba_ref[...])
    gi = _sigmoid(_dot(xcb, wx_ref[...]) + bx_ref[...])
    log_a = RG_C * r * _log_sigmoid(lam_ref[...])
    a = jnp.exp(log_a)
    u = jnp.sqrt(1.0 - jnp.exp(2.0 * log_a)) * (gi * xc)
    rowi = lax.broadcasted_iota(jnp.int32, (C, 1), 0)
    d = 1
    while d < C:
        keep = rowi >= d
        a_s = pltpu.roll(a, d, axis=0)
        u_s = pltpu.roll(u, d, axis=0)
        u = jnp.where(keep, a * u_s + u, u)
        a = jnp.where(keep, a * a_s, a)
        d *= 2
    hseq = u + a * hc_ref[...]
    hc_ref[...] = hseq[C - 1:C, :]
    gelu = 0.5 * y * (1.0 + jnp.tanh(math.sqrt(2.0 / math.pi) * (y + 0.044715 * (y * y * y))))
    oc_ref[...] = (hseq * gelu).astype(BF16)
    tail = xp_ref[C:C + SUBLANE]
    xp_ref[0:SUBLANE] = tail

    @pl.when(ci == pl.num_programs(1) - 1)
    def _():
        conv_ref[0] = tail
        hout_ref[0] = hseq[C - 1:C, :]


def _lru(lru_in, cw, cb, wa_bd, ba, wx_bd, bx, lam, buf0, h0, seq):
    n = lru_in.shape[0]
    b = n // seq
    c = min(LRU_CHUNK, seq)
    per_b = seq // c
    const2 = lambda bi, ci: (0, 0)
    vec = pl.BlockSpec((1, W_C), const2)
    return pl.pallas_call(
        functools.partial(_lru_kernel, C=c),
        grid=(b, per_b),
        in_specs=[pl.BlockSpec((c, 2 * W_C), lambda bi, ci: (bi * per_b + ci, 0)),
                  pl.BlockSpec((CONV_W, W_C), const2), vec,
                  pl.BlockSpec((W_C, W_C), const2), vec, pl.BlockSpec((W_C, W_C), const2), vec, vec,
                  pl.BlockSpec((1, SUBLANE, W_C), lambda bi, ci: (bi, 0, 0)),
                  pl.BlockSpec((1, 1, W_C), lambda bi, ci: (bi, 0, 0))],
        out_specs=[pl.BlockSpec((c, W_C), lambda bi, ci: (bi * per_b + ci, 0)),
                   pl.BlockSpec((1, SUBLANE, W_C), lambda bi, ci: (bi, 0, 0)),
                   pl.BlockSpec((1, 1, W_C), lambda bi, ci: (bi, 0, 0))],
        out_shape=[jax.ShapeDtypeStruct((n, W_C), BF16),
                   jax.ShapeDtypeStruct((b, SUBLANE, W_C), F32),
                   jax.ShapeDtypeStruct((b, 1, W_C), F32)],
        scratch_shapes=[pltpu.VMEM((c + SUBLANE, W_C), F32), pltpu.VMEM((1, W_C), F32)],
        compiler_params=_cparams(("arbitrary", "arbitrary")),
        name="conv_rglru",
    )(lru_in, cw, cb, wa_bd, ba, wx_bd, bx, lam, buf0, h0)


def _postmix_kernel(x_ref, oa_ref, ob_ref, oc_ref, wa_ref, wb_ref, wc_ref, gpm_ref, gpf_ref,
                    gt_ref, sc_ref, sh_ref, wr_hi_ref, wr_lo_ref, br_ref, tri_ref, ustrict_ref,
                    xm_ref, h2_ref, route_ref, stat_ref, *, tm):
    y =_dot(oa_ref[...], wa_ref[...]) + _dot(ob_ref[...], wb_ref[...]) + _dot(oc_ref[...], wc_ref[...])
    xm = x_ref[...] + gt_ref[0] * _rms(y, gpm_ref[...])
    xm_ref[...] = xm
    h2 = _rms(xm, gpf_ref[...]) * (1.0 + sc_ref[0]) + sh_ref[0]
    h2_ref[...] = h2
    hi, lo = _split_bf16(h2)
    wh = wr_hi_ref[...]
    logits = _dot(hi, wh) + _dot(lo, wh) + _dot(hi, wr_lo_ref[...]) + br_ref[...]
    lane = lax.broadcasted_iota(jnp.int32, (tm, LANE), 1)
    lane_f = lane.astype(F32)
    neg = jnp.float32(-jnp.inf)
    vals, hots = [], []
    for _ in range(TOP_K):
        m = jnp.max(logits, axis=-1, keepdims=True)
        idx = jnp.min(jnp.where(logits == m, lane_f, float(LANE)), axis=-1, keepdims=True)
        hot = lane_f == idx
        logits = jnp.where(hot, neg, logits)
        vals.append(m)
        hots.append(hot)
    ex = [jnp.exp(vk - vals[0]) for vk in vals]
    inv = 1.0 / (ex[0] + ex[1] + ex[2] + ex[3])
    sel = jnp.zeros((tm, LANE), F32)
    for hot in hots:
        sel = jnp.where(hot, 1.0, sel)
    selb = sel.astype(BF16)
    tile_cnt = jnp.sum(sel, axis=0, keepdims=True)
    groups8 = jnp.floor((tile_cnt + (SUBLANE - 1.0)) * (1.0 / SUBLANE))
    g8b = jnp.broadcast_to(groups8, (SUBLANE, LANE)).astype(BF16)
    loc_start = _dot(g8b, ustrict_ref[...])[0:1] * float(SUBLANE)
    local = _dot(tri_ref[...], selb) + loc_start
    route = jnp.zeros((tm, LANE), F32)
    for kk in range(TOP_K):
        p_k = jnp.sum(jnp.where(hots[kk], local, 0.0), axis=-1, keepdims=True)
        route = jnp.where(lane == kk, ex[kk] * inv, route)
        route = jnp.where(lane == TOP_K + kk, p_k, route)
    route_ref[...] = route
    srow = lax.broadcasted_iota(jnp.int32, (SUBLANE, LANE), 0)
    stat_ref[...] = jnp.where(srow == 0, groups8 * float(SUBLANE), jnp.where(srow == 1, loc_start, 0.0))


def _postmix(x2, oa, ob, oc, wo_a, wo_b, wo_c, gpm, gpf, gt, sc, sh, wr_hi, wr_lo, br, seq, tm):
    n, d = x2.shape
    per_b = seq // tm
    tok = lambda i: (i, 0)
    bat = lambda i: (i // per_b, 0, 0)
    const = lambda i: (0, 0)
    tri = jnp.asarray(np.tril(np.ones((tm, tm), np.float32), -1), BF16)
    ustrict = jnp.asarray(np.triu(np.ones((LANE, LANE), np.float32), 1), BF16)
    full = lambda a: pl.BlockSpec(a.shape, const)
    nt = n // tm
    return pl.pallas_call(
        functools.partial(_postmix_kernel, tm=tm),
        grid=(nt,),
        in_specs=[pl.BlockSpec((tm, d), tok), pl.BlockSpec((tm, W_A), tok),
                  pl.BlockSpec((tm, HP_B), tok), pl.BlockSpec((tm, W_C), tok),
                  full(wo_a), full(wo_b), full(wo_c),
                  pl.BlockSpec((1, d), const), pl.BlockSpec((1, d), const),
                  pl.BlockSpec((1, 1, d), bat), pl.BlockSpec((1, 1, d), bat), pl.BlockSpec((1, 1, d), bat),
                  full(wr_hi), full(wr_lo), pl.BlockSpec((1, LANE), const), full(tri), full(ustrict)],
        out_specs=[pl.BlockSpec((tm, d), tok), pl.BlockSpec((tm, d), tok),
                   pl.BlockSpec((tm, LANE), tok), pl.BlockSpec((SUBLANE, LANE), tok)],
        out_shape=[jax.ShapeDtypeStruct((n, d), F32), jax.ShapeDtypeStruct((n, d), F32),
                   jax.ShapeDtypeStruct((n, LANE), F32), jax.ShapeDtypeStruct((nt * SUBLANE, LANE), F32)],
        compiler_params=_cparams(("arbitrary",)),
        name="postmix_router",
    )(x2, oa, ob, oc, wo_a, wo_b, wo_c, gpm.reshape(1, d), gpf.reshape(1, d), gt, sc, sh,
      wr_hi, wr_lo, br, tri, ustrict)


def _run_copies(tab_ref, tm, make_copy, wait, tile=0):
    sizes = [1 << b for b in range(int(math.log2(tm)), int(math.log2(SUBLANE)) - 1, -1)]

    def per_expert(e, carry):
        n = tab_ref[tile, 0, e]
        loc = tab_ref[tile, 0, N_EXPERTS + e]
        dst = tab_ref[tile, 0, 2 * N_EXPERTS + e]
        for sz in sizes:
            done = n & ~(2 * sz - 1)

            @pl.when((n & sz) != 0)
            def _():
                cp = make_copy(pl.multiple_of(loc + done, SUBLANE), pl.multiple_of(dst + done, SUBLANE), sz)
                if wait:
                    cp.wait()
                else:
                    cp.start()
        return carry

    lax.fori_loop(0, N_EXPERTS, per_expert, 0)


def _local_rows(tm):
    return tm * TOP_K + N_EXPERTS * SUBLANE


def _slot_onehot(route_ref, tm):
    nloc = _local_rows(tm)
    pos = lax.broadcasted_iota(jnp.int32, (tm, nloc), 1).astype(F32)
    route = route_ref[...]

    def build(values):
        m = jnp.zeros((tm, nloc), F32)
        for kk in range(TOP_K):
            m = jnp.where(route[:, TOP_K + kk:TOP_K + kk + 1] == pos, values[kk], m)
        return m

    return route, build


def _dispatch_kernel(tab_ref, tabp_ref, tail_ref, route_ref, h_ref, *rest, tm, first, later_tile):
    i = pl.program_id(0)
    slot = i % 2
    if first:
        later_ref, xs_ref, sorted_ref, sem, zero_ref = rest
        bm = MOE_BLOCK_ROWS

        @pl.when(i == 0)
        def _():
            zero_ref[...] = jnp.zeros_like(zero_ref)
            fill = lambda loc, dst, sz: pltpu.make_async_copy(zero_ref.at[pl.ds(0, sz)],
                                                              xs_ref.at[pl.ds(dst, sz)], sem.at[0])
            n_later, later_tm = later_ref.shape[0], later_tile

            def block_fill(wait):
                def body(g, carry):
                    cp = fill(0, pl.multiple_of(g * bm, bm), bm)
                    cp.wait() if wait else cp.start()
                    return carry
                lax.fori_loop(tail_ref[0, 0, 3 * N_EXPERTS], xs_ref.shape[0] // bm, body, 0)

            for wait in (False, True):
                _run_copies(tail_ref, bm, fill, wait)
                block_fill(wait)
                for ti in range(n_later):
                    _run_copies(later_ref, later_tm, fill, wait, tile=ti)
    else:
        _, xs_ref, sorted_ref, sem = rest
    _, build = _slot_onehot(route_ref, tm)
    perm = build([1.0] * TOP_K).astype(BF16)
    sorted_ref[slot] = lax.dot_general(perm, h_ref[...].astype(BF16), (((0,), (0,)), ((), ())),
                                       preferred_element_type=F32)

    def runs(s):
        return lambda loc, dst, sz: pltpu.make_async_copy(sorted_ref.at[s, pl.ds(loc, sz)],
                                                          xs_ref.at[pl.ds(dst, sz)], sem.at[s])

    _run_copies(tab_ref, tm, runs(slot), wait=False)

    @pl.when(i > 0)
    def _():
        _run_copies(tabp_ref, tm, runs(1 - slot), wait=True)

    @pl.when(i == pl.num_programs(0) - 1)
    def _():
        _run_copies(tab_ref, tm, runs(slot), wait=True)


def _dispatch(tab, tail, route, h2, xs, n_pad, tm, later=None, later_tm=None):
    n, d = h2.shape
    tok = lambda i: (i, 0)
    first = xs is None
    in_specs = [pl.BlockSpec((1, 1, LANE), lambda i: (i, 0, 0), memory_space=pltpu.SMEM),
                pl.BlockSpec((1, 1, LANE), lambda i: (jnp.maximum(i - 1, 0), 0, 0), memory_space=pltpu.SMEM),
                pl.BlockSpec((1, 1, LANE), lambda i: (0, 0, 0), memory_space=pltpu.SMEM),
                pl.BlockSpec((tm, LANE), tok), pl.BlockSpec((tm, d), tok)]
    scratch = [pltpu.VMEM((2, _local_rows(tm), d), F32), pltpu.SemaphoreType.DMA((2,))]
    args = [tab, tab, tail, route, h2]
    if first:
        scratch.append(pltpu.VMEM((MOE_BLOCK_ROWS, d), F32))
        in_specs.append(pl.BlockSpec(later.shape, lambda i: (0, 0, 0), memory_space=pltpu.SMEM))
        args.append(later)
    else:
        in_specs.append(pl.BlockSpec(memory_space=pl.ANY))
        args.append(xs)
    return pl.pallas_call(
        functools.partial(_dispatch_kernel, tm=tm, first=first, later_tile=later_tm),
        grid=(n // tm,),
        in_specs=in_specs,
        out_specs=pl.BlockSpec(memory_space=pl.ANY),
        out_shape=jax.ShapeDtypeStruct((n_pad, d), F32),
        scratch_shapes=scratch,
        input_output_aliases={} if first else {5: 0},
        compiler_params=_cparams(("arbitrary",)),
        name="moe_dispatch",
    )(*args)


def _expert_kernel(be_ref, nv_ref, x_ref, wgu_ref, bgu_ref, wd_ref, bd_ref, y_ref, wgu_bf, wd_bf):
    g = pl.program_id(0)

    @pl.when(jnp.logical_or(g == 0, be_ref[g] != be_ref[jnp.maximum(g - 1, 0)]))
    def _():
        wgu_bf[...] = wgu_ref[0, 0].astype(BF16)
        wd_bf[...] = wd_ref[0, 0].astype(BF16)

    @pl.when(g < nv_ref[0])
    def _():
        gu = _dot(x_ref[...].astype(BF16), wgu_bf[...]) + bgu_ref[0, 0]
        gt = jnp.minimum(gu[:, :D_FF], SWIGLU_LIMIT)
        up = jnp.clip(gu[:, D_FF:], -SWIGLU_LIMIT, SWIGLU_LIMIT)
        act = (up + 1.0) * (gt * _sigmoid(SWIGLU_ALPHA * gt))
        y_ref[...] = _dot(act.astype(BF16), wd_bf[...]) + bd_ref[0, 0]

    @pl.when(g >= nv_ref[0])
    def _():
        y_ref[...] = jnp.zeros_like(y_ref)


def _experts(l, block_e, nvalid, xs, wgu, bgu, wd, bd):
    n_pad, d = xs.shape
    depth = wgu.shape[0]
    bm = MOE_BLOCK_ROWS
    nb = n_pad // bm
    grid_spec = pltpu.PrefetchScalarGridSpec(
        num_scalar_prefetch=2,
        grid=(nb,),
        in_specs=[pl.BlockSpec((bm, d), lambda g, be, nv: (jnp.minimum(g, nv[0] - 1), 0)),
                  pl.BlockSpec((1, 1, d, 2 * D_FF), lambda g, be, nv: (l, be[g], 0, 0)),
                  pl.BlockSpec((1, 1, 1, 2 * D_FF), lambda g, be, nv: (l, be[g], 0, 0)),
                  pl.BlockSpec((1, 1, D_FF, d), lambda g, be, nv: (l, be[g], 0, 0)),
                  pl.BlockSpec((1, 1, 1, d), lambda g, be, nv: (l, be[g], 0, 0))],
        out_specs=pl.BlockSpec((bm, d), lambda g, be, nv: (g, 0)),
        scratch_shapes=[pltpu.VMEM((d, 2 * D_FF), BF16), pltpu.VMEM((D_FF, d), BF16)],
    )
    return pl.pallas_call(
        _expert_kernel,
        grid_spec=grid_spec,
        out_shape=jax.ShapeDtypeStruct((n_pad, d), F32),
        compiler_params=_cparams(("arbitrary",)),
        name="moe_experts",
    )(block_e, nvalid, xs, wgu, bgu.reshape(depth, N_EXPERTS, 1, -1), wd, bd.reshape(depth, N_EXPERTS, 1, -1))


def _combine_kernel(tab_ref, tabn_ref, route_ref, ys_ref, xm_ref, gt_ref, g_ref, o_ref, buf_ref, sem, *, tm):
    i = pl.program_id(0)
    slot = i % 2

    def runs(s):
        return lambda loc, dst, sz: pltpu.make_async_copy(ys_ref.at[pl.ds(dst, sz)],
                                                          buf_ref.at[s, pl.ds(loc, sz)], sem.at[s])

    @pl.when(i == 0)
    def _():
        buf_ref[...] = jnp.zeros_like(buf_ref)
        _run_copies(tab_ref, tm, runs(0), wait=False)

    @pl.when(i + 1 < pl.num_programs(0))
    def _():
        _run_copies(tabn_ref, tm, runs(1 - slot), wait=False)

    route, build = _slot_onehot(route_ref, tm)
    gate_m = build([route[:, kk:kk + 1] for kk in range(TOP_K)])
    g_hi, g_lo = _split_bf16(gate_m)
    _run_copies(tab_ref, tm, runs(slot), wait=True)
    b_hi, b_lo = _split_bf16(buf_ref[slot])
    y = _dot(g_hi, b_hi) + _dot(g_lo, b_hi) + _dot(g_hi, b_lo)
    o_ref[...] = xm_ref[...] + gt_ref[0] * _rms(y, g_ref[...])


def _combine(tab, route, ys, xm, gt, g, seq, tm):
    n, d = xm.shape
    per_b = seq // tm
    nt = n // tm
    tok = lambda i: (i, 0)
    return pl.pallas_call(
        functools.partial(_combine_kernel, tm=tm),
        grid=(nt,),
        in_specs=[pl.BlockSpec((1, 1, LANE), lambda i: (i, 0, 0), memory_space=pltpu.SMEM),
                  pl.BlockSpec((1, 1, LANE), lambda i: (jnp.minimum(i + 1, nt - 1), 0, 0),
                               memory_space=pltpu.SMEM),
                  pl.BlockSpec((tm, LANE), tok), pl.BlockSpec(memory_space=pl.ANY),
                  pl.BlockSpec((tm, d), tok),
                  pl.BlockSpec((1, 1, d), lambda i: (i // per_b, 0, 0)),
                  pl.BlockSpec((1, d), lambda i: (0, 0))],
        out_specs=pl.BlockSpec((tm, d), tok),
        out_shape=jax.ShapeDtypeStruct((n, d), F32),
        scratch_shapes=[pltpu.VMEM((2, _local_rows(tm), d), F32), pltpu.SemaphoreType.DMA((2,))],
        compiler_params=_cparams(("arbitrary",)),
        name="moe_combine",
    )(tab, tab, route, ys, xm, gt, g.reshape(1, d))


def _pad_heads(w, width):
    lead = w.shape[:-1]
    w = w.reshape(lead + (H_B, width))
    w = jnp.pad(w, [(0, 0)] * len(lead) + [(0, 0), (0, LANE - width)])
    return w.reshape(lead + (HP_B,))


def _layer_weights(l, w_in, w_gla_gate, b_gla_gate, g_gla_norm, w_rg_a, w_rg_x, w_out, w_router, b_router):
    d = D_MODEL
    wi = w_in[l]
    o = 3 * W_A
    qb = wi[:, o:o + H_B * DK_B]
    kb = wi[:, o + 192:o + 384]
    vb = wi[:, o + 384:o + 768]
    gb = wi[:, o + 768:o + 1152]
    rb = wi[:, o + 1152:o + 1168]
    xy = wi[:, o + 1168:]
    w_in_p = jnp.concatenate(
        [wi[:, :o], _pad_heads(qb, DK_B), _pad_heads(kb, DK_B), _pad_heads(vb, DV_B), _pad_heads(gb, DV_B),
         jnp.pad(rb, ((0, 0), (0, LANE - GATE_RANK_B))), xy], axis=1).astype(BF16)
    wg_p = jnp.pad(_pad_heads(w_gla_gate[l], DK_B), ((0, LANE - GATE_RANK_B), (0, 0))).astype(BF16)
    bg_p = _pad_heads(b_gla_gate[l].reshape(1, -1), DK_B)
    gn_p = jnp.tile(jnp.pad(g_gla_norm[l], (0, LANE - DV_B)), H_B).reshape(1, HP_B)
    wa_bd = jax.scipy.linalg.block_diag(*[w_rg_a[l, i] for i in range(NBLK_C)]).astype(BF16)
    wx_bd = jax.scipy.linalg.block_diag(*[w_rg_x[l, i] for i in range(NBLK_C)]).astype(BF16)
    wo = w_out[l]
    wo_a = wo[:W_A].astype(BF16)
    wo_b = jnp.pad(wo[W_A:W_A + W_B].reshape(H_B, DV_B, d), ((0, 0), (0, LANE - DV_B), (0, 0)))
    wo_b = wo_b.reshape(HP_B, d).astype(BF16)
    wo_c = wo[W_A + W_B:].astype(BF16)
    wr = jnp.pad(w_router[l], ((0, 0), (0, LANE - N_EXPERTS)))
    wr_hi = wr.astype(BF16)
    wr_lo = (wr - wr_hi.astype(F32)).astype(BF16)
    br = jnp.pad(b_router[l], (0, LANE - N_EXPERTS), constant_values=-1e30).reshape(1, LANE)
    return dict(w_in_p=w_in_p, wg_p=wg_p, bg_p=bg_p, gn_p=gn_p, wa_bd=wa_bd, wx_bd=wx_bd,
                wo_a=wo_a, wo_b=wo_b, wo_c=wo_c, wr_hi=wr_hi, wr_lo=wr_lo, br=br)


def _pad_state(s):
    st = jnp.swapaxes(s, -1, -2)
    return jnp.pad(st, ((0, 0), (0, 0), (0, LANE - DV_B), (0, LANE - DK_B)))


def _unpad_state(st):
    return jnp.swapaxes(st[:, :, :DV_B, :DK_B], -1, -2)


def kernel(x_prompt, x_sample, c_prompt, c_sample, cache_k_sb, cache_v_sb, state_gla, state_conv, state_lru, w_ada, b_ada, g_pre_mix, g_post_mix, g_pre_ff, g_post_ff, w_in, w_gla_gate, b_gla_gate, g_gla_norm, w_conv, b_conv, w_rg_a, b_rg_a, w_rg_x, b_rg_x, lru_lambda, w_out, w_router, b_router, w_gate_up, b_gate_up, w_down, b_down):
    depth = w_ada.shape[0]
    d = D_MODEL
    groups = []
    for x, past in ((x_prompt, False), (x_sample, True)):
        b, t, _ = x.shape
        groups.append(dict(b=b, t=t, tm=min(TOKEN_TILE, t), n=b * t, past=past, x=x.reshape(b * t, d)))
    n_all = sum(g["n"] for g in groups)
    bm = MOE_BLOCK_ROWS
    n_tiles = sum(g["n"] // g["tm"] for g in groups)
    nb = -(-(n_all * TOP_K + n_tiles * N_EXPERTS * (SUBLANE - 1)) // bm) + N_EXPERTS
    n_pad = nb * bm

    nb_rows = sum(g["b"] for g in groups)
    c_all = jnp.concatenate([c_prompt, c_sample], axis=0)
    r_pad = -(-nb_rows // SUBLANE) * SUBLANE
    c_all = jnp.pad(c_all, ((0, r_pad - nb_rows), (0, 0)))
    mod = _modulation(c_all, w_ada, b_ada)

    st =[[[] for _ in range(5)] for _ in groups]
    for l in range(depth):
        lw = _layer_weights(l, w_in, w_gla_gate, b_gla_gate, g_gla_norm, w_rg_a, w_rg_x, w_out,
                            w_router, b_router)
        row0 = 0
        per_group = []
        for gi, g in enumerate(groups):
            b, t, tm, n = g["b"], g["t"], g["tm"], g["n"]
            m = mod[l, row0:row0 + b].reshape(b, 1, 6 * d)
            row0 += b
            sh_m, sc_m, gt_m, sh_f, sc_f, gt_f = [m[:, :, j * d:(j + 1) * d] for j in range(6)]
            ka, va, qab, kab, vab, gla_in, lru_in = _premix(g["x"], g_pre_mix[l], sc_m, sh_m, lw["w_in_p"], t, tm)
            q3 = qab.reshape(b, t, W_A)
            k3 = kab.reshape(b, t, W_A)
            v3 = vab.reshape(b, t, W_A)
            if g["past"]:
                n_past = cache_k_sb.shape[2]
                padk = ((0, 0), (0, -t % ATTN_KEY_TILE), (0, 0))
                oa = _attention(q3, jnp.pad(k3, padk), jnp.pad(v3, padk),
                                cache_k_sb[l].reshape(b, n_past, W_A), cache_v_sb[l].reshape(b, n_past, W_A))
                s0 = state_gla[l]
                buf0 = state_conv[l]
                h0 = state_lru[l]
            else:
                oa = _attention(q3, k3, v3)
                s0 = jnp.zeros((b, H_B, DK_B, DV_B), F32)
                buf0 = jnp.zeros((b, CONV_W - 1, W_C), F32)
                h0 = jnp.zeros((b, W_C), F32)
            oa = oa.reshape(n, W_A)
            ob, s_new = _gla(gla_in, lw["wg_p"], lw["bg_p"], lw["gn_p"], _pad_state(s0), t)
            buf0p = jnp.pad(buf0, ((0, 0), (SUBLANE - (CONV_W - 1), 0), (0, 0)))
            oc, conv_new, h_new = _lru(lru_in, w_conv[l], b_conv[l].reshape(1, -1), lw["wa_bd"],
                                       b_rg_a[l].reshape(1, -1), lw["wx_bd"], b_rg_x[l].reshape(1, -1),
                                       lru_lambda[l].reshape(1, -1), buf0p, h0.reshape(b, 1, W_C), t)
            xm, h2, route, stat = _postmix(g["x"], oa, ob, oc, lw["wo_a"], lw["wo_b"], lw["wo_c"],
                                           g_post_mix[l], g_pre_ff[l], gt_m, sc_f, sh_f,
                                           lw["wr_hi"], lw["wr_lo"], lw["br"], t, tm)
            per_group.append(dict(xm=xm, h2=h2, route=route, stat=stat, gt_f=gt_f))
            new = (ka.reshape(b, t, W_A // HD_A, HD_A), va.reshape(b, t, W_A // HD_A, HD_A),
                   _unpad_state(s_new), conv_new[:, SUBLANE - (CONV_W - 1):], h_new.reshape(b, W_C))
            for j in range(5):
                st[gi][j].append(new[j])

        stat = jnp.concatenate([p["stat"].reshape(-1, SUBLANE, LANE)[:, :2, :N_EXPERTS] for p in per_group],
                               axis=0).astype(jnp.int32)
        run_rows, run_loc = stat[:, 0], stat[:, 1]
        before = jnp.cumsum(run_rows, axis=0) - run_rows
        counts = jnp.sum(run_rows, axis=0)
        padded = (counts + bm - 1) // bm * bm
        pad_end = jnp.cumsum(padded)
        pad_start = pad_end - padded
        block_start = jnp.arange(nb, dtype=jnp.int32) * bm
        block_e = jnp.minimum(jnp.sum((pad_end[None, :] <= block_start[:, None]).astype(jnp.int32), axis=1),
                              N_EXPERTS - 1)
        nvalid = (pad_end[-1] // bm).astype(jnp.int32).reshape(1)
        tab = jnp.concatenate([run_rows, run_loc, pad_start[None, :] + before, jnp.zeros_like(run_rows)],
                              axis=1).reshape(-1, 1, LANE)
        tile0 = 0
        for gi, g in enumerate(groups):
            nt = g["n"] // g["tm"]
            per_group[gi]["tab"] = tab[tile0:tile0 + nt]
            tile0 += nt
        zeros32 = jnp.zeros_like(counts)
        tail = jnp.concatenate([padded - counts, zeros32, pad_start + counts,
                                jnp.broadcast_to(nvalid, (N_EXPERTS,))]).reshape(1, 1, LANE)
        first, second = per_group
        xs = _dispatch(first["tab"], tail, first["route"], first["h2"], None, n_pad, groups[0]["tm"],
                       later=second["tab"], later_tm=groups[1]["tm"])
        xs = _dispatch(second["tab"], tail, second["route"], second["h2"], xs, n_pad, groups[1]["tm"])
        ys = _experts(l, block_e, nvalid, xs, w_gate_up, b_gate_up, w_down, b_down)
        for gi, g in enumerate(groups):
            p = per_group[gi]
            g["x"] = _combine(p["tab"], p["route"], ys, p["xm"], p["gt_f"], g_post_ff[l], g["t"], g["tm"])

    outs = [g["x"].reshape(g["b"], g["t"], d) for g in groups]
    for gi in range(len(groups)):
        outs.extend(jnp.stack(st[gi][j]) for j in range(5))
    return tuple(outs)
```

```python
import functools
import math

import numpy as np
import jax
import jax.numpy as jnp
from jax import lax
from jax.experimental import pallas as pl
from jax.experimental.pallas import tpu as pltpu

F32 = jnp.float32
BF16 = jnp.bfloat16

D_MODEL = 1024
EPS = 1e-6
HD_A = 64
W_A = 384
H_B = 4
DK_B = 48
DV_B = 96
W_B = 384
GATE_RANK_B = 16
GATE_TEMP_B = 16.0
W_C = 256
NBLK_C = 4
BW_C = 64
CONV_W = 4
RG_C = 8.0
N_EXPERTS = 32
TOP_K = 4
D_FF = 1024
SWIGLU_ALPHA = 1.702
SWIGLU_LIMIT = 7.0
LOG_WEIGHT_FLOOR = -120.0

LANE = 128
SUBLANE = 8
VMEM_LIMIT = 56 * 1024 * 1024

HP_B = H_B * LANE
OFF_GLA = 3 * W_A
GLA_W = 4 * HP_B + LANE
OFF_LRU = OFF_GLA + GLA_W
IN_PAD = OFF_LRU + 2 * W_C

TOKEN_TILE = 256
ATTN_TILE = 256
ATTN_KEY_TILE = 256
ATTN_SUBTILES = 4
GLA_CHUNK = 128
GLA_CHUNKS_PER_STEP = 2
LRU_CHUNK = 256
MOE_BLOCK_ROWS = 512


def _cparams(sem):
    return pltpu.CompilerParams(dimension_semantics=sem, vmem_limit_bytes=VMEM_LIMIT)


def _sigmoid(x):
    return 1.0 / (1.0 + jnp.exp(-x))


def _log_sigmoid(x):
    return jnp.minimum(x, 0.0) - jnp.log(1.0 + jnp.exp(-jnp.abs(x)))


def _split_bf16(x):
    hi = x.astype(BF16)
    lo = (x - hi.astype(F32)).astype(BF16)
    return hi, lo


def _dot(a, b):
    return jnp.dot(a, b, preferred_element_type=F32)


def _dot_nt(a, b):
    return lax.dot_general(a, b, (((1,), (1,)), ((), ())), preferred_element_type=F32)


def _rms(x, g):
    return x * lax.rsqrt(jnp.mean(x * x, axis=-1, keepdims=True) + EPS) * g


def _mod_kernel(c_ref, w_ref, b_ref, o_ref):
    c = c_ref[...]
    a = (c * _sigmoid(c)).astype(BF16)
    o_ref[0] = _dot(a, w_ref[0].astype(BF16)) + b_ref[0]


def _modulation(c_all, w_ada, b_ada):
    depth, d, n = w_ada.shape
    r = c_all.shape[0]
    tn = 1536
    return pl.pallas_call(
        _mod_kernel,
        grid=(depth, n // tn),
        in_specs=[pl.BlockSpec((r, d), lambda l, j: (0, 0)),
                  pl.BlockSpec((1, d, tn), lambda l, j: (l, 0, j)),
                  pl.BlockSpec((1, 1, tn), lambda l, j: (l, 0, j))],
        out_specs=pl.BlockSpec((1, r, tn), lambda l, j: (l, 0, j)),
        out_shape=jax.ShapeDtypeStruct((depth, r, n), F32),
        compiler_params=_cparams(("arbitrary", "arbitrary")),
        name="adaln_mod",
    )(c_all, w_ada, b_ada.reshape(depth, 1, n))


def _premix_kernel(x_ref, g_ref, sc_ref, sh_ref, w_ref,
                   ka_ref, va_ref, qab_ref, kab_ref, vab_ref, gla_ref, lru_ref):
    h = _rms(x_ref[...], g_ref[...]) * (1.0 + sc_ref[0]) + sh_ref[0]
    r = _dot(h.astype(BF16), w_ref[...])
    ka = r[:, W_A:2 * W_A]
    va = r[:, 2 * W_A:3 * W_A]
    ka_ref[...] = ka
    va_ref[...] = va
    qab_ref[...] = (r[:, 0:W_A] * (HD_A ** -0.5)).astype(BF16)
    kab_ref[...] = ka.astype(BF16)
    vab_ref[...] = va.astype(BF16)
    gla_ref[...] = r[:, OFF_GLA:OFF_LRU]
    lru_ref[...] = r[:, OFF_LRU:IN_PAD]


def _premix(x2, g, sc, sh, w_in_p, seq, tm):
    n, d = x2.shape
    per_b = seq // tm
    tok = lambda i: (i, 0)
    bat = lambda i: (i // per_b, 0, 0)
    const = lambda i: (0, 0)
    outs = [((n, W_A), F32), ((n, W_A), F32), ((n, W_A), BF16), ((n, W_A), BF16), ((n, W_A), BF16),
            ((n, GLA_W), F32), ((n, 2 * W_C), F32)]
    return pl.pallas_call(
        _premix_kernel,
        grid=(n // tm,),
        in_specs=[pl.BlockSpec((tm, d), tok), pl.BlockSpec((1, d), const),
                  pl.BlockSpec((1, 1, d), bat), pl.BlockSpec((1, 1, d), bat),
                  pl.BlockSpec((d, IN_PAD), const)],
        out_specs=[pl.BlockSpec((tm, s[1]), tok) for s, _ in outs],
        out_shape=[jax.ShapeDtypeStruct(s, t) for s, t in outs],
        compiler_params=_cparams(("arbitrary",)),
        name="premix_proj",
    )(x2, g.reshape(1, d), sc, sh, w_in_p)


def _attn_kernel(q_ref, k_ref, v_ref, *rest, tq, tk, off, has_past, nsub):
    if has_past:
        kp_ref, vp_ref, u_ref, o_ref, acc_ref, c_ref = rest
    else:
        u_ref, o_ref, acc_ref, c_ref = rest
    i = pl.program_id(2)
    lane = lax.broadcasted_iota(jnp.int32, (1, LANE), 1)
    head_lanes = (lane < HD_A, lane >= HD_A)
    acc_ref[...] = jnp.zeros_like(acc_ref)
    c_ref[...] = jnp.zeros_like(c_ref)
    row = lax.broadcasted_iota(jnp.int32, (tq, tk), 0)
    col = lax.broadcasted_iota(jnp.int32, (tq, tk), 1)
    u = u_ref[...]
    qh, qpos0, nk = [], [], []
    for a in range(nsub):
        q = q_ref[0, a * tq:(a + 1) * tq, :]
        qz = jnp.zeros_like(q)
        qh.append(tuple(jnp.where(m, q, qz) for m in head_lanes))
        qpos0.append(off + (i * nsub + a) * tq)
        nk.append((qpos0[a] + tq - 1 + tk - 1) // tk)

    def step(jj, masked):
        cmax = jnp.float32(-jnp.inf)
        for a in range(nsub):
            j = nk[a] - 1 - jj
            live = j >= 0
            ks = pl.multiple_of(jnp.maximum(j, 0) * tk, tk)
            if has_past and not masked:
                kb = kp_ref[0, 0, pl.ds(ks, tk), :].astype(BF16)
                vb = vp_ref[0, 0, pl.ds(ks, tk), :].astype(BF16)
            else:
                kb = k_ref[0, pl.ds(pl.multiple_of(ks - off, tk), tk), :]
                vb = v_ref[0, pl.ds(pl.multiple_of(ks - off, tk), tk), :]
            vz = jnp.zeros_like(vb)
            mask = (col + ks) < (row + qpos0[a])
            for h in range(2):
                s = _dot_nt(qh[a][h], kb)
                lf = -(jnp.maximum(s, 0.0) + jnp.log(1.0 + jnp.exp(-jnp.abs(s))))
                lb = s + lf
                if masked:
                    lf = jnp.where(mask, lf, 0.0)
                hi, lo = _split_bf16(lf)
                cr = _dot(jnp.concatenate([hi, lo], axis=1), u)
                c = c_ref[a, h]
                cfull = jnp.concatenate([c] * (tk // LANE), axis=1)
                w = jnp.exp(lb + cr[:, :tk] + cfull)
                if masked:
                    w = jnp.where(mask, w, 0.0)
                vh = jnp.where(head_lanes[h], vb, vz)
                pv = _dot(w.astype(BF16), vh)
                rs = cr[:, tk:]
                if not masked:
                    pv = jnp.where(live, pv, 0.0)
                    rs = jnp.where(live, rs, 0.0)
                acc_ref[a] += pv
                c_ref[a, h] = c + rs
            cmax = jnp.maximum(cmax, jnp.where(j > 0, jnp.max(c_ref[a]), -jnp.inf))
        return cmax > LOG_WEIGHT_FLOOR

    n_masked = max(1, tq // tk)
    for jj in range(n_masked):
        go = step(jj, True)

    def body(carry):
        jj, _ = carry
        return jj + 1, step(jj, False)

    lax.while_loop(lambda carry: jnp.logical_and(carry[0] < nk[-1], carry[1]), body, (jnp.int32(n_masked), go))
    for a in range(nsub):
        o_ref[0, a * tq:(a + 1) * tq, :] = acc_ref[a].astype(o_ref.dtype)


def _attn_umat(tk):
    jp = np.arange(tk)[:, None]
    j = np.arange(tk)[None, :]
    u = np.concatenate([(jp > j).astype(np.float32), np.ones((tk, LANE), np.float32)], axis=1)
    return jnp.asarray(np.concatenate([u, u], axis=0), BF16)


def _attention(q, k, v, kp=None, vp=None, layer=0):
    b, tq_all, _ = q.shape
    tc = k.shape[1]
    has_past = kp is not None
    off = kp.shape[2] if has_past else 0
    tq = min(ATTN_TILE, tq_all)
    tk = ATTN_KEY_TILE
    assert tc % tk == 0 and tq_all % tq == 0 and tq_all <= tc
    assert off % tk == 0 and (tk % tq == 0 or tq % tk == 0)
    assert not has_past or (tq == tq_all and tq_all <= tk)
    nsub = math.gcd(ATTN_SUBTILES, tq_all // tq)
    kern = functools.partial(_attn_kernel, tq=tq, tk=tk, off=off, has_past=has_past, nsub=nsub)
    cur = pl.BlockSpec((1, tc, LANE), lambda bi, hp, i: (bi, 0, hp))
    in_specs = [pl.BlockSpec((1, nsub * tq, LANE), lambda bi, hp, i: (bi, i, hp)), cur, cur]
    args = [q, k, v]
    if has_past:
        past = pl.BlockSpec((1, 1, off, LANE), lambda bi, hp, i: (layer, bi, 0, hp))
        in_specs += [past, past]
        args += [kp, vp]
    in_specs.append(pl.BlockSpec((2 * tk, tk + LANE), lambda bi, hp, i: (0, 0)))
    args.append(_attn_umat(tk))
    return pl.pallas_call(
        kern,
        grid=(b, W_A // LANE, tq_all // (nsub * tq)),
        in_specs=in_specs,
        out_specs=pl.BlockSpec((1, nsub * tq, LANE), lambda bi, hp, i: (bi, i, hp)),
        out_shape=jax.ShapeDtypeStruct((b, tq_all, W_A), BF16),
        scratch_shapes=[pltpu.VMEM((nsub, tq, LANE), F32), pltpu.VMEM((nsub, 2, tq, LANE), F32)],
        compiler_params=_cparams(("arbitrary", "arbitrary", "arbitrary")),
        name="stick_breaking_attn",
    )(*args)


def _gla_levels(c):
    return int(math.log2(c))


def _gla_mats(c):
    t = np.arange(c)[:, None]
    s = np.arange(c)[None, :]
    mats = [(s <= t).astype(np.float32), (s > t).astype(np.float32)]
    for lv in range(_gla_levels(c)):
        m = 1 << lv
        ref = (t // (2 * m)) * (2 * m) + m - 1
        mats.append(((s > ref) & (s <= t)).astype(np.float32) - ((s > t) & (s <= ref)).astype(np.float32))
    m = np.concatenate(mats, axis=0)
    return jnp.asarray(np.concatenate([m, m], axis=1), BF16)


def _gla_kernel(in_ref, wg_ref, bg_ref, gn_ref, mall_ref, s0_ref, ob_ref, sout_ref, st_ref, *, C, G):
    ci = pl.program_id(1)

    @pl.when(ci == 0)
    def _():
        st_ref[...] = s0_ref[0]

    for sub in range(G):
        _gla_chunk(in_ref, wg_ref, bg_ref, gn_ref, mall_ref, ob_ref, st_ref, C, sub * C)

    @pl.when(ci == pl.num_programs(1) - 1)
    def _():
        sout_ref[0] = st_ref[...]


def _gla_chunk(in_ref, wg_ref, bg_ref, gn_ref, mall_ref, ob_ref, st_ref, C, r0):
    rows = slice(r0, r0 + C)
    q = in_ref[rows, 0:HP_B] * (DK_B ** -0.5)
    k = in_ref[rows, HP_B:2 * HP_B]
    v = in_ref[rows, 2 * HP_B:3 * HP_B]
    gb = in_ref[rows, 3 * HP_B:4 * HP_B]
    rb = in_ref[rows, 4 * HP_B:4 * HP_B + LANE]
    lg = _log_sigmoid(_dot(rb.astype(BF16), wg_ref[...]) + bg_ref[...]) * (1.0 / GATE_TEMP_B)
    hi, lo = _split_bf16(lg)
    dall = _dot(mall_ref[...], jnp.concatenate([hi, lo], axis=0))
    eb = jnp.exp(dall[0:C])
    elast = jnp.exp(dall[C:2 * C])
    eb_last = eb[C - 1:C, :]
    rowi = lax.broadcasted_iota(jnp.int32, (C, 1), 0)
    row = lax.broadcasted_iota(jnp.int32, (C, C), 0)
    col = lax.broadcasted_iota(jnp.int32, (C, C), 1)
    qe = (q * eb).astype(BF16)
    ke = (k * elast).astype(BF16)
    qb = q.astype(BF16)
    kb = k.astype(BF16)
    vb = v.astype(BF16)
    zero = jnp.zeros_like(q)
    lv_q, lv_k = [], []
    for lv in range(_gla_levels(C)):
        e = jnp.exp(-jnp.abs(dall[(2 + lv) * C:(3 + lv) * C]))
        second = ((rowi >> lv) & 1) == 1
        lv_q.append(jnp.where(second, q * e, zero).astype(BF16))
        lv_k.append(jnp.where(second, zero, k * e).astype(BF16))
    outs = []
    for h in range(H_B):
        sl = slice(h * LANE, (h + 1) * LANE)
        att = jnp.where(row == col, _dot_nt(qb[:, sl], kb[:, sl]), 0.0)
        for lv in range(_gla_levels(C)):
            same = (row >> (lv + 1)) == (col >> (lv + 1))
            att = att + jnp.where(same, _dot_nt(lv_q[lv][:, sl], lv_k[lv][:, sl]), 0.0)
        st = st_ref[h]
        o = _dot(att.astype(BF16), vb[:, sl]) + _dot_nt(qe[:, sl], st.astype(BF16))
        st_ref[h] = st * eb_last[:, sl] + _dot(v[:, sl].T.astype(BF16), ke[:, sl])
        ms = jnp.sum(o * o, axis=-1, keepdims=True) * (1.0 / DV_B)
        on = o * lax.rsqrt(ms + EPS) * gn_ref[:, sl]
        g = gb[:, sl]
        outs.append((on * (g * _sigmoid(g))).astype(BF16))
    ob_ref[rows, :] = jnp.concatenate(outs, axis=1)


def _gla(gla_in, wg_p, bg_p, gn_p, s0t, seq):
    n = gla_in.shape[0]
    b = n // seq
    c = min(GLA_CHUNK, seq)
    g = math.gcd(GLA_CHUNKS_PER_STEP, seq // c)
    per_b = seq // (c * g)
    mall = _gla_mats(c)
    const2 = lambda bi, ci: (0, 0)
    return pl.pallas_call(
        functools.partial(_gla_kernel, C=c, G=g),
        grid=(b, per_b),
        in_specs=[pl.BlockSpec((c * g, GLA_W), lambda bi, ci: (bi * per_b + ci, 0)),
                  pl.BlockSpec((LANE, HP_B), const2), pl.BlockSpec((1, HP_B), const2),
                  pl.BlockSpec((1, HP_B), const2), pl.BlockSpec(mall.shape, const2),
                  pl.BlockSpec((1, H_B, LANE, LANE), lambda bi, ci: (bi, 0, 0, 0))],
        out_specs=[pl.BlockSpec((c * g, HP_B), lambda bi, ci: (bi * per_b + ci, 0)),
                   pl.BlockSpec((1, H_B, LANE, LANE), lambda bi, ci: (bi, 0, 0, 0))],
        out_shape=[jax.ShapeDtypeStruct((n, HP_B), BF16),
                   jax.ShapeDtypeStruct((b, H_B, LANE, LANE), F32)],
        scratch_shapes=[pltpu.VMEM((H_B, LANE, LANE), F32)],
        compiler_params=_cparams(("arbitrary", "arbitrary")),
        name="gla_chunked",
    )(gla_in, wg_p, bg_p, gn_p, mall, s0t)


def _lru_kernel(in_ref, cw_ref, cb_ref, wa_ref, ba_ref, wx_ref, bx_ref, lam_ref, buf0_ref, h0_ref,
                oc_ref, conv_ref, hout_ref, xp_ref, hc_ref, *, C):
    ci = pl.program_id(1)

    @pl.when(ci == 0)
    def _():
        xp_ref[0:SUBLANE] = buf0_ref[0]
        hc_ref[...] = h0_ref[0]

    x = in_ref[:, 0:W_C]
    y = in_ref[:, W_C:2 * W_C]
    xp_ref[SUBLANE:SUBLANE + C] = x
    xc = cb_ref[...]
    for j in range(CONV_W):
        xc = xc + xp_ref[pl.ds(SUBLANE - (CONV_W - 1) + j, C), :] * cw_ref[j:j + 1, :]
    xcb = xc.astype(BF16)
    r = _sigmoid(_dot(xcb, wa_ref[...]) + ba_ref[...])
    gi = _sigmoid(_dot(xcb, wx_ref[...]) + bx_ref[...])
    log_a = RG_C * r * _log_sigmoid(lam_ref[...])
    a = jnp.exp(log_a)
    u = jnp.sqrt(1.0 - jnp.exp(2.0 * log_a)) * (gi * xc)
    rowi = lax.broadcasted_iota(jnp.int32, (C, 1), 0)
    d = 1
    while d < C:
        keep = rowi >= d
        a_s = pltpu.roll(a, d, axis=0)
        u_s = pltpu.roll(u, d, axis=0)
        u = jnp.where(keep, a * u_s + u, u)
        a = jnp.where(keep, a * a_s, a)
        d *= 2
    hseq = u + a * hc_ref[...]
    hc_ref[...] = hseq[C - 1:C, :]
    gelu = 0.5 * y * (1.0 + jnp.tanh(math.sqrt(2.0 / math.pi) * (y + 0.044715 * (y * y * y))))
    oc_ref[...] = (hseq * gelu).astype(BF16)
    tail = xp_ref[C:C + SUBLANE]
    xp_ref[0:SUBLANE] = tail

    @pl.when(ci == pl.num_programs(1) - 1)
    def _():
        conv_ref[0] = tail
        hout_ref[0] = hseq[C - 1:C, :]


def _lru(lru_in, cw, cb, wa_bd, ba, wx_bd, bx, lam, buf0, h0, seq):
    n = lru_in.shape[0]
    b = n // seq
    c = min(LRU_CHUNK, seq)
    per_b = seq // c
    const2 = lambda bi, ci: (0, 0)
    vec = pl.BlockSpec((1, W_C), const2)
    return pl.pallas_call(
        functools.partial(_lru_kernel, C=c),
        grid=(b, per_b),
        in_specs=[pl.BlockSpec((c, 2 * W_C), lambda bi, ci: (bi * per_b + ci, 0)),
                  pl.BlockSpec((CONV_W, W_C), const2), vec,
                  pl.BlockSpec((W_C, W_C), const2), vec, pl.BlockSpec((W_C, W_C), const2), vec, vec,
                  pl.BlockSpec((1, SUBLANE, W_C), lambda bi, ci: (bi, 0, 0)),
                  pl.BlockSpec((1, 1, W_C), lambda bi, ci: (bi, 0, 0))],
        out_specs=[pl.BlockSpec((c, W_C), lambda bi, ci: (bi * per_b + ci, 0)),
                   pl.BlockSpec((1, SUBLANE, W_C), lambda bi, ci: (bi, 0, 0)),
                   pl.BlockSpec((1, 1, W_C), lambda bi, ci: (bi, 0, 0))],
        out_shape=[jax.ShapeDtypeStruct((n, W_C), BF16),
                   jax.ShapeDtypeStruct((b, SUBLANE, W_C), F32),
                   jax.ShapeDtypeStruct((b, 1, W_C), F32)],
        scratch_shapes=[pltpu.VMEM((c + SUBLANE, W_C), F32), pltpu.VMEM((1, W_C), F32)],
        compiler_params=_cparams(("arbitrary", "arbitrary")),
        name="conv_rglru",
    )(lru_in, cw, cb, wa_bd, ba, wx_bd, bx, lam, buf0, h0)


def _postmix_kernel(x_ref, oa_ref, ob_ref, oc_ref, wa_ref, wb_ref, wc_ref, gpm_ref, gpf_ref,
                    gt_ref, sc_ref, sh_ref, wr_hi_ref, wr_lo_ref, br_ref, tri_ref, ustrict_ref,
                    xm_ref, h2_ref, route_ref, stat_ref, *, tm):
    y =_dot(oa_ref[...], wa_ref[...]) + _dot(ob_ref[...], wb_ref[...]) + _dot(oc_ref[...], wc_ref[...])
    xm = x_ref[...] + gt_ref[0] * _rms(y, gpm_ref[...])
    xm_ref[...] = xm
    h2 = _rms(xm, gpf_ref[...]) * (1.0 + sc_ref[0]) + sh_ref[0]
    h2_ref[...] = h2
    hi, lo = _split_bf16(h2)
    wh = wr_hi_ref[...]
    logits = _dot(hi, wh) + _dot(lo, wh) + _dot(hi, wr_lo_ref[...]) + br_ref[...]
    lane = lax.broadcasted_iota(jnp.int32, (tm, LANE), 1)
    lane_f = lane.astype(F32)
    neg = jnp.float32(-jnp.inf)
    vals, hots = [], []
    for _ in range(TOP_K):
        m = jnp.max(logits, axis=-1, keepdims=True)
        idx = jnp.min(jnp.where(logits == m, lane_f, float(LANE)), axis=-1, keepdims=True)
        hot = lane_f == idx
        logits = jnp.where(hot, neg, logits)
        vals.append(m)
        hots.append(hot)
    ex = [jnp.exp(vk - vals[0]) for vk in vals]
    inv = 1.0 / (ex[0] + ex[1] + ex[2] + ex[3])
    sel = jnp.zeros((tm, LANE), F32)
    for hot in hots:
        sel = jnp.where(hot, 1.0, sel)
    selb = sel.astype(BF16)
    tile_cnt = jnp.sum(sel, axis=0, keepdims=True)
    groups8 = jnp.floor((tile_cnt + (SUBLANE - 1.0)) * (1.0 / SUBLANE))
    g8b = jnp.broadcast_to(groups8, (SUBLANE, LANE)).astype(BF16)
    loc_start = _dot(g8b, ustrict_ref[...])[0:1] * float(SUBLANE)
    local = _dot(tri_ref[...], selb) + loc_start
    route = jnp.zeros((tm, LANE), F32)
    for kk in range(TOP_K):
        p_k = jnp.sum(jnp.where(hots[kk], local, 0.0), axis=-1, keepdims=True)
        route = jnp.where(lane == kk, ex[kk] * inv, route)
        route = jnp.where(lane == TOP_K + kk, p_k, route)
    route_ref[...] = route
    srow = lax.broadcasted_iota(jnp.int32, (SUBLANE, LANE), 0)
    stat_ref[...] = jnp.where(srow == 0, groups8 * float(SUBLANE), jnp.where(srow == 1, loc_start, 0.0))


def _postmix(x2, oa, ob, oc, wo_a, wo_b, wo_c, gpm, gpf, gt, sc, sh, wr_hi, wr_lo, br, seq, tm):
    n, d = x2.shape
    per_b = seq // tm
    tok = lambda i: (i, 0)
    bat = lambda i: (i // per_b, 0, 0)
    const = lambda i: (0, 0)
    tri = jnp.asarray(np.tril(np.ones((tm, tm), np.float32), -1), BF16)
    ustrict = jnp.asarray(np.triu(np.ones((LANE, LANE), np.float32), 1), BF16)
    full = lambda a: pl.BlockSpec(a.shape, const)
    nt = n // tm
    return pl.pallas_call(
        functools.partial(_postmix_kernel, tm=tm),
        grid=(nt,),
        in_specs=[pl.BlockSpec((tm, d), tok), pl.BlockSpec((tm, W_A), tok),
                  pl.BlockSpec((tm, HP_B), tok), pl.BlockSpec((tm, W_C), tok),
                  full(wo_a), full(wo_b), full(wo_c),
                  pl.BlockSpec((1, d), const), pl.BlockSpec((1, d), const),
                  pl.BlockSpec((1, 1, d), bat), pl.BlockSpec((1, 1, d), bat), pl.BlockSpec((1, 1, d), bat),
                  full(wr_hi), full(wr_lo), pl.BlockSpec((1, LANE), const), full(tri), full(ustrict)],
        out_specs=[pl.BlockSpec((tm, d), tok), pl.BlockSpec((tm, d), tok),
                   pl.BlockSpec((tm, LANE), tok), pl.BlockSpec((SUBLANE, LANE), tok)],
        out_shape=[jax.ShapeDtypeStruct((n, d), F32), jax.ShapeDtypeStruct((n, d), F32),
                   jax.ShapeDtypeStruct((n, LANE), F32), jax.ShapeDtypeStruct((nt * SUBLANE, LANE), F32)],
        compiler_params=_cparams(("arbitrary",)),
        name="postmix_router",
    )(x2, oa, ob, oc, wo_a, wo_b, wo_c, gpm.reshape(1, d), gpf.reshape(1, d), gt, sc, sh,
      wr_hi, wr_lo, br, tri, ustrict)


def _run_copies(tab_ref, tm, make_copy, wait, tile=0):
    sizes = [1 << b for b in range(int(math.log2(tm)), int(math.log2(SUBLANE)) - 1, -1)]

    def per_expert(e, carry):
        n = tab_ref[tile, 0, e]
        loc = tab_ref[tile, 0, N_EXPERTS + e]
        dst = tab_ref[tile, 0, 2 * N_EXPERTS + e]
        for sz in sizes:
            done = n & ~(2 * sz - 1)

            @pl.when((n & sz) != 0)
            def _():
                cp = make_copy(pl.multiple_of(loc + done, SUBLANE), pl.multiple_of(dst + done, SUBLANE), sz)
                if wait:
                    cp.wait()
                else:
                    cp.start()
        return carry

    lax.fori_loop(0, N_EXPERTS, per_expert, 0)


def _local_rows(tm):
    return tm * TOP_K + N_EXPERTS * SUBLANE


def _slot_onehot(route_ref, tm):
    nloc = _local_rows(tm)
    pos = lax.broadcasted_iota(jnp.int32, (tm, nloc), 1).astype(F32)
    route = route_ref[...]

    def build(values):
        m = jnp.zeros((tm, nloc), F32)
        for kk in range(TOP_K):
            m = jnp.where(route[:, TOP_K + kk:TOP_K + kk + 1] == pos, values[kk], m)
        return m

    return route, build


def _dispatch_kernel(tab_ref, tabp_ref, tail_ref, route_ref, h_ref, *rest, tm, first, later_tile):
    i = pl.program_id(0)
    slot = i % 2
    if first:
        later_ref, xs_ref, sorted_ref, sem, zero_ref = rest
        bm = MOE_BLOCK_ROWS

        @pl.when(i == 0)
        def _():
            zero_ref[...] = jnp.zeros_like(zero_ref)
            fill = lambda loc, dst, sz: pltpu.make_async_copy(zero_ref.at[pl.ds(0, sz)],
                                                              xs_ref.at[pl.ds(dst, sz)], sem.at[0])
            n_later, later_tm = later_ref.shape[0], later_tile

            def block_fill(wait):
                def body(g, carry):
                    cp = fill(0, pl.multiple_of(g * bm, bm), bm)
                    cp.wait() if wait else cp.start()
                    return carry
                lax.fori_loop(tail_ref[0, 0, 3 * N_EXPERTS], xs_ref.shape[0] // bm, body, 0)

            for wait in (False, True):
                _run_copies(tail_ref, bm, fill, wait)
                block_fill(wait)
                for ti in range(n_later):
                    _run_copies(later_ref, later_tm, fill, wait, tile=ti)
    else:
        _, xs_ref, sorted_ref, sem = rest
    _, build = _slot_onehot(route_ref, tm)
    perm = build([1.0] * TOP_K).astype(BF16)
    sorted_ref[slot] = lax.dot_general(perm, h_ref[...].astype(BF16), (((0,), (0,)), ((), ())),
                                       preferred_element_type=F32)

    def runs(s):
        return lambda loc, dst, sz: pltpu.make_async_copy(sorted_ref.at[s, pl.ds(loc, sz)],
                                                          xs_ref.at[pl.ds(dst, sz)], sem.at[s])

    _run_copies(tab_ref, tm, runs(slot), wait=False)

    @pl.when(i > 0)
    def _():
        _run_copies(tabp_ref, tm, runs(1 - slot), wait=True)

    @pl.when(i == pl.num_programs(0) - 1)
    def _():
        _run_copies(tab_ref, tm, runs(slot), wait=True)


def _dispatch(tab, tail, route, h2, xs, n_pad, tm, later=None, later_tm=None):
    n, d = h2.shape
    tok = lambda i: (i, 0)
    first = xs is None
    in_specs = [pl.BlockSpec((1, 1, LANE), lambda i: (i, 0, 0), memory_space=pltpu.SMEM),
                pl.BlockSpec((1, 1, LANE), lambda i: (jnp.maximum(i - 1, 0), 0, 0), memory_space=pltpu.SMEM),
                pl.BlockSpec((1, 1, LANE), lambda i: (0, 0, 0), memory_space=pltpu.SMEM),
                pl.BlockSpec((tm, LANE), tok), pl.BlockSpec((tm, d), tok)]
    scratch = [pltpu.VMEM((2, _local_rows(tm), d), F32), pltpu.SemaphoreType.DMA((2,))]
    args = [tab, tab, tail, route, h2]
    if first:
        scratch.append(pltpu.VMEM((MOE_BLOCK_ROWS, d), F32))
        in_specs.append(pl.BlockSpec(later.shape, lambda i: (0, 0, 0), memory_space=pltpu.SMEM))
        args.append(later)
    else:
        in_specs.append(pl.BlockSpec(memory_space=pl.ANY))
        args.append(xs)
    return pl.pallas_call(
        functools.partial(_dispatch_kernel, tm=tm, first=first, later_tile=later_tm),
        grid=(n // tm,),
        in_specs=in_specs,
        out_specs=pl.BlockSpec(memory_space=pl.ANY),
        out_shape=jax.ShapeDtypeStruct((n_pad, d), F32),
        scratch_shapes=scratch,
        input_output_aliases={} if first else {5: 0},
        compiler_params=_cparams(("arbitrary",)),
        name="moe_dispatch",
    )(*args)


def _expert_kernel(be_ref, nv_ref, x_ref, wgu_ref, bgu_ref, wd_ref, bd_ref, y_ref, wgu_bf, wd_bf):
    g = pl.program_id(0)

    @pl.when(jnp.logical_or(g == 0, be_ref[g] != be_ref[jnp.maximum(g - 1, 0)]))
    def _():
        wgu_bf[...] = wgu_ref[0, 0].astype(BF16)
        wd_bf[...] = wd_ref[0, 0].astype(BF16)

    @pl.when(g < nv_ref[0])
    def _():
        gu = _dot(x_ref[...].astype(BF16), wgu_bf[...]) + bgu_ref[0, 0]
        gt = jnp.minimum(gu[:, :D_FF], SWIGLU_LIMIT)
        up = jnp.clip(gu[:, D_FF:], -SWIGLU_LIMIT, SWIGLU_LIMIT)
        act = (up + 1.0) * (gt * _sigmoid(SWIGLU_ALPHA * gt))
        y_ref[...] = _dot(act.astype(BF16), wd_bf[...]) + bd_ref[0, 0]

    @pl.when(g >= nv_ref[0])
    def _():
        y_ref[...] = jnp.zeros_like(y_ref)


def _experts(l, block_e, nvalid, xs, wgu, bgu, wd, bd):
    n_pad, d = xs.shape
    depth = wgu.shape[0]
    bm = MOE_BLOCK_ROWS
    nb = n_pad // bm
    grid_spec = pltpu.PrefetchScalarGridSpec(
        num_scalar_prefetch=2,
        grid=(nb,),
        in_specs=[pl.BlockSpec((bm, d), lambda g, be, nv: (jnp.minimum(g, nv[0] - 1), 0)),
                  pl.BlockSpec((1, 1, d, 2 * D_FF), lambda g, be, nv: (l, be[g], 0, 0)),
                  pl.BlockSpec((1, 1, 1, 2 * D_FF), lambda g, be, nv: (l, be[g], 0, 0)),
                  pl.BlockSpec((1, 1, D_FF, d), lambda g, be, nv: (l, be[g], 0, 0)),
                  pl.BlockSpec((1, 1, 1, d), lambda g, be, nv: (l, be[g], 0, 0))],
        out_specs=pl.BlockSpec((bm, d), lambda g, be, nv: (g, 0)),
        scratch_shapes=[pltpu.VMEM((d, 2 * D_FF), BF16), pltpu.VMEM((D_FF, d), BF16)],
    )
    return pl.pallas_call(
        _expert_kernel,
        grid_spec=grid_spec,
        out_shape=jax.ShapeDtypeStruct((n_pad, d), F32),
        compiler_params=_cparams(("arbitrary",)),
        name="moe_experts",
    )(block_e, nvalid, xs, wgu, bgu.reshape(depth, N_EXPERTS, 1, -1), wd, bd.reshape(depth, N_EXPERTS, 1, -1))


def _combine_kernel(tab_ref, tabn_ref, route_ref, ys_ref, xm_ref, gt_ref, g_ref, o_ref, buf_ref, sem, *, tm):
    i = pl.program_id(0)
    slot = i % 2

    def runs(s):
        return lambda loc, dst, sz: pltpu.make_async_copy(ys_ref.at[pl.ds(dst, sz)],
                                                          buf_ref.at[s, pl.ds(loc, sz)], sem.at[s])

    @pl.when(i == 0)
    def _():
        buf_ref[...] = jnp.zeros_like(buf_ref)
        _run_copies(tab_ref, tm, runs(0), wait=False)

    @pl.when(i + 1 < pl.num_programs(0))
    def _():
        _run_copies(tabn_ref, tm, runs(1 - slot), wait=False)

    route, build = _slot_onehot(route_ref, tm)
    gate_m = build([route[:, kk:kk + 1] for kk in range(TOP_K)])
    g_hi, g_lo = _split_bf16(gate_m)
    _run_copies(tab_ref, tm, runs(slot), wait=True)
    b_hi, b_lo = _split_bf16(buf_ref[slot])
    y = _dot(g_hi, b_hi) + _dot(g_lo, b_hi) + _dot(g_hi, b_lo)
    o_ref[...] = xm_ref[...] + gt_ref[0] * _rms(y, g_ref[...])


def _combine(tab, route, ys, xm, gt, g, seq, tm):
    n, d = xm.shape
    per_b = seq // tm
    nt = n // tm
    tok = lambda i: (i, 0)
    return pl.pallas_call(
        functools.partial(_combine_kernel, tm=tm),
        grid=(nt,),
        in_specs=[pl.BlockSpec((1, 1, LANE), lambda i: (i, 0, 0), memory_space=pltpu.SMEM),
                  pl.BlockSpec((1, 1, LANE), lambda i: (jnp.minimum(i + 1, nt - 1), 0, 0),
                               memory_space=pltpu.SMEM),
                  pl.BlockSpec((tm, LANE), tok), pl.BlockSpec(memory_space=pl.ANY),
                  pl.BlockSpec((tm, d), tok),
                  pl.BlockSpec((1, 1, d), lambda i: (i // per_b, 0, 0)),
                  pl.BlockSpec((1, d), lambda i: (0, 0))],
        out_specs=pl.BlockSpec((tm, d), tok),
        out_shape=jax.ShapeDtypeStruct((n, d), F32),
        scratch_shapes=[pltpu.VMEM((2, _local_rows(tm), d), F32), pltpu.SemaphoreType.DMA((2,))],
        compiler_params=_cparams(("arbitrary",)),
        name="moe_combine",
    )(tab, tab, route, ys, xm, gt, g.reshape(1, d))


def _pad_heads(w, width):
    lead = w.shape[:-1]
    w = w.reshape(lead + (H_B, width))
    w = jnp.pad(w, [(0, 0)] * len(lead) + [(0, 0), (0, LANE - width)])
    return w.reshape(lead + (HP_B,))


def _layer_weights(l, w_in, w_gla_gate, b_gla_gate, g_gla_norm, w_rg_a, w_rg_x, w_out, w_router, b_router):
    d = D_MODEL
    wi = w_in[l]
    o = 3 * W_A
    qb = wi[:, o:o + H_B * DK_B]
    kb = wi[:, o + 192:o + 384]
    vb = wi[:, o + 384:o + 768]
    gb = wi[:, o + 768:o + 1152]
    rb = wi[:, o + 1152:o + 1168]
    xy = wi[:, o + 1168:]
    w_in_p = jnp.concatenate(
        [wi[:, :o], _pad_heads(qb, DK_B), _pad_heads(kb, DK_B), _pad_heads(vb, DV_B), _pad_heads(gb, DV_B),
         jnp.pad(rb, ((0, 0), (0, LANE - GATE_RANK_B))), xy], axis=1).astype(BF16)
    wg_p = jnp.pad(_pad_heads(w_gla_gate[l], DK_B), ((0, LANE - GATE_RANK_B), (0, 0))).astype(BF16)
    bg_p = _pad_heads(b_gla_gate[l].reshape(1, -1), DK_B)
    gn_p = jnp.tile(jnp.pad(g_gla_norm[l], (0, LANE - DV_B)), H_B).reshape(1, HP_B)
    wa_bd = jax.scipy.linalg.block_diag(*[w_rg_a[l, i] for i in range(NBLK_C)]).astype(BF16)
    wx_bd = jax.scipy.linalg.block_diag(*[w_rg_x[l, i] for i in range(NBLK_C)]).astype(BF16)
    wo = w_out[l]
    wo_a = wo[:W_A].astype(BF16)
    wo_b = jnp.pad(wo[W_A:W_A + W_B].reshape(H_B, DV_B, d), ((0, 0), (0, LANE - DV_B), (0, 0)))
    wo_b = wo_b.reshape(HP_B, d).astype(BF16)
    wo_c = wo[W_A + W_B:].astype(BF16)
    wr = jnp.pad(w_router[l], ((0, 0), (0, LANE - N_EXPERTS)))
    wr_hi = wr.astype(BF16)
    wr_lo = (wr - wr_hi.astype(F32)).astype(BF16)
    br = jnp.pad(b_router[l], (0, LANE - N_EXPERTS), constant_values=-1e30).reshape(1, LANE)
    return dict(w_in_p=w_in_p, wg_p=wg_p, bg_p=bg_p, gn_p=gn_p, wa_bd=wa_bd, wx_bd=wx_bd,
                wo_a=wo_a, wo_b=wo_b, wo_c=wo_c, wr_hi=wr_hi, wr_lo=wr_lo, br=br)


def _pad_state(s):
    st = jnp.swapaxes(s, -1, -2)
    return jnp.pad(st, ((0, 0), (0, 0), (0, LANE - DV_B), (0, LANE - DK_B)))


def _unpad_state(st):
    return jnp.swapaxes(st[:, :, :DV_B, :DK_B], -1, -2)


def kernel(x_prompt, x_sample, c_prompt, c_sample, cache_k_sb, cache_v_sb, state_gla, state_conv, state_lru, w_ada, b_ada, g_pre_mix, g_post_mix, g_pre_ff, g_post_ff, w_in, w_gla_gate, b_gla_gate, g_gla_norm, w_conv, b_conv, w_rg_a, b_rg_a, w_rg_x, b_rg_x, lru_lambda, w_out, w_router, b_router, w_gate_up, b_gate_up, w_down, b_down):
    depth = w_ada.shape[0]
    d = D_MODEL
    groups = []
    for x, past in ((x_prompt, False), (x_sample, True)):
        b, t, _ = x.shape
        groups.append(dict(b=b, t=t, tm=min(TOKEN_TILE, t), n=b * t, past=past, x=x.reshape(b * t, d)))
    n_all = sum(g["n"] for g in groups)
    bm = MOE_BLOCK_ROWS
    n_tiles = sum(g["n"] // g["tm"] for g in groups)
    nb = -(-(n_all * TOP_K + n_tiles * N_EXPERTS * (SUBLANE - 1)) // bm) + N_EXPERTS
    n_pad = nb * bm

    nb_rows = sum(g["b"] for g in groups)
    c_all = jnp.concatenate([c_prompt, c_sample], axis=0)
    r_pad = -(-nb_rows // SUBLANE) * SUBLANE
    c_all = jnp.pad(c_all, ((0, r_pad - nb_rows), (0, 0)))
    mod = _modulation(c_all, w_ada, b_ada)

    st =[[[] for _ in range(5)] for _ in groups]
    for l in range(depth):
        lw = _layer_weights(l, w_in, w_gla_gate, b_gla_gate, g_gla_norm, w_rg_a, w_rg_x, w_out,
                            w_router, b_router)
        row0 = 0
        per_group = []
        for gi, g in enumerate(groups):
            b, t, tm, n = g["b"], g["t"], g["tm"], g["n"]
            m = mod[l, row0:row0 + b].reshape(b, 1, 6 * d)
            row0 += b
            sh_m, sc_m, gt_m, sh_f, sc_f, gt_f = [m[:, :, j * d:(j + 1) * d] for j in range(6)]
            ka, va, qab, kab, vab, gla_in, lru_in = _premix(g["x"], g_pre_mix[l], sc_m, sh_m, lw["w_in_p"], t, tm)
            q3 = qab.reshape(b, t, W_A)
            k3 = kab.reshape(b, t, W_A)
            v3 = vab.reshape(b, t, W_A)
            if g["past"]:
                n_past = cache_k_sb.shape[2]
                padk = ((0, 0), (0, -t % ATTN_KEY_TILE), (0, 0))
                oa = _attention(q3, jnp.pad(k3, padk), jnp.pad(v3, padk),
                                cache_k_sb.reshape(depth, b, n_past, W_A), cache_v_sb.reshape(depth, b, n_past, W_A),
                                layer=l)
                s0 = state_gla[l]
                buf0 = state_conv[l]
                h0 = state_lru[l]
            else:
                oa = _attention(q3, k3, v3)
                s0 = jnp.zeros((b, H_B, DK_B, DV_B), F32)
                buf0 = jnp.zeros((b, CONV_W - 1, W_C), F32)
                h0 = jnp.zeros((b, W_C), F32)
            oa = oa.reshape(n, W_A)
            ob, s_new = _gla(gla_in, lw["wg_p"], lw["bg_p"], lw["gn_p"], _pad_state(s0), t)
            buf0p = jnp.pad(buf0, ((0, 0), (SUBLANE - (CONV_W - 1), 0), (0, 0)))
            oc, conv_new, h_new = _lru(lru_in, w_conv[l], b_conv[l].reshape(1, -1), lw["wa_bd"],
                                       b_rg_a[l].reshape(1, -1), lw["wx_bd"], b_rg_x[l].reshape(1, -1),
                                       lru_lambda[l].reshape(1, -1), buf0p, h0.reshape(b, 1, W_C), t)
            xm, h2, route, stat = _postmix(g["x"], oa, ob, oc, lw["wo_a"], lw["wo_b"], lw["wo_c"],
                                           g_post_mix[l], g_pre_ff[l], gt_m, sc_f, sh_f,
                                           lw["wr_hi"], lw["wr_lo"], lw["br"], t, tm)
            per_group.append(dict(xm=xm, h2=h2, route=route, stat=stat, gt_f=gt_f))
            new = (ka.reshape(b, t, W_A // HD_A, HD_A), va.reshape(b, t, W_A // HD_A, HD_A),
                   _unpad_state(s_new), conv_new[:, SUBLANE - (CONV_W - 1):], h_new.reshape(b, W_C))
            for j in range(5):
                st[gi][j].append(new[j])

        stat = jnp.concatenate([p["stat"].reshape(-1, SUBLANE, LANE)[:, :2, :N_EXPERTS] for p in per_group],
                               axis=0).astype(jnp.int32)
        run_rows, run_loc = stat[:, 0], stat[:, 1]
        before = jnp.cumsum(run_rows, axis=0) - run_rows
        counts = jnp.sum(run_rows, axis=0)
        padded = (counts + bm - 1) // bm * bm
        pad_end = jnp.cumsum(padded)
        pad_start = pad_end - padded
        block_start = jnp.arange(nb, dtype=jnp.int32) * bm
        block_e = jnp.minimum(jnp.sum((pad_end[None, :] <= block_start[:, None]).astype(jnp.int32), axis=1),
                              N_EXPERTS - 1)
        nvalid = (pad_end[-1] // bm).astype(jnp.int32).reshape(1)
        tab = jnp.concatenate([run_rows, run_loc, pad_start[None, :] + before, jnp.zeros_like(run_rows)],
                              axis=1).reshape(-1, 1, LANE)
        tile0 = 0
        for gi, g in enumerate(groups):
            nt = g["n"] // g["tm"]
            per_group[gi]["tab"] = tab[tile0:tile0 + nt]
            tile0 += nt
        zeros32 = jnp.zeros_like(counts)
        tail = jnp.concatenate([padded - counts, zeros32, pad_start + counts,
                                jnp.broadcast_to(nvalid, (N_EXPERTS,))]).reshape(1, 1, LANE)
        first, second = per_group
        xs = _dispatch(first["tab"], tail, first["route"], first["h2"], None, n_pad, groups[0]["tm"],
                       later=second["tab"], later_tm=groups[1]["tm"])
        xs = _dispatch(second["tab"], tail, second["route"], second["h2"], xs, n_pad, groups[1]["tm"])
        ys = _experts(l, block_e, nvalid, xs, w_gate_up, b_gate_up, w_down, b_down)
        for gi, g in enumerate(groups):
            p = per_group[gi]
            g["x"] = _combine(p["tab"], p["route"], ys, p["xm"], p["gt_f"], g_post_ff[l], g["t"], g["tm"])

    outs = [g["x"].reshape(g["b"], g["t"], d) for g in groups]
    for gi in range(len(groups)):
        outs.extend(jnp.stack(st[gi][j]) for j in range(5))
    return tuple(outs)
```

```python
import functools
import math

import numpy as np
import jax
import jax.numpy as jnp
from jax import lax
from jax.experimental import pallas as pl
from jax.experimental.pallas import tpu as pltpu

F32 = jnp.float32
BF16 = jnp.bfloat16

D_MODEL = 1024
EPS = 1e-6
HD_A = 64
W_A = 384
H_B = 4
DK_B = 48
DV_B = 96
W_B = 384
GATE_RANK_B = 16
GATE_TEMP_B = 16.0
W_C = 256
NBLK_C = 4
BW_C = 64
CONV_W = 4
RG_C = 8.0
N_EXPERTS = 32
TOP_K = 4
D_FF = 1024
SWIGLU_ALPHA = 1.702
SWIGLU_LIMIT = 7.0
LOG_WEIGHT_FLOOR = -120.0

LANE = 128
SUBLANE = 8
VMEM_LIMIT = 56 * 1024 * 1024

HP_B = H_B * LANE
OFF_GLA = 3 * W_A
GLA_W = 4 * HP_B + LANE
OFF_LRU = OFF_GLA + GLA_W
IN_PAD = OFF_LRU + 2 * W_C

TOKEN_TILE = 256
ATTN_TILE = 256
ATTN_KEY_TILE = 256
ATTN_SUBTILES = 8
GLA_CHUNK = 128
GLA_CHUNKS_PER_STEP = 2
LRU_CHUNK = 256
MOE_BLOCK_ROWS = 512
RUN_BIG_ROWS = 64


def _cparams(sem):
    return pltpu.CompilerParams(dimension_semantics=sem, vmem_limit_bytes=VMEM_LIMIT)


def _sigmoid(x):
    return 1.0 / (1.0 + jnp.exp(-x))


def _log_sigmoid(x):
    return jnp.minimum(x, 0.0) - jnp.log(1.0 + jnp.exp(-jnp.abs(x)))


def _split_bf16(x):
    hi = x.astype(BF16)
    lo = (x - hi.astype(F32)).astype(BF16)
    return hi, lo


def _dot(a, b):
    return jnp.dot(a, b, preferred_element_type=F32)


def _dot_nt(a, b):
    return lax.dot_general(a, b, (((1,), (1,)), ((), ())), preferred_element_type=F32)


def _rms(x, g):
    return x * lax.rsqrt(jnp.mean(x * x, axis=-1, keepdims=True) + EPS) * g


def _mod_kernel(c_ref, w_ref, b_ref, o_ref):
    c = c_ref[...]
    a = (c * _sigmoid(c)).astype(BF16)
    o_ref[0] = _dot(a, w_ref[0].astype(BF16)) + b_ref[0]


def _modulation(c_all, w_ada, b_ada):
    depth, d, n = w_ada.shape
    r = c_all.shape[0]
    tn = 1536
    return pl.pallas_call(
        _mod_kernel,
        grid=(depth, n // tn),
        in_specs=[pl.BlockSpec((r, d), lambda l, j: (0, 0)),
                  pl.BlockSpec((1, d, tn), lambda l, j: (l, 0, j)),
                  pl.BlockSpec((1, 1, tn), lambda l, j: (l, 0, j))],
        out_specs=pl.BlockSpec((1, r, tn), lambda l, j: (l, 0, j)),
        out_shape=jax.ShapeDtypeStruct((depth, r, n), F32),
        compiler_params=_cparams(("arbitrary", "arbitrary")),
        name="adaln_mod",
    )(c_all, w_ada, b_ada.reshape(depth, 1, n))


def _premix_kernel(x_ref, g_ref, sc_ref, sh_ref, w_ref,
                   ka_ref, va_ref, qab_ref, kab_ref, vab_ref, gla_ref, lru_ref):
    h = _rms(x_ref[...], g_ref[...]) * (1.0 + sc_ref[0]) + sh_ref[0]
    r = _dot(h.astype(BF16), w_ref[...])
    ka = r[:, W_A:2 * W_A]
    va = r[:, 2 * W_A:3 * W_A]
    ka_ref[...] = ka
    va_ref[...] = va
    qab_ref[...] = (r[:, 0:W_A] * (HD_A ** -0.5)).astype(BF16)
    kab_ref[...] = ka.astype(BF16)
    vab_ref[...] = va.astype(BF16)
    gla_ref[...] = r[:, OFF_GLA:OFF_LRU]
    lru_ref[...] = r[:, OFF_LRU:IN_PAD]


def _premix(x2, g, sc, sh, w_in_p, seq, tm):
    n, d = x2.shape
    per_b = seq // tm
    tok = lambda i: (i, 0)
    bat = lambda i: (i // per_b, 0, 0)
    const = lambda i: (0, 0)
    outs = [((n, W_A), F32), ((n, W_A), F32), ((n, W_A), BF16), ((n, W_A), BF16), ((n, W_A), BF16),
            ((n, GLA_W), F32), ((n, 2 * W_C), F32)]
    return pl.pallas_call(
        _premix_kernel,
        grid=(n // tm,),
        in_specs=[pl.BlockSpec((tm, d), tok), pl.BlockSpec((1, d), const),
                  pl.BlockSpec((1, 1, d), bat), pl.BlockSpec((1, 1, d), bat),
                  pl.BlockSpec((d, IN_PAD), const)],
        out_specs=[pl.BlockSpec((tm, s[1]), tok) for s, _ in outs],
        out_shape=[jax.ShapeDtypeStruct(s, t) for s, t in outs],
        compiler_params=_cparams(("arbitrary",)),
        name="premix_proj",
    )(x2, g.reshape(1, d), sc, sh, w_in_p)


def _attn_kernel(q_ref, k_ref, v_ref, *rest, tq, tk, off, has_past, nsub):
    if has_past:
        kp_ref, vp_ref, u_ref, o_ref, acc_ref, c_ref = rest
    else:
        u_ref, o_ref, acc_ref, c_ref = rest
    i = pl.program_id(2)
    lane = lax.broadcasted_iota(jnp.int32, (1, LANE), 1)
    head_lanes = (lane < HD_A, lane >= HD_A)
    acc_ref[...] = jnp.zeros_like(acc_ref)
    c_ref[...] = jnp.zeros_like(c_ref)
    row = lax.broadcasted_iota(jnp.int32, (tq, tk), 0)
    col = lax.broadcasted_iota(jnp.int32, (tq, tk), 1)
    u = u_ref[...]
    qh, qpos0, nk = [], [], []
    for a in range(nsub):
        q = q_ref[0, a * tq:(a + 1) * tq, :]
        qz = jnp.zeros_like(q)
        qh.append(tuple(jnp.where(m, q, qz) for m in head_lanes))
        qpos0.append(off + (i * nsub + a) * tq)
        nk.append((qpos0[a] + tq - 1 + tk - 1) // tk)

    def step(jj, masked):
        cmax = jnp.float32(-jnp.inf)
        for a in range(nsub):
            j = nk[a] - 1 - jj
            live = j >= 0
            ks = pl.multiple_of(jnp.maximum(j, 0) * tk, tk)
            if has_past and not masked:
                kb = kp_ref[0, 0, pl.ds(ks, tk), :].astype(BF16)
                vb = vp_ref[0, 0, pl.ds(ks, tk), :].astype(BF16)
            else:
                kb = k_ref[0, pl.ds(pl.multiple_of(ks - off, tk), tk), :]
                vb = v_ref[0, pl.ds(pl.multiple_of(ks - off, tk), tk), :]
            vz = jnp.zeros_like(vb)
            mask = (col + ks) < (row + qpos0[a])
            for h in range(2):
                s = _dot_nt(qh[a][h], kb)
                lf = -(jnp.maximum(s, 0.0) + jnp.log(1.0 + jnp.exp(-jnp.abs(s))))
                lb = s + lf
                if masked:
                    lf = jnp.where(mask, lf, 0.0)
                hi, lo = _split_bf16(lf)
                cr = _dot(jnp.concatenate([hi, lo], axis=1), u)
                c = c_ref[a, h]
                cfull = jnp.concatenate([c] * (tk // LANE), axis=1)
                w = jnp.exp(lb + cr[:, :tk] + cfull)
                if masked:
                    w = jnp.where(mask, w, 0.0)
                vh = jnp.where(head_lanes[h], vb, vz)
                pv = _dot(w.astype(BF16), vh)
                rs = cr[:, tk:]
                if not masked:
                    pv = jnp.where(live, pv, 0.0)
                    rs = jnp.where(live, rs, 0.0)
                acc_ref[a] += pv
                c_ref[a, h] = c + rs
            cmax = jnp.maximum(cmax, jnp.where(j > 0, jnp.max(c_ref[a]), -jnp.inf))
        return cmax > LOG_WEIGHT_FLOOR

    n_masked = max(1, tq // tk)
    for jj in range(n_masked):
        go = step(jj, True)

    def body(carry):
        jj, _ = carry
        return jj + 1, step(jj, False)

    lax.while_loop(lambda carry: jnp.logical_and(carry[0] < nk[-1], carry[1]), body, (jnp.int32(n_masked), go))
    for a in range(nsub):
        o_ref[0, a * tq:(a + 1) * tq, :] = acc_ref[a].astype(o_ref.dtype)


def _attn_umat(tk):
    jp = np.arange(tk)[:, None]
    j = np.arange(tk)[None, :]
    u = np.concatenate([(jp > j).astype(np.float32), np.ones((tk, LANE), np.float32)], axis=1)
    return jnp.asarray(np.concatenate([u, u], axis=0), BF16)


def _attention(q, k, v, kp=None, vp=None, layer=0):
    b, tq_all, _ = q.shape
    tc = k.shape[1]
    has_past = kp is not None
    off = kp.shape[2] if has_past else 0
    tq = min(ATTN_TILE, tq_all)
    tk = ATTN_KEY_TILE
    assert tc % tk == 0 and tq_all % tq == 0 and tq_all <= tc
    assert off % tk == 0 and (tk % tq == 0 or tq % tk == 0)
    assert not has_past or (tq == tq_all and tq_all <= tk)
    nsub = math.gcd(ATTN_SUBTILES, tq_all // tq)
    kern = functools.partial(_attn_kernel, tq=tq, tk=tk, off=off, has_past=has_past, nsub=nsub)
    cur = pl.BlockSpec((1, tc, LANE), lambda bi, hp, i: (bi, 0, hp))
    in_specs = [pl.BlockSpec((1, nsub * tq, LANE), lambda bi, hp, i: (bi, i, hp)), cur, cur]
    args = [q, k, v]
    if has_past:
        past = pl.BlockSpec((1, 1, off, LANE), lambda bi, hp, i: (layer, bi, 0, hp))
        in_specs += [past, past]
        args += [kp, vp]
    in_specs.append(pl.BlockSpec((2 * tk, tk + LANE), lambda bi, hp, i: (0, 0)))
    args.append(_attn_umat(tk))
    return pl.pallas_call(
        kern,
        grid=(b, W_A // LANE, tq_all // (nsub * tq)),
        in_specs=in_specs,
        out_specs=pl.BlockSpec((1, nsub * tq, LANE), lambda bi, hp, i: (bi, i, hp)),
        out_shape=jax.ShapeDtypeStruct((b, tq_all, W_A), BF16),
        scratch_shapes=[pltpu.VMEM((nsub, tq, LANE), F32), pltpu.VMEM((nsub, 2, tq, LANE), F32)],
        compiler_params=_cparams(("arbitrary", "arbitrary", "arbitrary")),
        name="stick_breaking_attn",
    )(*args)


def _gla_levels(c):
    return int(math.log2(c))


def _gla_mats(c):
    t = np.arange(c)[:, None]
    s = np.arange(c)[None, :]
    mats = [(s <= t).astype(np.float32), (s > t).astype(np.float32)]
    for lv in range(_gla_levels(c)):
        m = 1 << lv
        ref = (t // (2 * m)) * (2 * m) + m - 1
        mats.append(((s > ref) & (s <= t)).astype(np.float32) - ((s > t) & (s <= ref)).astype(np.float32))
    m = np.concatenate(mats, axis=0)
    return jnp.asarray(np.concatenate([m, m], axis=1), BF16)


def _gla_kernel(in_ref, wg_ref, bg_ref, gn_ref, mall_ref, s0_ref, ob_ref, sout_ref, st_ref, *, C, G):
    ci = pl.program_id(1)

    @pl.when(ci == 0)
    def _():
        st_ref[...] = s0_ref[0]

    for sub in range(G):
        _gla_chunk(in_ref, wg_ref, bg_ref, gn_ref, mall_ref, ob_ref, st_ref, C, sub * C)

    @pl.when(ci == pl.num_programs(1) - 1)
    def _():
        sout_ref[0] = st_ref[...]


def _gla_chunk(in_ref, wg_ref, bg_ref, gn_ref, mall_ref, ob_ref, st_ref, C, r0):
    rows = slice(r0, r0 + C)
    q = in_ref[rows, 0:HP_B] * (DK_B ** -0.5)
    k = in_ref[rows, HP_B:2 * HP_B]
    v = in_ref[rows, 2 * HP_B:3 * HP_B]
    gb = in_ref[rows, 3 * HP_B:4 * HP_B]
    rb = in_ref[rows, 4 * HP_B:4 * HP_B + LANE]
    lg = _log_sigmoid(_dot(rb.astype(BF16), wg_ref[...]) + bg_ref[...]) * (1.0 / GATE_TEMP_B)
    hi, lo = _split_bf16(lg)
    dall = _dot(mall_ref[...], jnp.concatenate([hi, lo], axis=0))
    eb = jnp.exp(dall[0:C])
    elast = jnp.exp(dall[C:2 * C])
    eb_last = eb[C - 1:C, :]
    rowi = lax.broadcasted_iota(jnp.int32, (C, 1), 0)
    row = lax.broadcasted_iota(jnp.int32, (C, C), 0)
    col = lax.broadcasted_iota(jnp.int32, (C, C), 1)
    qe = (q * eb).astype(BF16)
    ke = (k * elast).astype(BF16)
    qb = q.astype(BF16)
    kb = k.astype(BF16)
    vb = v.astype(BF16)
    zero = jnp.zeros_like(q)
    lv_q, lv_k = [], []
    for lv in range(_gla_levels(C)):
        e = jnp.exp(-jnp.abs(dall[(2 + lv) * C:(3 + lv) * C]))
        second = ((rowi >> lv) & 1) == 1
        lv_q.append(jnp.where(second, q * e, zero).astype(BF16))
        lv_k.append(jnp.where(second, zero, k * e).astype(BF16))
    outs = []
    for h in range(H_B):
        sl = slice(h * LANE, (h + 1) * LANE)
        att = jnp.where(row == col, _dot_nt(qb[:, sl], kb[:, sl]), 0.0)
        for lv in range(_gla_levels(C)):
            same = (row >> (lv + 1)) == (col >> (lv + 1))
            att = att + jnp.where(same, _dot_nt(lv_q[lv][:, sl], lv_k[lv][:, sl]), 0.0)
        st = st_ref[h]
        o = _dot(att.astype(BF16), vb[:, sl]) + _dot_nt(qe[:, sl], st.astype(BF16))
        st_ref[h] = st * eb_last[:, sl] + _dot(v[:, sl].T.astype(BF16), ke[:, sl])
        ms = jnp.sum(o * o, axis=-1, keepdims=True) * (1.0 / DV_B)
        on = o * lax.rsqrt(ms + EPS) * gn_ref[:, sl]
        g = gb[:, sl]
        outs.append((on * (g * _sigmoid(g))).astype(BF16))
    ob_ref[rows, :] = jnp.concatenate(outs, axis=1)


def _gla(gla_in, wg_p, bg_p, gn_p, s0t, seq):
    n = gla_in.shape[0]
    b = n // seq
    c = min(GLA_CHUNK, seq)
    g = math.gcd(GLA_CHUNKS_PER_STEP, seq // c)
    per_b = seq // (c * g)
    mall = _gla_mats(c)
    const2 = lambda bi, ci: (0, 0)
    return pl.pallas_call(
        functools.partial(_gla_kernel, C=c, G=g),
        grid=(b, per_b),
        in_specs=[pl.BlockSpec((c * g, GLA_W), lambda bi, ci: (bi * per_b + ci, 0)),
                  pl.BlockSpec((LANE, HP_B), const2), pl.BlockSpec((1, HP_B), const2),
                  pl.BlockSpec((1, HP_B), const2), pl.BlockSpec(mall.shape, const2),
                  pl.BlockSpec((1, H_B, LANE, LANE), lambda bi, ci: (bi, 0, 0, 0))],
        out_specs=[pl.BlockSpec((c * g, HP_B), lambda bi, ci: (bi * per_b + ci, 0)),
                   pl.BlockSpec((1, H_B, LANE, LANE), lambda bi, ci: (bi, 0, 0, 0))],
        out_shape=[jax.ShapeDtypeStruct((n, HP_B), BF16),
                   jax.ShapeDtypeStruct((b, H_B, LANE, LANE), F32)],
        scratch_shapes=[pltpu.VMEM((H_B, LANE, LANE), F32)],
        compiler_params=_cparams(("arbitrary", "arbitrary")),
        name="gla_chunked",
    )(gla_in, wg_p, bg_p, gn_p, mall, s0t)


def _lru_kernel(in_ref, cw_ref, cb_ref, wa_ref, ba_ref, wx_ref, bx_ref, lam_ref, buf0_ref, h0_ref,
                oc_ref, conv_ref, hout_ref, xp_ref, hc_ref, *, C):
    ci = pl.program_id(1)

    @pl.when(ci == 0)
    def _():
        xp_ref[0:SUBLANE] = buf0_ref[0]
        hc_ref[...] = h0_ref[0]

    x = in_ref[:, 0:W_C]
    y = in_ref[:, W_C:2 * W_C]
    xp_ref[SUBLANE:SUBLANE + C] = x
    xc = cb_ref[...]
    for j in range(CONV_W):
        xc = xc + xp_ref[pl.ds(SUBLANE - (CONV_W - 1) + j, C), :] * cw_ref[j:j + 1, :]
    xcb = xc.astype(BF16)
    r = _sigmoid(_dot(xcb, wa_ref[...]) + ba_ref[...])
    gi = _sigmoid(_dot(xcb, wx_ref[...]) + bx_ref[...])
    log_a = RG_C * r * _log_sigmoid(lam_ref[...])
    a = jnp.exp(log_a)
    u = jnp.sqrt(1.0 - jnp.exp(2.0 * log_a)) * (gi * xc)
    rowi = lax.broadcasted_iota(jnp.int32, (C, 1), 0)
    d = 1
    while d < C:
        keep = rowi >= d
        a_s = pltpu.roll(a, d, axis=0)
        u_s = pltpu.roll(u, d, axis=0)
        u = jnp.where(keep, a * u_s + u, u)
        a = jnp.where(keep, a * a_s, a)
        d *= 2
    hseq = u + a * hc_ref[...]
    hc_ref[...] = hseq[C - 1:C, :]
    gelu = 0.5 * y * (1.0 + jnp.tanh(math.sqrt(2.0 / math.pi) * (y + 0.044715 * (y * y * y))))
    oc_ref[...] = (hseq * gelu).astype(BF16)
    tail = xp_ref[C:C + SUBLANE]
    xp_ref[0:SUBLANE] = tail

    @pl.when(ci == pl.num_programs(1) - 1)
    def _():
        conv_ref[0] = tail
        hout_ref[0] = hseq[C - 1:C, :]


def _lru(lru_in, cw, cb, wa_bd, ba, wx_bd, bx, lam, buf0, h0, seq):
    n = lru_in.shape[0]
    b = n // seq
    c = min(LRU_CHUNK, seq)
    per_b = seq // c
    const2 = lambda bi, ci: (0, 0)
    vec = pl.BlockSpec((1, W_C), const2)
    return pl.pallas_call(
        functools.partial(_lru_kernel, C=c),
        grid=(b, per_b),
        in_specs=[pl.BlockSpec((c, 2 * W_C), lambda bi, ci: (bi * per_b + ci, 0)),
                  pl.BlockSpec((CONV_W, W_C), const2), vec,
                  pl.BlockSpec((W_C, W_C), const2), vec, pl.BlockSpec((W_C, W_C), const2), vec, vec,
                  pl.BlockSpec((1, SUBLANE, W_C), lambda bi, ci: (bi, 0, 0)),
                  pl.BlockSpec((1, 1, W_C), lambda bi, ci: (bi, 0, 0))],
        out_specs=[pl.BlockSpec((c, W_C), lambda bi, ci: (bi * per_b + ci, 0)),
                   pl.BlockSpec((1, SUBLANE, W_C), lambda bi, ci: (bi, 0, 0)),
                   pl.BlockSpec((1, 1, W_C), lambda bi, ci: (bi, 0, 0))],
        out_shape=[jax.ShapeDtypeStruct((n, W_C), BF16),
                   jax.ShapeDtypeStruct((b, SUBLANE, W_C), F32),
                   jax.ShapeDtypeStruct((b, 1, W_C), F32)],
        scratch_shapes=[pltpu.VMEM((c + SUBLANE, W_C), F32), pltpu.VMEM((1, W_C), F32)],
        compiler_params=_cparams(("arbitrary", "arbitrary")),
        name="conv_rglru",
    )(lru_in, cw, cb, wa_bd, ba, wx_bd, bx, lam, buf0, h0)


def _postmix_kernel(x_ref, oa_ref, ob_ref, oc_ref, wa_ref, wb_ref, wc_ref, gpm_ref, gpf_ref,
                    gt_ref, sc_ref, sh_ref, wr_hi_ref, wr_lo_ref, br_ref, tri_ref, ustrict_ref,
                    xm_ref, h2_ref, route_ref, stat_ref, *, tm):
    y =_dot(oa_ref[...], wa_ref[...]) + _dot(ob_ref[...], wb_ref[...]) + _dot(oc_ref[...], wc_ref[...])
    xm = x_ref[...] + gt_ref[0] * _rms(y, gpm_ref[...])
    xm_ref[...] = xm
    h2 = _rms(xm, gpf_ref[...]) * (1.0 + sc_ref[0]) + sh_ref[0]
    h2_ref[...] = h2
    hi, lo = _split_bf16(h2)
    wh = wr_hi_ref[...]
    logits = _dot(hi, wh) + _dot(lo, wh) + _dot(hi, wr_lo_ref[...]) + br_ref[...]
    lane = lax.broadcasted_iota(jnp.int32, (tm, LANE), 1)
    lane_f = lane.astype(F32)
    neg = jnp.float32(-jnp.inf)
    vals, hots = [], []
    for _ in range(TOP_K):
        m = jnp.max(logits, axis=-1, keepdims=True)
        idx = jnp.min(jnp.where(logits == m, lane_f, float(LANE)), axis=-1, keepdims=True)
        hot = lane_f == idx
        logits = jnp.where(hot, neg, logits)
        vals.append(m)
        hots.append(hot)
    ex = [jnp.exp(vk - vals[0]) for vk in vals]
    inv = 1.0 / (ex[0] + ex[1] + ex[2] + ex[3])
    sel = jnp.zeros((tm, LANE), F32)
    for hot in hots:
        sel = jnp.where(hot, 1.0, sel)
    selb = sel.astype(BF16)
    tile_cnt = jnp.sum(sel, axis=0, keepdims=True)
    groups8 = jnp.floor((tile_cnt + (SUBLANE - 1.0)) * (1.0 / SUBLANE))
    g8b = jnp.broadcast_to(groups8, (SUBLANE, LANE)).astype(BF16)
    loc_start = _dot(g8b, ustrict_ref[...])[0:1] * float(SUBLANE)
    local = _dot(tri_ref[...], selb) + loc_start
    route = jnp.zeros((tm, LANE), F32)
    for kk in range(TOP_K):
        p_k = jnp.sum(jnp.where(hots[kk], local, 0.0), axis=-1, keepdims=True)
        route = jnp.where(lane == kk, ex[kk] * inv, route)
        route = jnp.where(lane == TOP_K + kk, p_k, route)
    route_ref[...] = route
    srow = lax.broadcasted_iota(jnp.int32, (SUBLANE, LANE), 0)
    stat_ref[...] = jnp.where(srow == 0, groups8 * float(SUBLANE), jnp.where(srow == 1, loc_start, 0.0))


def _postmix(x2, oa, ob, oc, wo_a, wo_b, wo_c, gpm, gpf, gt, sc, sh, wr_hi, wr_lo, br, seq, tm):
    n, d = x2.shape
    per_b = seq // tm
    tok = lambda i: (i, 0)
    bat = lambda i: (i // per_b, 0, 0)
    const = lambda i: (0, 0)
    tri = jnp.asarray(np.tril(np.ones((tm, tm), np.float32), -1), BF16)
    ustrict = jnp.asarray(np.triu(np.ones((LANE, LANE), np.float32), 1), BF16)
    full = lambda a: pl.BlockSpec(a.shape, const)
    nt = n // tm
    return pl.pallas_call(
        functools.partial(_postmix_kernel, tm=tm),
        grid=(nt,),
        in_specs=[pl.BlockSpec((tm, d), tok), pl.BlockSpec((tm, W_A), tok),
                  pl.BlockSpec((tm, HP_B), tok), pl.BlockSpec((tm, W_C), tok),
                  full(wo_a), full(wo_b), full(wo_c),
                  pl.BlockSpec((1, d), const), pl.BlockSpec((1, d), const),
                  pl.BlockSpec((1, 1, d), bat), pl.BlockSpec((1, 1, d), bat), pl.BlockSpec((1, 1, d), bat),
                  full(wr_hi), full(wr_lo), pl.BlockSpec((1, LANE), const), full(tri), full(ustrict)],
        out_specs=[pl.BlockSpec((tm, d), tok), pl.BlockSpec((tm, d), tok),
                   pl.BlockSpec((tm, LANE), tok), pl.BlockSpec((SUBLANE, LANE), tok)],
        out_shape=[jax.ShapeDtypeStruct((n, d), F32), jax.ShapeDtypeStruct((n, d), F32),
                   jax.ShapeDtypeStruct((n, LANE), F32), jax.ShapeDtypeStruct((nt * SUBLANE, LANE), F32)],
        compiler_params=_cparams(("arbitrary",)),
        name="postmix_router",
    )(x2, oa, ob, oc, wo_a, wo_b, wo_c, gpm.reshape(1, d), gpf.reshape(1, d), gt, sc, sh,
      wr_hi, wr_lo, br, tri, ustrict)


def _run_copies(tab_ref, tm, make_copy, wait, tile=0):
    sizes = [1 << b for b in range(int(math.log2(tm)), int(math.log2(SUBLANE)) - 1, -1)]
    big = [sz for sz in sizes if sz >= RUN_BIG_ROWS]
    small = [sz for sz in sizes if sz < RUN_BIG_ROWS]

    def per_expert(e, carry):
        n = tab_ref[tile, 0, e]
        loc = tab_ref[tile, 0, N_EXPERTS + e]
        dst = tab_ref[tile, 0, 2 * N_EXPERTS + e]

        def pieces(szs):
            for sz in szs:
                done = n & ~(2 * sz - 1)

                @pl.when((n & sz) != 0)
                def _():
                    cp = make_copy(pl.multiple_of(loc + done, SUBLANE), pl.multiple_of(dst + done, SUBLANE), sz)
                    if wait:
                        cp.wait()
                    else:
                        cp.start()

        if big:
            pl.when(n >= RUN_BIG_ROWS)(lambda: pieces(big))
        pieces(small)
        return carry

    lax.fori_loop(0, N_EXPERTS, per_expert, 0)


def _wait_rows(total, make_copy, max_rows):
    top = int(math.log2(max_rows))
    for sz in [1 << b for b in range(top, int(math.log2(SUBLANE)) - 1, -1)]:
        @pl.when((total & sz) != 0)
        def _():
            make_copy(0, 0, sz).wait()


def _local_rows(tm):
    return tm * TOP_K + N_EXPERTS * SUBLANE


def _slot_onehot(route_ref, tm):
    nloc = _local_rows(tm)
    pos = lax.broadcasted_iota(jnp.int32, (tm, nloc), 1).astype(F32)
    route = route_ref[...]

    def build(values):
        m = jnp.zeros((tm, nloc), F32)
        for kk in range(TOP_K):
            m = jnp.where(route[:, TOP_K + kk:TOP_K + kk + 1] == pos, values[kk], m)
        return m

    return route, build


def _dispatch_kernel(tab_ref, tabp_ref, tail_ref, route_ref, h_ref, *rest, tm, first, later_tile):
    i = pl.program_id(0)
    slot = i % 2
    if first:
        later_ref, xs_ref, sorted_ref, sem, zero_ref = rest
        bm = MOE_BLOCK_ROWS

        @pl.when(i == 0)
        def _():
            zero_ref[...] = jnp.zeros_like(zero_ref)
            fill = lambda loc, dst, sz: pltpu.make_async_copy(zero_ref.at[pl.ds(0, sz)],
                                                              xs_ref.at[pl.ds(dst, sz)], sem.at[0])
            n_later, later_tm = later_ref.shape[0], later_tile

            def block_fill(wait):
                def body(g, carry):
                    cp = fill(0, pl.multiple_of(g * bm, bm), bm)
                    cp.wait() if wait else cp.start()
                    return carry
                lax.fori_loop(tail_ref[0, 0, 3 * N_EXPERTS], xs_ref.shape[0] // bm, body, 0)

            for wait in (False, True):
                _run_copies(tail_ref, bm, fill, wait)
                block_fill(wait)
                for ti in range(n_later):
                    _run_copies(later_ref, later_tm, fill, wait, tile=ti)
    else:
        _, xs_ref, sorted_ref, sem = rest
    _, build = _slot_onehot(route_ref, tm)
    perm = build([1.0] * TOP_K).astype(BF16)
    sorted_ref[slot] = lax.dot_general(perm, h_ref[...].astype(BF16), (((0,), (0,)), ((), ())),
                                       preferred_element_type=F32)

    def runs(s):
        return lambda loc, dst, sz: pltpu.make_async_copy(sorted_ref.at[s, pl.ds(loc, sz)],
                                                          xs_ref.at[pl.ds(dst, sz)], sem.at[s])

    _run_copies(tab_ref, tm, runs(slot), wait=False)

    @pl.when(i > 0)
    def _():
        _wait_rows(tabp_ref[0, 0, 3 * N_EXPERTS], runs(1 - slot), _local_rows(tm))

    @pl.when(i == pl.num_programs(0) - 1)
    def _():
        _wait_rows(tab_ref[0, 0, 3 * N_EXPERTS], runs(slot), _local_rows(tm))


def _dispatch(tab, tail, route, h2, xs, n_pad, tm, later=None, later_tm=None):
    n, d = h2.shape
    tok = lambda i: (i, 0)
    first = xs is None
    in_specs = [pl.BlockSpec((1, 1, LANE), lambda i: (i, 0, 0), memory_space=pltpu.SMEM),
                pl.BlockSpec((1, 1, LANE), lambda i: (jnp.maximum(i - 1, 0), 0, 0), memory_space=pltpu.SMEM),
                pl.BlockSpec((1, 1, LANE), lambda i: (0, 0, 0), memory_space=pltpu.SMEM),
                pl.BlockSpec((tm, LANE), tok), pl.BlockSpec((tm, d), tok)]
    scratch = [pltpu.VMEM((2, _local_rows(tm), d), F32), pltpu.SemaphoreType.DMA((2,))]
    args = [tab, tab, tail, route, h2]
    if first:
        scratch.append(pltpu.VMEM((MOE_BLOCK_ROWS, d), F32))
        in_specs.append(pl.BlockSpec(later.shape, lambda i: (0, 0, 0), memory_space=pltpu.SMEM))
        args.append(later)
    else:
        in_specs.append(pl.BlockSpec(memory_space=pl.ANY))
        args.append(xs)
    return pl.pallas_call(
        functools.partial(_dispatch_kernel, tm=tm, first=first, later_tile=later_tm),
        grid=(n // tm,),
        in_specs=in_specs,
        out_specs=pl.BlockSpec(memory_space=pl.ANY),
        out_shape=jax.ShapeDtypeStruct((n_pad, d), F32),
        scratch_shapes=scratch,
        input_output_aliases={} if first else {5: 0},
        compiler_params=_cparams(("arbitrary",)),
        name="moe_dispatch",
    )(*args)


def _expert_kernel(be_ref, nv_ref, x_ref, wgu_ref, bgu_ref, wd_ref, bd_ref, y_ref, wgu_bf, wd_bf):
    g = pl.program_id(0)

    @pl.when(jnp.logical_or(g == 0, be_ref[g] != be_ref[jnp.maximum(g - 1, 0)]))
    def _():
        wgu_bf[...] = wgu_ref[0, 0].astype(BF16)
        wd_bf[...] = wd_ref[0, 0].astype(BF16)

    @pl.when(g < nv_ref[0])
    def _():
        gu = _dot(x_ref[...].astype(BF16), wgu_bf[...]) + bgu_ref[0, 0]
        gt = jnp.minimum(gu[:, :D_FF], SWIGLU_LIMIT)
        up = jnp.clip(gu[:, D_FF:], -SWIGLU_LIMIT, SWIGLU_LIMIT)
        act = (up + 1.0) * (gt * _sigmoid(SWIGLU_ALPHA * gt))
        y_ref[...] = _dot(act.astype(BF16), wd_bf[...]) + bd_ref[0, 0]

    @pl.when(g >= nv_ref[0])
    def _():
        y_ref[...] = jnp.zeros_like(y_ref)


def _experts(l, block_e, nvalid, xs, wgu, bgu, wd, bd):
    n_pad, d = xs.shape
    depth = wgu.shape[0]
    bm = MOE_BLOCK_ROWS
    nb = n_pad // bm
    grid_spec = pltpu.PrefetchScalarGridSpec(
        num_scalar_prefetch=2,
        grid=(nb,),
        in_specs=[pl.BlockSpec((bm, d), lambda g, be, nv: (jnp.minimum(g, nv[0] - 1), 0)),
                  pl.BlockSpec((1, 1, d, 2 * D_FF), lambda g, be, nv: (l, be[g], 0, 0)),
                  pl.BlockSpec((1, 1, 1, 2 * D_FF), lambda g, be, nv: (l, be[g], 0, 0)),
                  pl.BlockSpec((1, 1, D_FF, d), lambda g, be, nv: (l, be[g], 0, 0)),
                  pl.BlockSpec((1, 1, 1, d), lambda g, be, nv: (l, be[g], 0, 0))],
        out_specs=pl.BlockSpec((bm, d), lambda g, be, nv: (g, 0)),
        scratch_shapes=[pltpu.VMEM((d, 2 * D_FF), BF16), pltpu.VMEM((D_FF, d), BF16)],
    )
    return pl.pallas_call(
        _expert_kernel,
        grid_spec=grid_spec,
        out_shape=jax.ShapeDtypeStruct((n_pad, d), F32),
        compiler_params=_cparams(("arbitrary",)),
        name="moe_experts",
    )(block_e, nvalid, xs, wgu, bgu.reshape(depth, N_EXPERTS, 1, -1), wd, bd.reshape(depth, N_EXPERTS, 1, -1))


def _combine_kernel(tab_ref, tabn_ref, route_ref, ys_ref, xm_ref, gt_ref, g_ref, o_ref, buf_ref, sem, *, tm):
    i = pl.program_id(0)
    slot = i % 2

    def runs(s):
        return lambda loc, dst, sz: pltpu.make_async_copy(ys_ref.at[pl.ds(dst, sz)],
                                                          buf_ref.at[s, pl.ds(loc, sz)], sem.at[s])

    @pl.when(i == 0)
    def _():
        buf_ref[...] = jnp.zeros_like(buf_ref)
        _run_copies(tab_ref, tm, runs(0), wait=False)

    @pl.when(i + 1 < pl.num_programs(0))
    def _():
        _run_copies(tabn_ref, tm, runs(1 - slot), wait=False)

    route, build = _slot_onehot(route_ref, tm)
    gate_m = build([route[:, kk:kk + 1] for kk in range(TOP_K)])
    g_hi, g_lo = _split_bf16(gate_m)
    _wait_rows(tab_ref[0, 0, 3 * N_EXPERTS], runs(slot), _local_rows(tm))
    b_hi, b_lo = _split_bf16(buf_ref[slot])
    y = _dot(g_hi, b_hi) + _dot(g_lo, b_hi) + _dot(g_hi, b_lo)
    o_ref[...] = xm_ref[...] + gt_ref[0] * _rms(y, g_ref[...])


def _combine(tab, route, ys, xm, gt, g, seq, tm):
    n, d = xm.shape
    per_b = seq // tm
    nt = n // tm
    tok = lambda i: (i, 0)
    return pl.pallas_call(
        functools.partial(_combine_kernel, tm=tm),
        grid=(nt,),
        in_specs=[pl.BlockSpec((1, 1, LANE), lambda i: (i, 0, 0), memory_space=pltpu.SMEM),
                  pl.BlockSpec((1, 1, LANE), lambda i: (jnp.minimum(i + 1, nt - 1), 0, 0),
                               memory_space=pltpu.SMEM),
                  pl.BlockSpec((tm, LANE), tok), pl.BlockSpec(memory_space=pl.ANY),
                  pl.BlockSpec((tm, d), tok),
                  pl.BlockSpec((1, 1, d), lambda i: (i // per_b, 0, 0)),
                  pl.BlockSpec((1, d), lambda i: (0, 0))],
        out_specs=pl.BlockSpec((tm, d), tok),
        out_shape=jax.ShapeDtypeStruct((n, d), F32),
        scratch_shapes=[pltpu.VMEM((2, _local_rows(tm), d), F32), pltpu.SemaphoreType.DMA((2,))],
        compiler_params=_cparams(("arbitrary",)),
        name="moe_combine",
    )(tab, tab, route, ys, xm, gt, g.reshape(1, d))


def _pad_heads(w, width):
    lead = w.shape[:-1]
    w = w.reshape(lead + (H_B, width))
    w = jnp.pad(w, [(0, 0)] * len(lead) + [(0, 0), (0, LANE - width)])
    return w.reshape(lead + (HP_B,))


def _layer_weights(l, w_in, w_gla_gate, b_gla_gate, g_gla_norm, w_rg_a, w_rg_x, w_out, w_router, b_router):
    d = D_MODEL
    wi = w_in[l]
    o = 3 * W_A
    qb = wi[:, o:o + H_B * DK_B]
    kb = wi[:, o + 192:o + 384]
    vb = wi[:, o + 384:o + 768]
    gb = wi[:, o + 768:o + 1152]
    rb = wi[:, o + 1152:o + 1168]
    xy = wi[:, o + 1168:]
    w_in_p = jnp.concatenate(
        [wi[:, :o], _pad_heads(qb, DK_B), _pad_heads(kb, DK_B), _pad_heads(vb, DV_B), _pad_heads(gb, DV_B),
         jnp.pad(rb, ((0, 0), (0, LANE - GATE_RANK_B))), xy], axis=1).astype(BF16)
    wg_p = jnp.pad(_pad_heads(w_gla_gate[l], DK_B), ((0, LANE - GATE_RANK_B), (0, 0))).astype(BF16)
    bg_p = _pad_heads(b_gla_gate[l].reshape(1, -1), DK_B)
    gn_p = jnp.tile(jnp.pad(g_gla_norm[l], (0, LANE - DV_B)), H_B).reshape(1, HP_B)
    wa_bd = jax.scipy.linalg.block_diag(*[w_rg_a[l, i] for i in range(NBLK_C)]).astype(BF16)
    wx_bd = jax.scipy.linalg.block_diag(*[w_rg_x[l, i] for i in range(NBLK_C)]).astype(BF16)
    wo = w_out[l]
    wo_a = wo[:W_A].astype(BF16)
    wo_b = jnp.pad(wo[W_A:W_A + W_B].reshape(H_B, DV_B, d), ((0, 0), (0, LANE - DV_B), (0, 0)))
    wo_b = wo_b.reshape(HP_B, d).astype(BF16)
    wo_c = wo[W_A + W_B:].astype(BF16)
    wr = jnp.pad(w_router[l], ((0, 0), (0, LANE - N_EXPERTS)))
    wr_hi = wr.astype(BF16)
    wr_lo = (wr - wr_hi.astype(F32)).astype(BF16)
    br = jnp.pad(b_router[l], (0, LANE - N_EXPERTS), constant_values=-1e30).reshape(1, LANE)
    return dict(w_in_p=w_in_p, wg_p=wg_p, bg_p=bg_p, gn_p=gn_p, wa_bd=wa_bd, wx_bd=wx_bd,
                wo_a=wo_a, wo_b=wo_b, wo_c=wo_c, wr_hi=wr_hi, wr_lo=wr_lo, br=br)


def _pad_state(s):
    st = jnp.swapaxes(s, -1, -2)
    return jnp.pad(st, ((0, 0), (0, 0), (0, LANE - DV_B), (0, LANE - DK_B)))


def _unpad_state(st):
    return jnp.swapaxes(st[:, :, :DV_B, :DK_B], -1, -2)


def kernel(x_prompt, x_sample, c_prompt, c_sample, cache_k_sb, cache_v_sb, state_gla, state_conv, state_lru, w_ada, b_ada, g_pre_mix, g_post_mix, g_pre_ff, g_post_ff, w_in, w_gla_gate, b_gla_gate, g_gla_norm, w_conv, b_conv, w_rg_a, b_rg_a, w_rg_x, b_rg_x, lru_lambda, w_out, w_router, b_router, w_gate_up, b_gate_up, w_down, b_down):
    depth = w_ada.shape[0]
    d = D_MODEL
    groups = []
    for x, past in ((x_prompt, False), (x_sample, True)):
        b, t, _ = x.shape
        groups.append(dict(b=b, t=t, tm=min(TOKEN_TILE, t), n=b * t, past=past, x=x.reshape(b * t, d)))
    n_all = sum(g["n"] for g in groups)
    bm = MOE_BLOCK_ROWS
    n_tiles = sum(g["n"] // g["tm"] for g in groups)
    nb = -(-(n_all * TOP_K + n_tiles * N_EXPERTS * (SUBLANE - 1)) // bm) + N_EXPERTS
    n_pad = nb * bm

    nb_rows = sum(g["b"] for g in groups)
    c_all = jnp.concatenate([c_prompt, c_sample], axis=0)
    r_pad = -(-nb_rows // SUBLANE) * SUBLANE
    c_all = jnp.pad(c_all, ((0, r_pad - nb_rows), (0, 0)))
    mod = _modulation(c_all, w_ada, b_ada)

    st =[[[] for _ in range(5)] for _ in groups]
    for l in range(depth):
        lw = _layer_weights(l, w_in, w_gla_gate, b_gla_gate, g_gla_norm, w_rg_a, w_rg_x, w_out,
                            w_router, b_router)
        row0 = 0
        per_group = []
        for gi, g in enumerate(groups):
            b, t, tm, n = g["b"], g["t"], g["tm"], g["n"]
            m = mod[l, row0:row0 + b].reshape(b, 1, 6 * d)
            row0 += b
            sh_m, sc_m, gt_m, sh_f, sc_f, gt_f = [m[:, :, j * d:(j + 1) * d] for j in range(6)]
            ka, va, qab, kab, vab, gla_in, lru_in = _premix(g["x"], g_pre_mix[l], sc_m, sh_m, lw["w_in_p"], t, tm)
            q3 = qab.reshape(b, t, W_A)
            k3 = kab.reshape(b, t, W_A)
            v3 = vab.reshape(b, t, W_A)
            if g["past"]:
                n_past = cache_k_sb.shape[2]
                padk = ((0, 0), (0, -t % ATTN_KEY_TILE), (0, 0))
                oa = _attention(q3, jnp.pad(k3, padk), jnp.pad(v3, padk),
                                cache_k_sb.reshape(depth, b, n_past, W_A), cache_v_sb.reshape(depth, b, n_past, W_A),
                                layer=l)
                s0 = state_gla[l]
                buf0 = state_conv[l]
                h0 = state_lru[l]
            else:
                oa = _attention(q3, k3, v3)
                s0 = jnp.zeros((b, H_B, DK_B, DV_B), F32)
                buf0 = jnp.zeros((b, CONV_W - 1, W_C), F32)
                h0 = jnp.zeros((b, W_C), F32)
            oa = oa.reshape(n, W_A)
            ob, s_new = _gla(gla_in, lw["wg_p"], lw["bg_p"], lw["gn_p"], _pad_state(s0), t)
            buf0p = jnp.pad(buf0, ((0, 0), (SUBLANE - (CONV_W - 1), 0), (0, 0)))
            oc, conv_new, h_new = _lru(lru_in, w_conv[l], b_conv[l].reshape(1, -1), lw["wa_bd"],
                                       b_rg_a[l].reshape(1, -1), lw["wx_bd"], b_rg_x[l].reshape(1, -1),
                                       lru_lambda[l].reshape(1, -1), buf0p, h0.reshape(b, 1, W_C), t)
            xm, h2, route, stat = _postmix(g["x"], oa, ob, oc, lw["wo_a"], lw["wo_b"], lw["wo_c"],
                                           g_post_mix[l], g_pre_ff[l], gt_m, sc_f, sh_f,
                                           lw["wr_hi"], lw["wr_lo"], lw["br"], t, tm)
            per_group.append(dict(xm=xm, h2=h2, route=route, stat=stat, gt_f=gt_f))
            new = (ka.reshape(b, t, W_A // HD_A, HD_A), va.reshape(b, t, W_A // HD_A, HD_A),
                   _unpad_state(s_new), conv_new[:, SUBLANE - (CONV_W - 1):], h_new.reshape(b, W_C))
            for j in range(5):
                st[gi][j].append(new[j])

        stat = jnp.concatenate([p["stat"].reshape(-1, SUBLANE, LANE)[:, :2, :N_EXPERTS] for p in per_group],
                               axis=0).astype(jnp.int32)
        run_rows, run_loc = stat[:, 0], stat[:, 1]
        before = jnp.cumsum(run_rows, axis=0) - run_rows
        counts = jnp.sum(run_rows, axis=0)
        padded = (counts + bm - 1) // bm * bm
        pad_end = jnp.cumsum(padded)
        pad_start = pad_end - padded
        block_start = jnp.arange(nb, dtype=jnp.int32) * bm
        block_e = jnp.minimum(jnp.sum((pad_end[None, :] <= block_start[:, None]).astype(jnp.int32), axis=1),
                              N_EXPERTS - 1)
        nvalid = (pad_end[-1] // bm).astype(jnp.int32).reshape(1)
        tile_rows = jnp.broadcast_to(jnp.sum(run_rows, axis=1, keepdims=True), run_rows.shape)
        tab = jnp.concatenate([run_rows, run_loc, pad_start[None, :] + before, tile_rows],
                              axis=1).reshape(-1, 1, LANE)
        tile0 = 0
        for gi, g in enumerate(groups):
            nt = g["n"] // g["tm"]
            per_group[gi]["tab"] = tab[tile0:tile0 + nt]
            tile0 += nt
        zeros32 = jnp.zeros_like(counts)
        tail = jnp.concatenate([padded - counts, zeros32, pad_start + counts,
                                jnp.broadcast_to(nvalid, (N_EXPERTS,))]).reshape(1, 1, LANE)
        first, second = per_group
        xs = _dispatch(first["tab"], tail, first["route"], first["h2"], None, n_pad, groups[0]["tm"],
                       later=second["tab"], later_tm=groups[1]["tm"])
        xs = _dispatch(second["tab"], tail, second["route"], second["h2"], xs, n_pad, groups[1]["tm"])
        ys = _experts(l, block_e, nvalid, xs, w_gate_up, b_gate_up, w_down, b_down)
        for gi, g in enumerate(groups):
            p = per_group[gi]
            g["x"] = _combine(p["tab"], p["route"], ys, p["xm"], p["gt_f"], g_post_ff[l], g["t"], g["tm"])

    outs = [g["x"].reshape(g["b"], g["t"], d) for g in groups]
    for gi in range(len(groups)):
        outs.extend(jnp.stack(st[gi][j]) for j in range(5))
    return tuple(outs)
```

```python
import functools
import math

import numpy as np
import jax
import jax.numpy as jnp
from jax import lax
from jax.experimental import pallas as pl
from jax.experimental.pallas import tpu as pltpu

F32 = jnp.float32
BF16 = jnp.bfloat16

D_MODEL = 1024
EPS = 1e-6
HD_A = 64
W_A = 384
H_B = 4
DK_B = 48
DV_B = 96
W_B = 384
GATE_RANK_B = 16
GATE_TEMP_B = 16.0
W_C = 256
NBLK_C = 4
BW_C = 64
CONV_W = 4
RG_C = 8.0
N_EXPERTS = 32
TOP_K = 4
D_FF = 1024
SWIGLU_ALPHA = 1.702
SWIGLU_LIMIT = 7.0
LOG_WEIGHT_FLOOR = -120.0

LANE = 128
SUBLANE = 8
VMEM_LIMIT = 56 * 1024 * 1024

HP_B = H_B * LANE
HK_B = LANE // 2
HPK_B = H_B * HK_B
OFF_GLA = 3 * W_A
GLA_W = 2 * HPK_B + 2 * HP_B + LANE
OFF_LRU = OFF_GLA + GLA_W
IN_PAD = OFF_LRU + 2 * W_C

TOKEN_TILE = 256
ATTN_TILE = 256
ATTN_KEY_TILE = 256
ATTN_SUBTILES = 8
GLA_CHUNK = 128
GLA_CHUNKS_PER_STEP = 2
LRU_CHUNK = 256
MOE_BLOCK_ROWS = 512
RUN_BIG_ROWS = 64


def _cparams(sem):
    return pltpu.CompilerParams(dimension_semantics=sem, vmem_limit_bytes=VMEM_LIMIT)


def _sigmoid(x):
    return 1.0 / (1.0 + jnp.exp(-x))


def _log_sigmoid(x):
    return jnp.minimum(x, 0.0) - jnp.log(1.0 + jnp.exp(-jnp.abs(x)))


def _split_bf16(x):
    hi = x.astype(BF16)
    lo = (x - hi.astype(F32)).astype(BF16)
    return hi, lo


def _dot(a, b):
    return jnp.dot(a, b, preferred_element_type=F32)


def _dot_nt(a, b):
    return lax.dot_general(a, b, (((1,), (1,)), ((), ())), preferred_element_type=F32)


def _rms(x, g):
    return x * lax.rsqrt(jnp.mean(x * x, axis=-1, keepdims=True) + EPS) * g


def _mod_kernel(c_ref, w_ref, b_ref, o_ref):
    c = c_ref[...]
    a = (c * _sigmoid(c)).astype(BF16)
    o_ref[0] = _dot(a, w_ref[0].astype(BF16)) + b_ref[0]


def _modulation(c_all, w_ada, b_ada):
    depth, d, n = w_ada.shape
    r = c_all.shape[0]
    tn = 1536
    return pl.pallas_call(
        _mod_kernel,
        grid=(depth, n // tn),
        in_specs=[pl.BlockSpec((r, d), lambda l, j: (0, 0)),
                  pl.BlockSpec((1, d, tn), lambda l, j: (l, 0, j)),
                  pl.BlockSpec((1, 1, tn), lambda l, j: (l, 0, j))],
        out_specs=pl.BlockSpec((1, r, tn), lambda l, j: (l, 0, j)),
        out_shape=jax.ShapeDtypeStruct((depth, r, n), F32),
        compiler_params=_cparams(("arbitrary", "arbitrary")),
        name="adaln_mod",
    )(c_all, w_ada, b_ada.reshape(depth, 1, n))


def _premix_kernel(x_ref, g_ref, sc_ref, sh_ref, w_ref,
                   ka_ref, va_ref, qab_ref, kab_ref, vab_ref, gla_ref, lru_ref):
    h = _rms(x_ref[...], g_ref[...]) * (1.0 + sc_ref[0]) + sh_ref[0]
    r = _dot(h.astype(BF16), w_ref[...])
    ka = r[:, W_A:2 * W_A]
    va = r[:, 2 * W_A:3 * W_A]
    ka_ref[...] = ka
    va_ref[...] = va
    qab_ref[...] = (r[:, 0:W_A] * (HD_A ** -0.5)).astype(BF16)
    kab_ref[...] = ka.astype(BF16)
    vab_ref[...] = va.astype(BF16)
    gla_ref[...] = r[:, OFF_GLA:OFF_LRU]
    lru_ref[...] = r[:, OFF_LRU:IN_PAD]


def _premix(x2, g, sc, sh, w_in_p, seq, tm):
    n, d = x2.shape
    per_b = seq // tm
    tok = lambda i: (i, 0)
    bat = lambda i: (i // per_b, 0, 0)
    const = lambda i: (0, 0)
    outs = [((n, W_A), F32), ((n, W_A), F32), ((n, W_A), BF16), ((n, W_A), BF16), ((n, W_A), BF16),
            ((n, GLA_W), F32), ((n, 2 * W_C), F32)]
    return pl.pallas_call(
        _premix_kernel,
        grid=(n // tm,),
        in_specs=[pl.BlockSpec((tm, d), tok), pl.BlockSpec((1, d), const),
                  pl.BlockSpec((1, 1, d), bat), pl.BlockSpec((1, 1, d), bat),
                  pl.BlockSpec((d, IN_PAD), const)],
        out_specs=[pl.BlockSpec((tm, s[1]), tok) for s, _ in outs],
        out_shape=[jax.ShapeDtypeStruct(s, t) for s, t in outs],
        compiler_params=_cparams(("arbitrary",)),
        name="premix_proj",
    )(x2, g.reshape(1, d), sc, sh, w_in_p)


def _attn_kernel(q_ref, k_ref, v_ref, *rest, tq, tk, off, has_past, nsub):
    if has_past:
        kp_ref, vp_ref, u_ref, o_ref, acc_ref, c_ref = rest
    else:
        u_ref, o_ref, acc_ref, c_ref = rest
    i = pl.program_id(2)
    lane = lax.broadcasted_iota(jnp.int32, (1, LANE), 1)
    head_lanes = (lane < HD_A, lane >= HD_A)
    acc_ref[...] = jnp.zeros_like(acc_ref)
    c_ref[...] = jnp.zeros_like(c_ref)
    row = lax.broadcasted_iota(jnp.int32, (tq, tk), 0)
    col = lax.broadcasted_iota(jnp.int32, (tq, tk), 1)
    u = u_ref[...]
    qh, qpos0, nk = [], [], []
    for a in range(nsub):
        q = q_ref[0, a * tq:(a + 1) * tq, :]
        qz = jnp.zeros_like(q)
        qh.append(tuple(jnp.where(m, q, qz) for m in head_lanes))
        qpos0.append(off + (i * nsub + a) * tq)
        nk.append((qpos0[a] + tq - 1 + tk - 1) // tk)

    def step(jj, masked):
        cmax = jnp.float32(-jnp.inf)
        for a in range(nsub):
            j = nk[a] - 1 - jj
            live = j >= 0
            ks = pl.multiple_of(jnp.maximum(j, 0) * tk, tk)
            if has_past and not masked:
                kb = kp_ref[0, 0, pl.ds(ks, tk), :].astype(BF16)
                vb = vp_ref[0, 0, pl.ds(ks, tk), :].astype(BF16)
            else:
                kb = k_ref[0, pl.ds(pl.multiple_of(ks - off, tk), tk), :]
                vb = v_ref[0, pl.ds(pl.multiple_of(ks - off, tk), tk), :]
            vz = jnp.zeros_like(vb)
            mask = (col + ks) < (row + qpos0[a])
            for h in range(2):
                s = _dot_nt(qh[a][h], kb)
                lf = -(jnp.maximum(s, 0.0) + jnp.log(1.0 + jnp.exp(-jnp.abs(s))))
                lb = s + lf
                if masked:
                    lf = jnp.where(mask, lf, 0.0)
                hi, lo = _split_bf16(lf)
                cr = _dot(jnp.concatenate([hi, lo], axis=1), u)
                c = c_ref[a, h]
                cfull = jnp.concatenate([c] * (tk // LANE), axis=1)
                w = jnp.exp(lb + cr[:, :tk] + cfull)
                if masked:
                    w = jnp.where(mask, w, 0.0)
                vh = jnp.where(head_lanes[h], vb, vz)
                pv = _dot(w.astype(BF16), vh)
                rs = cr[:, tk:]
                if not masked:
                    pv = jnp.where(live, pv, 0.0)
                    rs = jnp.where(live, rs, 0.0)
                acc_ref[a] += pv
                c_ref[a, h] = c + rs
            cmax = jnp.maximum(cmax, jnp.where(j > 0, jnp.max(c_ref[a]), -jnp.inf))
        return cmax > LOG_WEIGHT_FLOOR

    n_masked = max(1, tq // tk)
    for jj in range(n_masked):
        go = step(jj, True)

    def body(carry):
        jj, _ = carry
        return jj + 1, step(jj, False)

    lax.while_loop(lambda carry: jnp.logical_and(carry[0] < nk[-1], carry[1]), body, (jnp.int32(n_masked), go))
    for a in range(nsub):
        o_ref[0, a * tq:(a + 1) * tq, :] = acc_ref[a].astype(o_ref.dtype)


def _attn_umat(tk):
    jp = np.arange(tk)[:, None]
    j = np.arange(tk)[None, :]
    u = np.concatenate([(jp > j).astype(np.float32), np.ones((tk, LANE), np.float32)], axis=1)
    return jnp.asarray(np.concatenate([u, u], axis=0), BF16)


def _attention(q, k, v, kp=None, vp=None, layer=0):
    b, tq_all, _ = q.shape
    tc = k.shape[1]
    has_past = kp is not None
    off = kp.shape[2] if has_past else 0
    tq = min(ATTN_TILE, tq_all)
    tk = ATTN_KEY_TILE
    assert tc % tk == 0 and tq_all % tq == 0 and tq_all <= tc
    assert off % tk == 0 and (tk % tq == 0 or tq % tk == 0)
    assert not has_past or (tq == tq_all and tq_all <= tk)
    nsub = math.gcd(ATTN_SUBTILES, tq_all // tq)
    kern = functools.partial(_attn_kernel, tq=tq, tk=tk, off=off, has_past=has_past, nsub=nsub)
    cur = pl.BlockSpec((1, tc, LANE), lambda bi, hp, i: (bi, 0, hp))
    in_specs = [pl.BlockSpec((1, nsub * tq, LANE), lambda bi, hp, i: (bi, i, hp)), cur, cur]
    args = [q, k, v]
    if has_past:
        past = pl.BlockSpec((1, 1, off, LANE), lambda bi, hp, i: (layer, bi, 0, hp))
        in_specs += [past, past]
        args += [kp, vp]
    in_specs.append(pl.BlockSpec((2 * tk, tk + LANE), lambda bi, hp, i: (0, 0)))
    args.append(_attn_umat(tk))
    return pl.pallas_call(
        kern,
        grid=(b, W_A // LANE, tq_all // (nsub * tq)),
        in_specs=in_specs,
        out_specs=pl.BlockSpec((1, nsub * tq, LANE), lambda bi, hp, i: (bi, i, hp)),
        out_shape=jax.ShapeDtypeStruct((b, tq_all, W_A), BF16),
        scratch_shapes=[pltpu.VMEM((nsub, tq, LANE), F32), pltpu.VMEM((nsub, 2, tq, LANE), F32)],
        compiler_params=_cparams(("arbitrary", "arbitrary", "arbitrary")),
        name="stick_breaking_attn",
    )(*args)


def _gla_levels(c):
    return int(math.log2(c))


def _gla_mats(c):
    t = np.arange(c)[:, None]
    s = np.arange(c)[None, :]
    mats = [(s <= t).astype(np.float32), (s > t).astype(np.float32)]
    for lv in range(_gla_levels(c)):
        m = 1 << lv
        ref = (t // (2 * m)) * (2 * m) + m - 1
        mats.append(((s > ref) & (s <= t)).astype(np.float32) - ((s > t) & (s <= ref)).astype(np.float32))
    m = np.concatenate(mats, axis=0)
    return jnp.asarray(np.concatenate([m, m], axis=1), BF16)


def _gla_kernel(in_ref, wg_ref, bg_ref, gn_ref, mall_ref, s0_ref, ob_ref, sout_ref, st_ref, *, C, G):
    ci = pl.program_id(1)

    @pl.when(ci == 0)
    def _():
        st_ref[...] = s0_ref[0]

    for sub in range(G):
        _gla_chunk(in_ref, wg_ref, bg_ref, gn_ref, mall_ref, ob_ref, st_ref, C, sub * C)

    @pl.when(ci == pl.num_programs(1) - 1)
    def _():
        sout_ref[0] = st_ref[...]


def _gla_chunk(in_ref, wg_ref, bg_ref, gn_ref, mall_ref, ob_ref, st_ref, C, r0):
    rows = slice(r0, r0 + C)
    q = in_ref[rows, 0:HPK_B] * (DK_B ** -0.5)
    k = in_ref[rows, HPK_B:2 * HPK_B]
    v = in_ref[rows, 2 * HPK_B:2 * HPK_B + HP_B]
    gb = in_ref[rows, 2 * HPK_B + HP_B:2 * HPK_B + 2 * HP_B]
    rb = in_ref[rows, 2 * HPK_B + 2 * HP_B:GLA_W]
    lg = _log_sigmoid(_dot(rb.astype(BF16), wg_ref[...]) + bg_ref[...]) * (1.0 / GATE_TEMP_B)
    hi, lo = _split_bf16(lg)
    dall = _dot(mall_ref[...], jnp.concatenate([hi, lo], axis=0))
    eb = jnp.exp(dall[0:C])
    elast = jnp.exp(dall[C:2 * C])
    eb_last = eb[C - 1:C, :]
    rowi = lax.broadcasted_iota(jnp.int32, (C, 1), 0)
    row = lax.broadcasted_iota(jnp.int32, (C, C), 0)
    col = lax.broadcasted_iota(jnp.int32, (C, C), 1)
    lane = lax.broadcasted_iota(jnp.int32, (1, LANE), 1)
    slot_half = (lane < HK_B, lane >= HK_B)
    qe = q * eb
    ke = k * elast
    kb = k.astype(BF16)
    vb = v.astype(BF16)
    zero = jnp.zeros_like(q)
    lv_q, lv_k = [], []
    for lv in range(_gla_levels(C)):
        e = jnp.exp(-jnp.abs(dall[(2 + lv) * C:(3 + lv) * C]))
        second = ((rowi >> lv) & 1) == 1
        lv_q.append(jnp.where(second, q * e, zero))
        lv_k.append(jnp.where(second, zero, k * e).astype(BF16))
    outs = []
    for h in range(H_B):
        sl = slice(h * LANE, (h + 1) * LANE)
        sk = slice((h // 2) * LANE, (h // 2 + 1) * LANE)

        def own(x):
            return jnp.where(slot_half[h % 2], x[:, sk], 0.0).astype(BF16)

        att = jnp.where(row == col, _dot_nt(own(q), kb[:, sk]), 0.0)
        for lv in range(_gla_levels(C)):
            same = (row >> (lv + 1)) == (col >> (lv + 1))
            att = att + jnp.where(same, _dot_nt(own(lv_q[lv]), lv_k[lv][:, sk]), 0.0)
        st = st_ref[h]
        o = _dot(att.astype(BF16), vb[:, sl]) + _dot_nt(own(qe), st.astype(BF16))
        st_ref[h] = st * eb_last[:, sk] + _dot(v[:, sl].T.astype(BF16), own(ke))
        ms = jnp.sum(o * o, axis=-1, keepdims=True) * (1.0 / DV_B)
        on = o * lax.rsqrt(ms + EPS) * gn_ref[:, sl]
        g = gb[:, sl]
        outs.append((on * (g * _sigmoid(g))).astype(BF16))
    ob_ref[rows, :] = jnp.concatenate(outs, axis=1)


def _gla(gla_in, wg_p, bg_p, gn_p, s0t, seq):
    n = gla_in.shape[0]
    b = n // seq
    c = min(GLA_CHUNK, seq)
    g = math.gcd(GLA_CHUNKS_PER_STEP, seq // c)
    per_b = seq // (c * g)
    mall = _gla_mats(c)
    const2 = lambda bi, ci: (0, 0)
    return pl.pallas_call(
        functools.partial(_gla_kernel, C=c, G=g),
        grid=(b, per_b),
        in_specs=[pl.BlockSpec((c * g, GLA_W), lambda bi, ci: (bi * per_b + ci, 0)),
                  pl.BlockSpec((LANE, HPK_B), const2), pl.BlockSpec((1, HPK_B), const2),
                  pl.BlockSpec((1, HP_B), const2), pl.BlockSpec(mall.shape, const2),
                  pl.BlockSpec((1, H_B, LANE, LANE), lambda bi, ci: (bi, 0, 0, 0))],
        out_specs=[pl.BlockSpec((c * g, HP_B), lambda bi, ci: (bi * per_b + ci, 0)),
                   pl.BlockSpec((1, H_B, LANE, LANE), lambda bi, ci: (bi, 0, 0, 0))],
        out_shape=[jax.ShapeDtypeStruct((n, HP_B), BF16),
                   jax.ShapeDtypeStruct((b, H_B, LANE, LANE), F32)],
        scratch_shapes=[pltpu.VMEM((H_B, LANE, LANE), F32)],
        compiler_params=_cparams(("arbitrary", "arbitrary")),
        name="gla_chunked",
    )(gla_in, wg_p, bg_p, gn_p, mall, s0t)


def _lru_kernel(in_ref, cw_ref, cb_ref, wa_ref, ba_ref, wx_ref, bx_ref, lam_ref, buf0_ref, h0_ref,
                oc_ref, conv_ref, hout_ref, xp_ref, hc_ref, *, C):
    ci = pl.program_id(1)

    @pl.when(ci == 0)
    def _():
        xp_ref[0:SUBLANE] = buf0_ref[0]
        hc_ref[...] = h0_ref[0]

    x = in_ref[:, 0:W_C]
    y = in_ref[:, W_C:2 * W_C]
    xp_ref[SUBLANE:SUBLANE + C] = x
    xc = cb_ref[...]
    for j in range(CONV_W):
        xc = xc + xp_ref[pl.ds(SUBLANE - (CONV_W - 1) + j, C), :] * cw_ref[j:j + 1, :]
    xcb = xc.astype(BF16)
    r = _sigmoid(_dot(xcb, wa_ref[...]) + ba_ref[...])
    gi = _sigmoid(_dot(xcb, wx_ref[...]) + bx_ref[...])
    log_a = RG_C * r * _log_sigmoid(lam_ref[...])
    a = jnp.exp(log_a)
    u = jnp.sqrt(1.0 - jnp.exp(2.0 * log_a)) * (gi * xc)
    rowi = lax.broadcasted_iota(jnp.int32, (C, 1), 0)
    d = 1
    while d < C:
        keep = rowi >= d
        a_s = pltpu.roll(a, d, axis=0)
        u_s = pltpu.roll(u, d, axis=0)
        u = jnp.where(keep, a * u_s + u, u)
        a = jnp.where(keep, a * a_s, a)
        d *= 2
    hseq = u + a * hc_ref[...]
    hc_ref[...] = hseq[C - 1:C, :]
    gelu = 0.5 * y * (1.0 + jnp.tanh(math.sqrt(2.0 / math.pi) * (y + 0.044715 * (y * y * y))))
    oc_ref[...] = (hseq * gelu).astype(BF16)
    tail = xp_ref[C:C + SUBLANE]
    xp_ref[0:SUBLANE] = tail

    @pl.when(ci == pl.num_programs(1) - 1)
    def _():
        conv_ref[0] = tail
        hout_ref[0] = hseq[C - 1:C, :]


def _lru(lru_in, cw, cb, wa_bd, ba, wx_bd, bx, lam, buf0, h0, seq):
    n = lru_in.shape[0]
    b = n // seq
    c = min(LRU_CHUNK, seq)
    per_b = seq // c
    const2 = lambda bi, ci: (0, 0)
    vec = pl.BlockSpec((1, W_C), const2)
    return pl.pallas_call(
        functools.partial(_lru_kernel, C=c),
        grid=(b, per_b),
        in_specs=[pl.BlockSpec((c, 2 * W_C), lambda bi, ci: (bi * per_b + ci, 0)),
                  pl.BlockSpec((CONV_W, W_C), const2), vec,
                  pl.BlockSpec((W_C, W_C), const2), vec, pl.BlockSpec((W_C, W_C), const2), vec, vec,
                  pl.BlockSpec((1, SUBLANE, W_C), lambda bi, ci: (bi, 0, 0)),
                  pl.BlockSpec((1, 1, W_C), lambda bi, ci: (bi, 0, 0))],
        out_specs=[pl.BlockSpec((c, W_C), lambda bi, ci: (bi * per_b + ci, 0)),
                   pl.BlockSpec((1, SUBLANE, W_C), lambda bi, ci: (bi, 0, 0)),
                   pl.BlockSpec((1, 1, W_C), lambda bi, ci: (bi, 0, 0))],
        out_shape=[jax.ShapeDtypeStruct((n, W_C), BF16),
                   jax.ShapeDtypeStruct((b, SUBLANE, W_C), F32),
                   jax.ShapeDtypeStruct((b, 1, W_C), F32)],
        scratch_shapes=[pltpu.VMEM((c + SUBLANE, W_C), F32), pltpu.VMEM((1, W_C), F32)],
        compiler_params=_cparams(("arbitrary", "arbitrary")),
        name="conv_rglru",
    )(lru_in, cw, cb, wa_bd, ba, wx_bd, bx, lam, buf0, h0)


def _postmix_kernel(x_ref, oa_ref, ob_ref, oc_ref, wa_ref, wb_ref, wc_ref, gpm_ref, gpf_ref,
                    gt_ref, sc_ref, sh_ref, wr_hi_ref, wr_lo_ref, br_ref, tri_ref, ustrict_ref,
                    xm_ref, h2_ref, route_ref, stat_ref, *, tm):
    y =_dot(oa_ref[...], wa_ref[...]) + _dot(ob_ref[...], wb_ref[...]) + _dot(oc_ref[...], wc_ref[...])
    xm = x_ref[...] + gt_ref[0] * _rms(y, gpm_ref[...])
    xm_ref[...] = xm
    h2 = _rms(xm, gpf_ref[...]) * (1.0 + sc_ref[0]) + sh_ref[0]
    h2_ref[...] = h2
    hi, lo = _split_bf16(h2)
    wh = wr_hi_ref[...]
    p_hi = _dot(hi, jnp.concatenate([wh, wr_lo_ref[...]], axis=1))
    logits = p_hi[:, :LANE] + _dot(lo, wh) + p_hi[:, LANE:] + br_ref[...]
    lane = lax.broadcasted_iota(jnp.int32, (tm, LANE), 1)
    lane_f = lane.astype(F32)
    neg = jnp.float32(-jnp.inf)
    vals, hots = [], []
    for _ in range(TOP_K):
        m = jnp.max(logits, axis=-1, keepdims=True)
        idx = jnp.min(jnp.where(logits == m, lane_f, float(LANE)), axis=-1, keepdims=True)
        hot = lane_f == idx
        logits = jnp.where(hot, neg, logits)
        vals.append(m)
        hots.append(hot)
    ex = [jnp.exp(vk - vals[0]) for vk in vals]
    inv = 1.0 / (ex[0] + ex[1] + ex[2] + ex[3])
    sel = jnp.zeros((tm, LANE), F32)
    for hot in hots:
        sel = jnp.where(hot, 1.0, sel)
    selb = sel.astype(BF16)
    tile_cnt = jnp.sum(sel, axis=0, keepdims=True)
    groups8 = jnp.floor((tile_cnt + (SUBLANE - 1.0)) * (1.0 / SUBLANE))
    g8b = jnp.broadcast_to(groups8, (SUBLANE, LANE)).astype(BF16)
    loc_start = _dot(g8b, ustrict_ref[...])[0:1] * float(SUBLANE)
    local = _dot(tri_ref[...], selb) + loc_start
    route = jnp.zeros((tm, LANE), F32)
    for kk in range(TOP_K):
        p_k = jnp.sum(jnp.where(hots[kk], local, 0.0), axis=-1, keepdims=True)
        route = jnp.where(lane == kk, ex[kk] * inv, route)
        route = jnp.where(lane == TOP_K + kk, p_k, route)
    route_ref[...] = route
    srow = lax.broadcasted_iota(jnp.int32, (SUBLANE, LANE), 0)
    stat_ref[...] = jnp.where(srow == 0, groups8 * float(SUBLANE), jnp.where(srow == 1, loc_start, 0.0))


def _postmix(x2, oa, ob, oc, wo_a, wo_b, wo_c, gpm, gpf, gt, sc, sh, wr_hi, wr_lo, br, seq, tm):
    n, d = x2.shape
    per_b = seq // tm
    tok = lambda i: (i, 0)
    bat = lambda i: (i // per_b, 0, 0)
    const = lambda i: (0, 0)
    tri = jnp.asarray(np.tril(np.ones((tm, tm), np.float32), -1), BF16)
    ustrict = jnp.asarray(np.triu(np.ones((LANE, LANE), np.float32), 1), BF16)
    full = lambda a: pl.BlockSpec(a.shape, const)
    nt = n // tm
    return pl.pallas_call(
        functools.partial(_postmix_kernel, tm=tm),
        grid=(nt,),
        in_specs=[pl.BlockSpec((tm, d), tok), pl.BlockSpec((tm, W_A), tok),
                  pl.BlockSpec((tm, HP_B), tok), pl.BlockSpec((tm, W_C), tok),
                  full(wo_a), full(wo_b), full(wo_c),
                  pl.BlockSpec((1, d), const), pl.BlockSpec((1, d), const),
                  pl.BlockSpec((1, 1, d), bat), pl.BlockSpec((1, 1, d), bat), pl.BlockSpec((1, 1, d), bat),
                  full(wr_hi), full(wr_lo), pl.BlockSpec((1, LANE), const), full(tri), full(ustrict)],
        out_specs=[pl.BlockSpec((tm, d), tok), pl.BlockSpec((tm, d), tok),
                   pl.BlockSpec((tm, LANE), tok), pl.BlockSpec((SUBLANE, LANE), tok)],
        out_shape=[jax.ShapeDtypeStruct((n, d), F32), jax.ShapeDtypeStruct((n, d), F32),
                   jax.ShapeDtypeStruct((n, LANE), F32), jax.ShapeDtypeStruct((nt * SUBLANE, LANE), F32)],
        compiler_params=_cparams(("arbitrary",)),
        name="postmix_router",
    )(x2, oa, ob, oc, wo_a, wo_b, wo_c, gpm.reshape(1, d), gpf.reshape(1, d), gt, sc, sh,
      wr_hi, wr_lo, br, tri, ustrict)


def _run_copies(tab_ref, tm, make_copy, wait, tile=0):
    sizes = [1 << b for b in range(int(math.log2(tm)), int(math.log2(SUBLANE)) - 1, -1)]
    big = [sz for sz in sizes if sz >= RUN_BIG_ROWS]
    small = [sz for sz in sizes if sz < RUN_BIG_ROWS]

    def per_expert(e, carry):
        n = tab_ref[tile, 0, e]
        loc = tab_ref[tile, 0, N_EXPERTS + e]
        dst = tab_ref[tile, 0, 2 * N_EXPERTS + e]

        def pieces(szs):
            for sz in szs:
                done = n & ~(2 * sz - 1)

                @pl.when((n & sz) != 0)
                def _():
                    cp = make_copy(pl.multiple_of(loc + done, SUBLANE), pl.multiple_of(dst + done, SUBLANE), sz)
                    if wait:
                        cp.wait()
                    else:
                        cp.start()

        if big:
            pl.when(n >= RUN_BIG_ROWS)(lambda: pieces(big))
        pieces(small)
        return carry

    lax.fori_loop(0, N_EXPERTS, per_expert, 0)


def _wait_rows(total, make_copy, max_rows):
    top = int(math.log2(max_rows))
    for sz in [1 << b for b in range(top, int(math.log2(SUBLANE)) - 1, -1)]:
        @pl.when((total & sz) != 0)
        def _():
            make_copy(0, 0, sz).wait()


def _local_rows(tm):
    return tm * TOP_K + N_EXPERTS * SUBLANE


def _slot_onehot(route_ref, tm):
    nloc = _local_rows(tm)
    pos = lax.broadcasted_iota(jnp.int32, (tm, nloc), 1).astype(F32)
    route = route_ref[...]

    def build(values):
        m = jnp.zeros((tm, nloc), F32)
        for kk in range(TOP_K):
            m = jnp.where(route[:, TOP_K + kk:TOP_K + kk + 1] == pos, values[kk], m)
        return m

    return route, build


def _dispatch_kernel(tab_ref, tabp_ref, tail_ref, route_ref, h_ref, *rest, tm, first, later_tile):
    i = pl.program_id(0)
    slot = i % 2
    if first:
        later_ref, xs_ref, sorted_ref, sem, zero_ref = rest
        bm = MOE_BLOCK_ROWS

        @pl.when(i == 0)
        def _():
            zero_ref[...] = jnp.zeros_like(zero_ref)
            fill = lambda loc, dst, sz: pltpu.make_async_copy(zero_ref.at[pl.ds(0, sz)],
                                                              xs_ref.at[pl.ds(dst, sz)], sem.at[0])
            n_later, later_tm = later_ref.shape[0], later_tile

            def block_fill(wait):
                def body(g, carry):
                    cp = fill(0, pl.multiple_of(g * bm, bm), bm)
                    cp.wait() if wait else cp.start()
                    return carry
                lax.fori_loop(tail_ref[0, 0, 3 * N_EXPERTS], xs_ref.shape[0] // bm, body, 0)

            for wait in (False, True):
                _run_copies(tail_ref, bm, fill, wait)
                block_fill(wait)
                for ti in range(n_later):
                    _run_copies(later_ref, later_tm, fill, wait, tile=ti)
    else:
        _, xs_ref, sorted_ref, sem = rest
    _, build = _slot_onehot(route_ref, tm)
    perm = build([1.0] * TOP_K).astype(BF16)
    sorted_ref[slot] = lax.dot_general(perm, h_ref[...].astype(BF16), (((0,), (0,)), ((), ())),
                                       preferred_element_type=F32)

    def runs(s):
        return lambda loc, dst, sz: pltpu.make_async_copy(sorted_ref.at[s, pl.ds(loc, sz)],
                                                          xs_ref.at[pl.ds(dst, sz)], sem.at[s])

    _run_copies(tab_ref, tm, runs(slot), wait=False)

    @pl.when(i > 0)
    def _():
        _wait_rows(tabp_ref[0, 0, 3 * N_EXPERTS], runs(1 - slot), _local_rows(tm))

    @pl.when(i == pl.num_programs(0) - 1)
    def _():
        _wait_rows(tab_ref[0, 0, 3 * N_EXPERTS], runs(slot), _local_rows(tm))


def _dispatch(tab, tail, route, h2, xs, n_pad, tm, later=None, later_tm=None):
    n, d = h2.shape
    tok = lambda i: (i, 0)
    first = xs is None
    in_specs = [pl.BlockSpec((1, 1, LANE), lambda i: (i, 0, 0), memory_space=pltpu.SMEM),
                pl.BlockSpec((1, 1, LANE), lambda i: (jnp.maximum(i - 1, 0), 0, 0), memory_space=pltpu.SMEM),
                pl.BlockSpec((1, 1, LANE), lambda i: (0, 0, 0), memory_space=pltpu.SMEM),
                pl.BlockSpec((tm, LANE), tok), pl.BlockSpec((tm, d), tok)]
    scratch = [pltpu.VMEM((2, _local_rows(tm), d), F32), pltpu.SemaphoreType.DMA((2,))]
    args = [tab, tab, tail, route, h2]
    if first:
        scratch.append(pltpu.VMEM((MOE_BLOCK_ROWS, d), F32))
        in_specs.append(pl.BlockSpec(later.shape, lambda i: (0, 0, 0), memory_space=pltpu.SMEM))
        args.append(later)
    else:
        in_specs.append(pl.BlockSpec(memory_space=pl.ANY))
        args.append(xs)
    return pl.pallas_call(
        functools.partial(_dispatch_kernel, tm=tm, first=first, later_tile=later_tm),
        grid=(n // tm,),
        in_specs=in_specs,
        out_specs=pl.BlockSpec(memory_space=pl.ANY),
        out_shape=jax.ShapeDtypeStruct((n_pad, d), F32),
        scratch_shapes=scratch,
        input_output_aliases={} if first else {5: 0},
        compiler_params=_cparams(("arbitrary",)),
        name="moe_dispatch",
    )(*args)


def _expert_kernel(be_ref, nv_ref, x_ref, wgu_ref, bgu_ref, wd_ref, bd_ref, y_ref, wgu_bf, wd_bf):
    g = pl.program_id(0)

    @pl.when(jnp.logical_or(g == 0, be_ref[g] != be_ref[jnp.maximum(g - 1, 0)]))
    def _():
        wgu_bf[...] = wgu_ref[0, 0].astype(BF16)
        wd_bf[...] = wd_ref[0, 0].astype(BF16)

    @pl.when(g < nv_ref[0])
    def _():
        gu = _dot(x_ref[...].astype(BF16), wgu_bf[...]) + bgu_ref[0, 0]
        gt = jnp.minimum(gu[:, :D_FF], SWIGLU_LIMIT)
        up = jnp.clip(gu[:, D_FF:], -SWIGLU_LIMIT, SWIGLU_LIMIT)
        act = (up + 1.0) * (gt * _sigmoid(SWIGLU_ALPHA * gt))
        y_ref[...] = _dot(act.astype(BF16), wd_bf[...]) + bd_ref[0, 0]

    @pl.when(g >= nv_ref[0])
    def _():
        y_ref[...] = jnp.zeros_like(y_ref)


def _experts(l, block_e, nvalid, xs, wgu, bgu, wd, bd):
    n_pad, d = xs.shape
    depth = wgu.shape[0]
    bm = MOE_BLOCK_ROWS
    nb = n_pad // bm
    grid_spec = pltpu.PrefetchScalarGridSpec(
        num_scalar_prefetch=2,
        grid=(nb,),
        in_specs=[pl.BlockSpec((bm, d), lambda g, be, nv: (jnp.minimum(g, nv[0] - 1), 0)),
                  pl.BlockSpec((1, 1, d, 2 * D_FF), lambda g, be, nv: (l, be[g], 0, 0)),
                  pl.BlockSpec((1, 1, 1, 2 * D_FF), lambda g, be, nv: (l, be[g], 0, 0)),
                  pl.BlockSpec((1, 1, D_FF, d), lambda g, be, nv: (l, be[g], 0, 0)),
                  pl.BlockSpec((1, 1, 1, d), lambda g, be, nv: (l, be[g], 0, 0))],
        out_specs=pl.BlockSpec((bm, d), lambda g, be, nv: (g, 0)),
        scratch_shapes=[pltpu.VMEM((d, 2 * D_FF), BF16), pltpu.VMEM((D_FF, d), BF16)],
    )
    return pl.pallas_call(
        _expert_kernel,
        grid_spec=grid_spec,
        out_shape=jax.ShapeDtypeStruct((n_pad, d), F32),
        compiler_params=_cparams(("arbitrary",)),
        name="moe_experts",
    )(block_e, nvalid, xs, wgu, bgu.reshape(depth, N_EXPERTS, 1, -1), wd, bd.reshape(depth, N_EXPERTS, 1, -1))


def _combine_kernel(tab_ref, tabn_ref, route_ref, ys_ref, xm_ref, gt_ref, g_ref, o_ref, buf_ref, sem, *, tm):
    i = pl.program_id(0)
    slot = i % 2

    def runs(s):
        return lambda loc, dst, sz: pltpu.make_async_copy(ys_ref.at[pl.ds(dst, sz)],
                                                          buf_ref.at[s, pl.ds(loc, sz)], sem.at[s])

    @pl.when(i == 0)
    def _():
        buf_ref[...] = jnp.zeros_like(buf_ref)
        _run_copies(tab_ref, tm, runs(0), wait=False)

    @pl.when(i + 1 < pl.num_programs(0))
    def _():
        _run_copies(tabn_ref, tm, runs(1 - slot), wait=False)

    route, build = _slot_onehot(route_ref, tm)
    gate_m = build([route[:, kk:kk + 1] for kk in range(TOP_K)])
    g_hi, g_lo = _split_bf16(gate_m)
    _wait_rows(tab_ref[0, 0, 3 * N_EXPERTS], runs(slot), _local_rows(tm))
    b_hi, b_lo = _split_bf16(buf_ref[slot])
    y = _dot(g_hi, b_hi) + _dot(g_lo, b_hi) + _dot(g_hi, b_lo)
    o_ref[...] = xm_ref[...] + gt_ref[0] * _rms(y, g_ref[...])


def _combine(tab, route, ys, xm, gt, g, seq, tm):
    n, d = xm.shape
    per_b = seq // tm
    nt = n // tm
    tok = lambda i: (i, 0)
    return pl.pallas_call(
        functools.partial(_combine_kernel, tm=tm),
        grid=(nt,),
        in_specs=[pl.BlockSpec((1, 1, LANE), lambda i: (i, 0, 0), memory_space=pltpu.SMEM),
                  pl.BlockSpec((1, 1, LANE), lambda i: (jnp.minimum(i + 1, nt - 1), 0, 0),
                               memory_space=pltpu.SMEM),
                  pl.BlockSpec((tm, LANE), tok), pl.BlockSpec(memory_space=pl.ANY),
                  pl.BlockSpec((tm, d), tok),
                  pl.BlockSpec((1, 1, d), lambda i: (i // per_b, 0, 0)),
                  pl.BlockSpec((1, d), lambda i: (0, 0))],
        out_specs=pl.BlockSpec((tm, d), tok),
        out_shape=jax.ShapeDtypeStruct((n, d), F32),
        scratch_shapes=[pltpu.VMEM((2, _local_rows(tm), d), F32), pltpu.SemaphoreType.DMA((2,))],
        compiler_params=_cparams(("arbitrary",)),
        name="moe_combine",
    )(tab, tab, route, ys, xm, gt, g.reshape(1, d))


def _pad_heads(w, width, slot=LANE):
    lead = w.shape[:-1]
    w = w.reshape(lead + (H_B, width))
    w = jnp.pad(w, [(0, 0)] * len(lead) + [(0, 0), (0, slot - width)])
    return w.reshape(lead + (H_B * slot,))


def _layer_weights(l, w_in, w_gla_gate, b_gla_gate, g_gla_norm, w_rg_a, w_rg_x, w_out, w_router, b_router):
    d = D_MODEL
    wi = w_in[l]
    o = 3 * W_A
    qb = wi[:, o:o + H_B * DK_B]
    kb = wi[:, o + 192:o + 384]
    vb = wi[:, o + 384:o + 768]
    gb = wi[:, o + 768:o + 1152]
    rb = wi[:, o + 1152:o + 1168]
    xy = wi[:, o + 1168:]
    w_in_p = jnp.concatenate(
        [wi[:, :o], _pad_heads(qb, DK_B, HK_B), _pad_heads(kb, DK_B, HK_B), _pad_heads(vb, DV_B),
         _pad_heads(gb, DV_B), jnp.pad(rb, ((0, 0), (0, LANE - GATE_RANK_B))), xy], axis=1).astype(BF16)
    wg_p = jnp.pad(_pad_heads(w_gla_gate[l], DK_B, HK_B), ((0, LANE - GATE_RANK_B), (0, 0))).astype(BF16)
    bg_p = _pad_heads(b_gla_gate[l].reshape(1, -1), DK_B, HK_B)
    gn_p = jnp.tile(jnp.pad(g_gla_norm[l], (0, LANE - DV_B)), H_B).reshape(1, HP_B)
    wa_bd = jax.scipy.linalg.block_diag(*[w_rg_a[l, i] for i in range(NBLK_C)]).astype(BF16)
    wx_bd = jax.scipy.linalg.block_diag(*[w_rg_x[l, i] for i in range(NBLK_C)]).astype(BF16)
    wo = w_out[l]
    wo_a = wo[:W_A].astype(BF16)
    wo_b = jnp.pad(wo[W_A:W_A + W_B].reshape(H_B, DV_B, d), ((0, 0), (0, LANE - DV_B), (0, 0)))
    wo_b = wo_b.reshape(HP_B, d).astype(BF16)
    wo_c = wo[W_A + W_B:].astype(BF16)
    wr = jnp.pad(w_router[l], ((0, 0), (0, LANE - N_EXPERTS)))
    wr_hi = wr.astype(BF16)
    wr_lo = (wr - wr_hi.astype(F32)).astype(BF16)
    br = jnp.pad(b_router[l], (0, LANE - N_EXPERTS), constant_values=-1e30).reshape(1, LANE)
    return dict(w_in_p=w_in_p, wg_p=wg_p, bg_p=bg_p, gn_p=gn_p, wa_bd=wa_bd, wx_bd=wx_bd,
                wo_a=wo_a, wo_b=wo_b, wo_c=wo_c, wr_hi=wr_hi, wr_lo=wr_lo, br=br)


def _pad_state(s):
    st = jnp.swapaxes(s, -1, -2)
    st = jnp.pad(st, ((0, 0), (0, 0), (0, LANE - DV_B), (0, HK_B - DK_B)))
    z = jnp.zeros_like(st)
    odd = (jnp.arange(H_B) % 2 == 1)[None, :, None, None]
    return jnp.concatenate([jnp.where(odd, z, st), jnp.where(odd, st, z)], axis=-1)


def _unpad_state(st):
    odd = (jnp.arange(H_B) % 2 == 1)[None, :, None, None]
    half = jnp.where(odd, st[..., HK_B:], st[..., :HK_B])
    return jnp.swapaxes(half[:, :, :DV_B, :DK_B], -1, -2)


def kernel(x_prompt, x_sample, c_prompt, c_sample, cache_k_sb, cache_v_sb, state_gla, state_conv, state_lru, w_ada, b_ada, g_pre_mix, g_post_mix, g_pre_ff, g_post_ff, w_in, w_gla_gate, b_gla_gate, g_gla_norm, w_conv, b_conv, w_rg_a, b_rg_a, w_rg_x, b_rg_x, lru_lambda, w_out, w_router, b_router, w_gate_up, b_gate_up, w_down, b_down):
    depth = w_ada.shape[0]
    d = D_MODEL
    groups = []
    for x, past in ((x_prompt, False), (x_sample, True)):
        b, t, _ = x.shape
        groups.append(dict(b=b, t=t, tm=min(TOKEN_TILE, t), n=b * t, past=past, x=x.reshape(b * t, d)))
    n_all = sum(g["n"] for g in groups)
    bm = MOE_BLOCK_ROWS
    n_tiles = sum(g["n"] // g["tm"] for g in groups)
    nb = -(-(n_all * TOP_K + n_tiles * N_EXPERTS * (SUBLANE - 1)) // bm) + N_EXPERTS
    n_pad = nb * bm

    nb_rows = sum(g["b"] for g in groups)
    c_all = jnp.concatenate([c_prompt, c_sample], axis=0)
    r_pad = -(-nb_rows // SUBLANE) * SUBLANE
    c_all = jnp.pad(c_all, ((0, r_pad - nb_rows), (0, 0)))
    mod = _modulation(c_all, w_ada, b_ada)

    st =[[[] for _ in range(5)] for _ in groups]
    for l in range(depth):
        lw = _layer_weights(l, w_in, w_gla_gate, b_gla_gate, g_gla_norm, w_rg_a, w_rg_x, w_out,
                            w_router, b_router)
        row0 = 0
        per_group = []
        for gi, g in enumerate(groups):
            b, t, tm, n = g["b"], g["t"], g["tm"], g["n"]
            m = mod[l, row0:row0 + b].reshape(b, 1, 6 * d)
            row0 += b
            sh_m, sc_m, gt_m, sh_f, sc_f, gt_f = [m[:, :, j * d:(j + 1) * d] for j in range(6)]
            ka, va, qab, kab, vab, gla_in, lru_in = _premix(g["x"], g_pre_mix[l], sc_m, sh_m, lw["w_in_p"], t, tm)
            q3 = qab.reshape(b, t, W_A)
            k3 = kab.reshape(b, t, W_A)
            v3 = vab.reshape(b, t, W_A)
            if g["past"]:
                n_past = cache_k_sb.shape[2]
                padk = ((0, 0), (0, -t % ATTN_KEY_TILE), (0, 0))
                oa = _attention(q3, jnp.pad(k3, padk), jnp.pad(v3, padk),
                                cache_k_sb.reshape(depth, b, n_past, W_A), cache_v_sb.reshape(depth, b, n_past, W_A),
                                layer=l)
                s0 = state_gla[l]
                buf0 = state_conv[l]
                h0 = state_lru[l]
            else:
                oa = _attention(q3, k3, v3)
                s0 = jnp.zeros((b, H_B, DK_B, DV_B), F32)
                buf0 = jnp.zeros((b, CONV_W - 1, W_C), F32)
                h0 = jnp.zeros((b, W_C), F32)
            oa = oa.reshape(n, W_A)
            ob, s_new = _gla(gla_in, lw["wg_p"], lw["bg_p"], lw["gn_p"], _pad_state(s0), t)
            buf0p = jnp.pad(buf0, ((0, 0), (SUBLANE - (CONV_W - 1), 0), (0, 0)))
            oc, conv_new, h_new = _lru(lru_in, w_conv[l], b_conv[l].reshape(1, -1), lw["wa_bd"],
                                       b_rg_a[l].reshape(1, -1), lw["wx_bd"], b_rg_x[l].reshape(1, -1),
                                       lru_lambda[l].reshape(1, -1), buf0p, h0.reshape(b, 1, W_C), t)
            xm, h2, route, stat = _postmix(g["x"], oa, ob, oc, lw["wo_a"], lw["wo_b"], lw["wo_c"],
                                           g_post_mix[l], g_pre_ff[l], gt_m, sc_f, sh_f,
                                           lw["wr_hi"], lw["wr_lo"], lw["br"], t, tm)
            per_group.append(dict(xm=xm, h2=h2, route=route, stat=stat, gt_f=gt_f))
            new = (ka.reshape(b, t, W_A // HD_A, HD_A), va.reshape(b, t, W_A // HD_A, HD_A),
                   _unpad_state(s_new), conv_new[:, SUBLANE - (CONV_W - 1):], h_new.reshape(b, W_C))
            for j in range(5):
                st[gi][j].append(new[j])

        stat = jnp.concatenate([p["stat"].reshape(-1, SUBLANE, LANE)[:, :2, :N_EXPERTS] for p in per_group],
                               axis=0).astype(jnp.int32)
        run_rows, run_loc = stat[:, 0], stat[:, 1]
        before = jnp.cumsum(run_rows, axis=0) - run_rows
        counts = jnp.sum(run_rows, axis=0)
        padded = (counts + bm - 1) // bm * bm
        pad_end = jnp.cumsum(padded)
        pad_start = pad_end - padded
        block_start = jnp.arange(nb, dtype=jnp.int32) * bm
        block_e = jnp.minimum(jnp.sum((pad_end[None, :] <= block_start[:, None]).astype(jnp.int32), axis=1),
                              N_EXPERTS - 1)
        nvalid = (pad_end[-1] // bm).astype(jnp.int32).reshape(1)
        tile_rows = jnp.broadcast_to(jnp.sum(run_rows, axis=1, keepdims=True), run_rows.shape)
        tab = jnp.concatenate([run_rows, run_loc, pad_start[None, :] + before, tile_rows],
                              axis=1).reshape(-1, 1, LANE)
        tile0 = 0
        for gi, g in enumerate(groups):
            nt = g["n"] // g["tm"]
            per_group[gi]["tab"] = tab[tile0:tile0 + nt]
            tile0 += nt
        zeros32 = jnp.zeros_like(counts)
        tail = jnp.concatenate([padded - counts, zeros32, pad_start + counts,
                                jnp.broadcast_to(nvalid, (N_EXPERTS,))]).reshape(1, 1, LANE)
        first, second = per_group
        xs = _dispatch(first["tab"], tail, first["route"], first["h2"], None, n_pad, groups[0]["tm"],
                       later=second["tab"], later_tm=groups[1]["tm"])
        xs = _dispatch(second["tab"], tail, second["route"], second["h2"], xs, n_pad, groups[1]["tm"])
        ys = _experts(l, block_e, nvalid, xs, w_gate_up, b_gate_up, w_down, b_down)
        for gi, g in enumerate(groups):
            p = per_group[gi]
            g["x"] = _combine(p["tab"], p["route"], ys, p["xm"], p["gt_f"], g_post_ff[l], g["t"], g["tm"])

    outs = [g["x"].reshape(g["b"], g["t"], d) for g in groups]
    for gi in range(len(groups)):
        outs.extend(jnp.stack(st[gi][j]) for j in range(5))
    return tuple(outs)
```

```python
import functools
import math

import numpy as np
import jax
import jax.numpy as jnp
from jax import lax
from jax.experimental import pallas as pl
from jax.experimental.pallas import tpu as pltpu

F32 = jnp.float32
BF16 = jnp.bfloat16

D_MODEL = 1024
EPS = 1e-6
HD_A = 64
W_A = 384
H_B = 4
DK_B = 48
DV_B = 96
W_B = 384
GATE_RANK_B = 16
GATE_TEMP_B = 16.0
W_C = 256
NBLK_C = 4
BW_C = 64
CONV_W = 4
RG_C = 8.0
N_EXPERTS = 32
TOP_K = 4
D_FF = 1024
SWIGLU_ALPHA = 1.702
SWIGLU_LIMIT = 7.0
LOG_WEIGHT_FLOOR = -120.0

LANE = 128
SUBLANE = 8
V7X_VMEM_BYTES = 64 * 1024 * 1024
VMEM_LIMIT = V7X_VMEM_BYTES * 7 // 8
ROUTER_PAD_BIAS = -1e30

HP_B = H_B * LANE
HK_B = LANE // 2
HPK_B = H_B * HK_B
OFF_GLA = 3 * W_A
GLA_W = 2 * HPK_B + 2 * HP_B + LANE
OFF_LRU = OFF_GLA + GLA_W
IN_PAD = OFF_LRU + 2 * W_C

TOKEN_TILE = 256
ATTN_TILE = 256
ATTN_KEY_TILE = 256
ATTN_CHAINS = 8
GLA_CHUNK = 128
GLA_CHUNKS_PER_STEP = 2
LRU_CHUNK = 256
MOE_BLOCK_ROWS = 512
RUN_BIG_ROWS = 64


def _cparams(sem):
    return pltpu.CompilerParams(dimension_semantics=sem, vmem_limit_bytes=VMEM_LIMIT)


def _sigmoid(x):
    return 1.0 / (1.0 + jnp.exp(-x))


def _log_sigmoid(x):
    return jnp.minimum(x, 0.0) - jnp.log(1.0 + jnp.exp(-jnp.abs(x)))


def _split_bf16(x):
    hi = x.astype(BF16)
    lo = (x - hi.astype(F32)).astype(BF16)
    return hi, lo


def _dot(a, b):
    return jnp.dot(a, b, preferred_element_type=F32)


def _dot_nt(a, b):
    return lax.dot_general(a, b, (((1,), (1,)), ((), ())), preferred_element_type=F32)


def _rms(x, g):
    return x * lax.rsqrt(jnp.mean(x * x, axis=-1, keepdims=True) + EPS) * g


def _mod_kernel(c_ref, w_ref, b_ref, o_ref):
    c = c_ref[...]
    a = (c * _sigmoid(c)).astype(BF16)
    o_ref[0] = _dot(a, w_ref[0].astype(BF16)) + b_ref[0]


def _modulation(c_all, w_ada, b_ada):
    depth, d, n = w_ada.shape
    r = c_all.shape[0]
    tn = n // 4
    return pl.pallas_call(
        _mod_kernel,
        grid=(depth, n // tn),
        in_specs=[pl.BlockSpec((r, d), lambda l, j: (0, 0)),
                  pl.BlockSpec((1, d, tn), lambda l, j: (l, 0, j)),
                  pl.BlockSpec((1, 1, tn), lambda l, j: (l, 0, j))],
        out_specs=pl.BlockSpec((1, r, tn), lambda l, j: (l, 0, j)),
        out_shape=jax.ShapeDtypeStruct((depth, r, n), F32),
        compiler_params=_cparams(("arbitrary", "arbitrary")),
        name="adaln_mod",
    )(c_all, w_ada, b_ada.reshape(depth, 1, n))


def _premix_kernel(x_ref, g_ref, sc_ref, sh_ref, w_ref,
                   ka_ref, va_ref, qab_ref, kab_ref, vab_ref, gla_ref, lru_ref):
    h = _rms(x_ref[...], g_ref[...]) * (1.0 + sc_ref[0]) + sh_ref[0]
    r = _dot(h.astype(BF16), w_ref[...])
    ka = r[:, W_A:2 * W_A]
    va = r[:, 2 * W_A:3 * W_A]
    ka_ref[...] = ka
    va_ref[...] = va
    qab_ref[...] = (r[:, 0:W_A] * (HD_A ** -0.5)).astype(BF16)
    kab_ref[...] = ka.astype(BF16)
    vab_ref[...] = va.astype(BF16)
    gla_ref[...] = r[:, OFF_GLA:OFF_LRU]
    lru_ref[...] = r[:, OFF_LRU:IN_PAD]


def _premix(x2, g, sc, sh, w_in_p, seq, tm):
    n, d = x2.shape
    per_b = seq // tm
    tok = lambda i: (i, 0)
    bat = lambda i: (i // per_b, 0, 0)
    const = lambda i: (0, 0)
    outs = [((n, W_A), F32), ((n, W_A), F32), ((n, W_A), BF16), ((n, W_A), BF16), ((n, W_A), BF16),
            ((n, GLA_W), F32), ((n, 2 * W_C), F32)]
    return pl.pallas_call(
        _premix_kernel,
        grid=(n // tm,),
        in_specs=[pl.BlockSpec((tm, d), tok), pl.BlockSpec((1, d), const),
                  pl.BlockSpec((1, 1, d), bat), pl.BlockSpec((1, 1, d), bat),
                  pl.BlockSpec((d, IN_PAD), const)],
        out_specs=[pl.BlockSpec((tm, s[1]), tok) for s, _ in outs],
        out_shape=[jax.ShapeDtypeStruct(s, t) for s, t in outs],
        compiler_params=_cparams(("arbitrary",)),
        name="premix_proj",
    )(x2, g.reshape(1, d), sc, sh, w_in_p)


def _attn_kernel(q_ref, k_ref, v_ref, *rest, tq, tk, off, has_past, nsub, nbat):
    if has_past:
        kp_ref, vp_ref, u_ref, o_ref, acc_ref, c_ref = rest
    else:
        u_ref, o_ref, acc_ref, c_ref = rest
    i = pl.program_id(2)
    lane = lax.broadcasted_iota(jnp.int32, (1, LANE), 1)
    head_lanes = (lane < HD_A, lane >= HD_A)
    acc_ref[...] = jnp.zeros_like(acc_ref)
    c_ref[...] = jnp.zeros_like(c_ref)
    row = lax.broadcasted_iota(jnp.int32, (tq, tk), 0)
    col = lax.broadcasted_iota(jnp.int32, (tq, tk), 1)
    u = u_ref[...]
    chains = [(bi, a) for bi in range(nbat) for a in range(nsub)]
    qh, qpos0, nk = [], [], []
    for bi, a in chains:
        q = q_ref[bi, a * tq:(a + 1) * tq, :]
        qz = jnp.zeros_like(q)
        qh.append(tuple(jnp.where(m, q, qz) for m in head_lanes))
        qpos0.append(off + (i * nsub + a) * tq)
        nk.append((qpos0[-1] + tq - 1 + tk - 1) // tk)

    def step(jj, masked):
        cmax = jnp.float32(-jnp.inf)
        for a, (bi, _) in enumerate(chains):
            j = nk[a] - 1 - jj
            live = j >= 0
            ks = pl.multiple_of(jnp.maximum(j, 0) * tk, tk)
            if has_past and not masked:
                kb = kp_ref[0, bi, pl.ds(ks, tk), :].astype(BF16)
                vb = vp_ref[0, bi, pl.ds(ks, tk), :].astype(BF16)
            else:
                kb = k_ref[bi, pl.ds(pl.multiple_of(ks - off, tk), tk), :]
                vb = v_ref[bi, pl.ds(pl.multiple_of(ks - off, tk), tk), :]
            vz = jnp.zeros_like(vb)
            mask = (col + ks) < (row + qpos0[a])
            for h in range(2):
                s = _dot_nt(qh[a][h], kb)
                lf = -(jnp.maximum(s, 0.0) + jnp.log(1.0 + jnp.exp(-jnp.abs(s))))
                lb = s + lf
                if masked:
                    lf = jnp.where(mask, lf, 0.0)
                hi, lo = _split_bf16(lf)
                cr = _dot(jnp.concatenate([hi, lo], axis=1), u)
                c = c_ref[a, h]
                cfull = jnp.concatenate([c] * (tk // LANE), axis=1)
                w = jnp.exp(lb + cr[:, :tk] + cfull)
                if masked:
                    w = jnp.where(mask, w, 0.0)
                vh = jnp.where(head_lanes[h], vb, vz)
                pv = _dot(w.astype(BF16), vh)
                rs = cr[:, tk:]
                if not masked:
                    pv = jnp.where(live, pv, 0.0)
                    rs = jnp.where(live, rs, 0.0)
                acc_ref[a] += pv
                c_ref[a, h] = c + rs
            cmax = jnp.maximum(cmax, jnp.where(j > 0, jnp.max(c_ref[a]), -jnp.inf))
        return cmax > LOG_WEIGHT_FLOOR

    n_masked = max(1, tq // tk)
    for jj in range(n_masked):
        go = step(jj, True)

    def body(carry):
        jj, _ = carry
        return jj + 1, step(jj, False)

    lax.while_loop(lambda carry: jnp.logical_and(carry[0] < nk[-1], carry[1]), body, (jnp.int32(n_masked), go))
    for c, (bi, a) in enumerate(chains):
        o_ref[bi, a * tq:(a + 1) * tq, :] = acc_ref[c].astype(o_ref.dtype)


def _attn_umat(tk):
    jp = np.arange(tk)[:, None]
    j = np.arange(tk)[None, :]
    u = np.concatenate([(jp > j).astype(np.float32), np.ones((tk, LANE), np.float32)], axis=1)
    return jnp.asarray(np.concatenate([u, u], axis=0), BF16)


def _attention(q, k, v, kp=None, vp=None, layer=0):
    b, tq_all, _ = q.shape
    tc = k.shape[1]
    has_past = kp is not None
    off = kp.shape[2] if has_past else 0
    tq = min(ATTN_TILE, tq_all)
    tk = ATTN_KEY_TILE
    assert tc % tk == 0 and tq_all % tq == 0 and tq_all <= tc
    assert off % tk == 0 and (tk % tq == 0 or tq % tk == 0)
    assert not has_past or (tq == tq_all and tq_all <= tk)
    nsub = math.gcd(ATTN_CHAINS, tq_all // tq)
    nbat = math.gcd(ATTN_CHAINS // nsub, b)
    kern = functools.partial(_attn_kernel, tq=tq, tk=tk, off=off, has_past=has_past, nsub=nsub, nbat=nbat)
    cur = pl.BlockSpec((nbat, tc, LANE), lambda bi, hp, i: (bi, 0, hp))
    in_specs = [pl.BlockSpec((nbat, nsub * tq, LANE), lambda bi, hp, i: (bi, i, hp)), cur, cur]
    args = [q, k, v]
    if has_past:
        past = pl.BlockSpec((1, nbat, off, LANE), lambda bi, hp, i: (layer, bi, 0, hp))
        in_specs += [past, past]
        args += [kp, vp]
    in_specs.append(pl.BlockSpec((2 * tk, tk + LANE), lambda bi, hp, i: (0, 0)))
    args.append(_attn_umat(tk))
    return pl.pallas_call(
        kern,
        grid=(b // nbat, W_A // LANE, tq_all // (nsub * tq)),
        in_specs=in_specs,
        out_specs=pl.BlockSpec((nbat, nsub * tq, LANE), lambda bi, hp, i: (bi, i, hp)),
        out_shape=jax.ShapeDtypeStruct((b, tq_all, W_A), BF16),
        scratch_shapes=[pltpu.VMEM((nbat * nsub, tq, LANE), F32), pltpu.VMEM((nbat * nsub, 2, tq, LANE), F32)],
        compiler_params=_cparams(("arbitrary", "arbitrary", "arbitrary")),
        name="stick_breaking_attn",
    )(*args)


def _gla_levels(c):
    return int(math.log2(c))


def _gla_mats(c):
    t = np.arange(c)[:, None]
    s = np.arange(c)[None, :]
    mats = [(s <= t).astype(np.float32), (s > t).astype(np.float32)]
    for lv in range(_gla_levels(c)):
        m = 1 << lv
        ref = (t // (2 * m)) * (2 * m) + m - 1
        mats.append(((s > ref) & (s <= t)).astype(np.float32) - ((s > t) & (s <= ref)).astype(np.float32))
    m = np.concatenate(mats, axis=0)
    return jnp.asarray(np.concatenate([m, m], axis=1), BF16)


def _gla_kernel(in_ref, wg_ref, bg_ref, gn_ref, mall_ref, s0_ref, ob_ref, sout_ref, st_ref, *, C, G):
    ci = pl.program_id(1)

    @pl.when(ci == 0)
    def _():
        st_ref[...] = s0_ref[0]

    for sub in range(G):
        _gla_chunk(in_ref, wg_ref, bg_ref, gn_ref, mall_ref, ob_ref, st_ref, C, sub * C)

    @pl.when(ci == pl.num_programs(1) - 1)
    def _():
        sout_ref[0] = st_ref[...]


def _gla_chunk(in_ref, wg_ref, bg_ref, gn_ref, mall_ref, ob_ref, st_ref, C, r0):
    rows = slice(r0, r0 + C)
    q = in_ref[rows, 0:HPK_B] * (DK_B ** -0.5)
    k = in_ref[rows, HPK_B:2 * HPK_B]
    v = in_ref[rows, 2 * HPK_B:2 * HPK_B + HP_B]
    gb = in_ref[rows, 2 * HPK_B + HP_B:2 * HPK_B + 2 * HP_B]
    rb = in_ref[rows, 2 * HPK_B + 2 * HP_B:GLA_W]
    lg = _log_sigmoid(_dot(rb.astype(BF16), wg_ref[...]) + bg_ref[...]) * (1.0 / GATE_TEMP_B)
    hi, lo = _split_bf16(lg)
    dall = _dot(mall_ref[...], jnp.concatenate([hi, lo], axis=0))
    eb = jnp.exp(dall[0:C])
    elast = jnp.exp(dall[C:2 * C])
    eb_last = eb[C - 1:C, :]
    rowi = lax.broadcasted_iota(jnp.int32, (C, 1), 0)
    row = lax.broadcasted_iota(jnp.int32, (C, C), 0)
    col = lax.broadcasted_iota(jnp.int32, (C, C), 1)
    lane = lax.broadcasted_iota(jnp.int32, (1, LANE), 1)
    slot_half = (lane < HK_B, lane >= HK_B)
    qe = q * eb
    ke = k * elast
    kb = k.astype(BF16)
    vb = v.astype(BF16)
    zero = jnp.zeros_like(q)
    lv_q, lv_k = [], []
    for lv in range(_gla_levels(C)):
        e = jnp.exp(-jnp.abs(dall[(2 + lv) * C:(3 + lv) * C]))
        second = ((rowi >> lv) & 1) == 1
        lv_q.append(jnp.where(second, q * e, zero))
        lv_k.append(jnp.where(second, zero, k * e).astype(BF16))
    outs = []
    for h in range(H_B):
        sl = slice(h * LANE, (h + 1) * LANE)
        sk = slice((h // 2) * LANE, (h // 2 + 1) * LANE)

        def own(x):
            return jnp.where(slot_half[h % 2], x[:, sk], 0.0).astype(BF16)

        att = jnp.where(row == col, _dot_nt(own(q), kb[:, sk]), 0.0)
        for lv in range(_gla_levels(C)):
            same = (row >> (lv + 1)) == (col >> (lv + 1))
            att = att + jnp.where(same, _dot_nt(own(lv_q[lv]), lv_k[lv][:, sk]), 0.0)
        st = st_ref[h]
        o = _dot(att.astype(BF16), vb[:, sl]) + _dot_nt(own(qe), st.astype(BF16))
        st_ref[h] = st * eb_last[:, sk] + _dot(v[:, sl].T.astype(BF16), own(ke))
        ms = jnp.sum(o * o, axis=-1, keepdims=True) * (1.0 / DV_B)
        on = o * lax.rsqrt(ms + EPS) * gn_ref[:, sl]
        g = gb[:, sl]
        outs.append((on * (g * _sigmoid(g))).astype(BF16))
    ob_ref[rows, :] = jnp.concatenate(outs, axis=1)


def _gla(gla_in, wg_p, bg_p, gn_p, s0t, seq):
    n = gla_in.shape[0]
    b = n // seq
    c = min(GLA_CHUNK, seq)
    g = math.gcd(GLA_CHUNKS_PER_STEP, seq // c)
    per_b = seq // (c * g)
    mall = _gla_mats(c)
    const2 = lambda bi, ci: (0, 0)
    return pl.pallas_call(
        functools.partial(_gla_kernel, C=c, G=g),
        grid=(b, per_b),
        in_specs=[pl.BlockSpec((c * g, GLA_W), lambda bi, ci: (bi * per_b + ci, 0)),
                  pl.BlockSpec((LANE, HPK_B), const2), pl.BlockSpec((1, HPK_B), const2),
                  pl.BlockSpec((1, HP_B), const2), pl.BlockSpec(mall.shape, const2),
                  pl.BlockSpec((1, H_B, LANE, LANE), lambda bi, ci: (bi, 0, 0, 0))],
        out_specs=[pl.BlockSpec((c * g, HP_B), lambda bi, ci: (bi * per_b + ci, 0)),
                   pl.BlockSpec((1, H_B, LANE, LANE), lambda bi, ci: (bi, 0, 0, 0))],
        out_shape=[jax.ShapeDtypeStruct((n, HP_B), BF16),
                   jax.ShapeDtypeStruct((b, H_B, LANE, LANE), F32)],
        scratch_shapes=[pltpu.VMEM((H_B, LANE, LANE), F32)],
        compiler_params=_cparams(("arbitrary", "arbitrary")),
        name="gla_chunked",
    )(gla_in, wg_p, bg_p, gn_p, mall, s0t)


def _lru_kernel(in_ref, cw_ref, cb_ref, wa_ref, ba_ref, wx_ref, bx_ref, lam_ref, buf0_ref, h0_ref,
                oc_ref, conv_ref, hout_ref, xp_ref, hc_ref, *, C):
    ci = pl.program_id(1)

    @pl.when(ci == 0)
    def _():
        xp_ref[0:SUBLANE] = buf0_ref[0]
        hc_ref[...] = h0_ref[0]

    x = in_ref[:, 0:W_C]
    y = in_ref[:, W_C:2 * W_C]
    xp_ref[SUBLANE:SUBLANE + C] = x
    xc = cb_ref[...]
    for j in range(CONV_W):
        xc = xc + xp_ref[pl.ds(SUBLANE - (CONV_W - 1) + j, C), :] * cw_ref[j:j + 1, :]
    xcb = xc.astype(BF16)
    r = _sigmoid(_dot(xcb, wa_ref[...]) + ba_ref[...])
    gi = _sigmoid(_dot(xcb, wx_ref[...]) + bx_ref[...])
    log_a = RG_C * r * _log_sigmoid(lam_ref[...])
    a = jnp.exp(log_a)
    u = jnp.sqrt(1.0 - jnp.exp(2.0 * log_a)) * (gi * xc)
    rowi = lax.broadcasted_iota(jnp.int32, (C, 1), 0)
    d = 1
    while d < C:
        keep = rowi >= d
        a_s = pltpu.roll(a, d, axis=0)
        u_s = pltpu.roll(u, d, axis=0)
        u = jnp.where(keep, a * u_s + u, u)
        a = jnp.where(keep, a * a_s, a)
        d *= 2
    hseq = u + a * hc_ref[...]
    hc_ref[...] = hseq[C - 1:C, :]
    gelu = 0.5 * y * (1.0 + jnp.tanh(math.sqrt(2.0 / math.pi) * (y + 0.044715 * (y * y * y))))
    oc_ref[...] = (hseq * gelu).astype(BF16)
    tail = xp_ref[C:C + SUBLANE]
    xp_ref[0:SUBLANE] = tail

    @pl.when(ci == pl.num_programs(1) - 1)
    def _():
        conv_ref[0] = tail
        hout_ref[0] = hseq[C - 1:C, :]


def _lru(lru_in, cw, cb, wa_bd, ba, wx_bd, bx, lam, buf0, h0, seq):
    n = lru_in.shape[0]
    b = n // seq
    c = min(LRU_CHUNK, seq)
    per_b = seq // c
    const2 = lambda bi, ci: (0, 0)
    vec = pl.BlockSpec((1, W_C), const2)
    return pl.pallas_call(
        functools.partial(_lru_kernel, C=c),
        grid=(b, per_b),
        in_specs=[pl.BlockSpec((c, 2 * W_C), lambda bi, ci: (bi * per_b + ci, 0)),
                  pl.BlockSpec((CONV_W, W_C), const2), vec,
                  pl.BlockSpec((W_C, W_C), const2), vec, pl.BlockSpec((W_C, W_C), const2), vec, vec,
                  pl.BlockSpec((1, SUBLANE, W_C), lambda bi, ci: (bi, 0, 0)),
                  pl.BlockSpec((1, 1, W_C), lambda bi, ci: (bi, 0, 0))],
        out_specs=[pl.BlockSpec((c, W_C), lambda bi, ci: (bi * per_b + ci, 0)),
                   pl.BlockSpec((1, SUBLANE, W_C), lambda bi, ci: (bi, 0, 0)),
                   pl.BlockSpec((1, 1, W_C), lambda bi, ci: (bi, 0, 0))],
        out_shape=[jax.ShapeDtypeStruct((n, W_C), BF16),
                   jax.ShapeDtypeStruct((b, SUBLANE, W_C), F32),
                   jax.ShapeDtypeStruct((b, 1, W_C), F32)],
        scratch_shapes=[pltpu.VMEM((c + SUBLANE, W_C), F32), pltpu.VMEM((1, W_C), F32)],
        compiler_params=_cparams(("arbitrary", "arbitrary")),
        name="conv_rglru",
    )(lru_in, cw, cb, wa_bd, ba, wx_bd, bx, lam, buf0, h0)


def _postmix_kernel(x_ref, oa_ref, ob_ref, oc_ref, wa_ref, wb_ref, wc_ref, gpm_ref, gpf_ref,
                    gt_ref, sc_ref, sh_ref, wr_hi_ref, wr_lo_ref, br_ref, tri_ref, ustrict_ref,
                    xm_ref, h2_ref, route_ref, stat_ref, *, tm):
    y =_dot(oa_ref[...], wa_ref[...]) + _dot(ob_ref[...], wb_ref[...]) + _dot(oc_ref[...], wc_ref[...])
    xm = x_ref[...] + gt_ref[0] * _rms(y, gpm_ref[...])
    xm_ref[...] = xm
    h2 = _rms(xm, gpf_ref[...]) * (1.0 + sc_ref[0]) + sh_ref[0]
    h2_ref[...] = h2
    hi, lo = _split_bf16(h2)
    wh = wr_hi_ref[...]
    p_hi = _dot(hi, jnp.concatenate([wh, wr_lo_ref[...]], axis=1))
    logits = p_hi[:, :LANE] + _dot(lo, wh) + p_hi[:, LANE:] + br_ref[...]
    lane = lax.broadcasted_iota(jnp.int32, (tm, LANE), 1)
    lane_f = lane.astype(F32)
    neg = jnp.float32(-jnp.inf)
    vals, hots = [], []
    for _ in range(TOP_K):
        m = jnp.max(logits, axis=-1, keepdims=True)
        idx = jnp.min(jnp.where(logits == m, lane_f, float(LANE)), axis=-1, keepdims=True)
        hot = lane_f == idx
        logits = jnp.where(hot, neg, logits)
        vals.append(m)
        hots.append(hot)
    ex = [jnp.exp(vk - vals[0]) for vk in vals]
    inv = 1.0 / (ex[0] + ex[1] + ex[2] + ex[3])
    sel = jnp.zeros((tm, LANE), F32)
    for hot in hots:
        sel = jnp.where(hot, 1.0, sel)
    selb = sel.astype(BF16)
    tile_cnt = jnp.sum(sel, axis=0, keepdims=True)
    groups8 = jnp.floor((tile_cnt + (SUBLANE - 1.0)) * (1.0 / SUBLANE))
    g8b = jnp.broadcast_to(groups8, (SUBLANE, LANE)).astype(BF16)
    loc_start = _dot(g8b, ustrict_ref[...])[0:1] * float(SUBLANE)
    local = _dot(tri_ref[...], selb) + loc_start
    route = jnp.zeros((tm, LANE), F32)
    for kk in range(TOP_K):
        p_k = jnp.sum(jnp.where(hots[kk], local, 0.0), axis=-1, keepdims=True)
        route = jnp.where(lane == kk, ex[kk] * inv, route)
        route = jnp.where(lane == TOP_K + kk, p_k, route)
    route_ref[...] = route
    srow = lax.broadcasted_iota(jnp.int32, (SUBLANE, LANE), 0)
    stat_ref[...] = jnp.where(srow == 0, groups8 * float(SUBLANE), jnp.where(srow == 1, loc_start, 0.0))


def _postmix(x2, oa, ob, oc, wo_a, wo_b, wo_c, gpm, gpf, gt, sc, sh, wr_hi, wr_lo, br, seq, tm):
    n, d = x2.shape
    per_b = seq // tm
    tok = lambda i: (i, 0)
    bat = lambda i: (i // per_b, 0, 0)
    const = lambda i: (0, 0)
    tri = jnp.asarray(np.tril(np.ones((tm, tm), np.float32), -1), BF16)
    ustrict = jnp.asarray(np.triu(np.ones((LANE, LANE), np.float32), 1), BF16)
    full = lambda a: pl.BlockSpec(a.shape, const)
    nt = n // tm
    return pl.pallas_call(
        functools.partial(_postmix_kernel, tm=tm),
        grid=(nt,),
        in_specs=[pl.BlockSpec((tm, d), tok), pl.BlockSpec((tm, W_A), tok),
                  pl.BlockSpec((tm, HP_B), tok), pl.BlockSpec((tm, W_C), tok),
                  full(wo_a), full(wo_b), full(wo_c),
                  pl.BlockSpec((1, d), const), pl.BlockSpec((1, d), const),
                  pl.BlockSpec((1, 1, d), bat), pl.BlockSpec((1, 1, d), bat), pl.BlockSpec((1, 1, d), bat),
                  full(wr_hi), full(wr_lo), pl.BlockSpec((1, LANE), const), full(tri), full(ustrict)],
        out_specs=[pl.BlockSpec((tm, d), tok), pl.BlockSpec((tm, d), tok),
                   pl.BlockSpec((tm, LANE), tok), pl.BlockSpec((SUBLANE, LANE), tok)],
        out_shape=[jax.ShapeDtypeStruct((n, d), F32), jax.ShapeDtypeStruct((n, d), F32),
                   jax.ShapeDtypeStruct((n, LANE), F32), jax.ShapeDtypeStruct((nt * SUBLANE, LANE), F32)],
        compiler_params=_cparams(("arbitrary",)),
        name="postmix_router",
    )(x2, oa, ob, oc, wo_a, wo_b, wo_c, gpm.reshape(1, d), gpf.reshape(1, d), gt, sc, sh,
      wr_hi, wr_lo, br, tri, ustrict)


def _run_copies(tab_ref, tm, make_copy, wait, tile=0):
    sizes = [1 << b for b in range(int(math.log2(tm)), int(math.log2(SUBLANE)) - 1, -1)]
    big = [sz for sz in sizes if sz >= RUN_BIG_ROWS]
    small = [sz for sz in sizes if sz < RUN_BIG_ROWS]

    def per_expert(e, carry):
        n = tab_ref[tile, 0, e]
        loc = tab_ref[tile, 0, N_EXPERTS + e]
        dst = tab_ref[tile, 0, 2 * N_EXPERTS + e]

        def pieces(szs):
            for sz in szs:
                done = n & ~(2 * sz - 1)

                @pl.when((n & sz) != 0)
                def _():
                    cp = make_copy(pl.multiple_of(loc + done, SUBLANE), pl.multiple_of(dst + done, SUBLANE), sz)
                    if wait:
                        cp.wait()
                    else:
                        cp.start()

        if big:
            pl.when(n >= RUN_BIG_ROWS)(lambda: pieces(big))
        pieces(small)
        return carry

    lax.fori_loop(0, N_EXPERTS, per_expert, 0)


def _wait_rows(total, make_copy, max_rows):
    top = int(math.log2(max_rows))
    for sz in [1 << b for b in range(top, int(math.log2(SUBLANE)) - 1, -1)]:
        @pl.when((total & sz) != 0)
        def _():
            make_copy(0, 0, sz).wait()


def _local_rows(tm):
    return tm * TOP_K + N_EXPERTS * SUBLANE


def _slot_onehot(route_ref, tm):
    nloc = _local_rows(tm)
    pos = lax.broadcasted_iota(jnp.int32, (tm, nloc), 1)
    route = route_ref[...]
    slot = [route[:, TOP_K + kk:TOP_K + kk + 1].astype(jnp.int32) for kk in range(TOP_K)]

    def build(values):
        m = jnp.zeros((tm, nloc), F32)
        for kk in range(TOP_K):
            m = jnp.where(slot[kk] == pos, values[kk], m)
        return m

    return route, build


def _dispatch_kernel(tab_ref, tabp_ref, tail_ref, route_ref, h_ref, *rest, tm, first, later_tile):
    i = pl.program_id(0)
    slot = i % 2
    if first:
        later_ref, xs_ref, sorted_ref, sem, zero_ref = rest
        bm = MOE_BLOCK_ROWS

        @pl.when(i == 0)
        def _():
            zero_ref[...] = jnp.zeros_like(zero_ref)
            fill = lambda loc, dst, sz: pltpu.make_async_copy(zero_ref.at[pl.ds(0, sz)],
                                                              xs_ref.at[pl.ds(dst, sz)], sem.at[0])
            n_later, later_tm = later_ref.shape[0], later_tile

            def block_fill(wait):
                def body(g, carry):
                    cp = fill(0, pl.multiple_of(g * bm, bm), bm)
                    cp.wait() if wait else cp.start()
                    return carry
                lax.fori_loop(tail_ref[0, 0, 3 * N_EXPERTS], xs_ref.shape[0] // bm, body, 0)

            for wait in (False, True):
                _run_copies(tail_ref, bm, fill, wait)
                block_fill(wait)
                for ti in range(n_later):
                    _run_copies(later_ref, later_tm, fill, wait, tile=ti)
    else:
        _, xs_ref, sorted_ref, sem = rest
    _, build = _slot_onehot(route_ref, tm)
    perm = build([1.0] * TOP_K).astype(BF16)
    sorted_ref[slot] = lax.dot_general(perm, h_ref[...].astype(BF16), (((0,), (0,)), ((), ())),
                                       preferred_element_type=F32)

    def runs(s):
        return lambda loc, dst, sz: pltpu.make_async_copy(sorted_ref.at[s, pl.ds(loc, sz)],
                                                          xs_ref.at[pl.ds(dst, sz)], sem.at[s])

    _run_copies(tab_ref, tm, runs(slot), wait=False)

    @pl.when(i > 0)
    def _():
        _wait_rows(tabp_ref[0, 0, 3 * N_EXPERTS], runs(1 - slot), _local_rows(tm))

    @pl.when(i == pl.num_programs(0) - 1)
    def _():
        _wait_rows(tab_ref[0, 0, 3 * N_EXPERTS], runs(slot), _local_rows(tm))


def _dispatch(tab, tail, route, h2, xs, n_pad, tm, later=None, later_tm=None):
    n, d = h2.shape
    tok = lambda i: (i, 0)
    first = xs is None
    in_specs = [pl.BlockSpec((1, 1, LANE), lambda i: (i, 0, 0), memory_space=pltpu.SMEM),
                pl.BlockSpec((1, 1, LANE), lambda i: (jnp.maximum(i - 1, 0), 0, 0), memory_space=pltpu.SMEM),
                pl.BlockSpec((1, 1, LANE), lambda i: (0, 0, 0), memory_space=pltpu.SMEM),
                pl.BlockSpec((tm, LANE), tok), pl.BlockSpec((tm, d), tok)]
    scratch = [pltpu.VMEM((2, _local_rows(tm), d), F32), pltpu.SemaphoreType.DMA((2,))]
    args = [tab, tab, tail, route, h2]
    if first:
        scratch.append(pltpu.VMEM((MOE_BLOCK_ROWS, d), F32))
        in_specs.append(pl.BlockSpec(later.shape, lambda i: (0, 0, 0), memory_space=pltpu.SMEM))
        args.append(later)
    else:
        in_specs.append(pl.BlockSpec(memory_space=pl.ANY))
        args.append(xs)
    return pl.pallas_call(
        functools.partial(_dispatch_kernel, tm=tm, first=first, later_tile=later_tm),
        grid=(n // tm,),
        in_specs=in_specs,
        out_specs=pl.BlockSpec(memory_space=pl.ANY),
        out_shape=jax.ShapeDtypeStruct((n_pad, d), F32),
        scratch_shapes=scratch,
        input_output_aliases={} if first else {5: 0},
        compiler_params=_cparams(("arbitrary",)),
        name="moe_dispatch",
    )(*args)


def _expert_kernel(be_ref, nv_ref, x_ref, wgu_ref, bgu_ref, wd_ref, bd_ref, y_ref, wgu_bf, wd_bf):
    g = pl.program_id(0)

    @pl.when(jnp.logical_or(g == 0, be_ref[g] != be_ref[jnp.maximum(g - 1, 0)]))
    def _():
        wgu_bf[...] = wgu_ref[0, 0].astype(BF16)
        wd_bf[...] = wd_ref[0, 0].astype(BF16)

    @pl.when(g < nv_ref[0])
    def _():
        gu = _dot(x_ref[...].astype(BF16), wgu_bf[...]) + bgu_ref[0, 0]
        gt = jnp.minimum(gu[:, :D_FF], SWIGLU_LIMIT)
        up = jnp.clip(gu[:, D_FF:], -SWIGLU_LIMIT, SWIGLU_LIMIT)
        act = (up + 1.0) * (gt * _sigmoid(SWIGLU_ALPHA * gt))
        y_ref[...] = _dot(act.astype(BF16), wd_bf[...]) + bd_ref[0, 0]

    @pl.when(g >= nv_ref[0])
    def _():
        y_ref[...] = jnp.zeros_like(y_ref)


def _experts(l, block_e, nvalid, xs, wgu, bgu, wd, bd):
    n_pad, d = xs.shape
    depth = wgu.shape[0]
    bm = MOE_BLOCK_ROWS
    nb = n_pad // bm
    grid_spec = pltpu.PrefetchScalarGridSpec(
        num_scalar_prefetch=2,
        grid=(nb,),
        in_specs=[pl.BlockSpec((bm, d), lambda g, be, nv: (jnp.minimum(g, nv[0] - 1), 0)),
                  pl.BlockSpec((1, 1, d, 2 * D_FF), lambda g, be, nv: (l, be[g], 0, 0)),
                  pl.BlockSpec((1, 1, 1, 2 * D_FF), lambda g, be, nv: (l, be[g], 0, 0)),
                  pl.BlockSpec((1, 1, D_FF, d), lambda g, be, nv: (l, be[g], 0, 0)),
                  pl.BlockSpec((1, 1, 1, d), lambda g, be, nv: (l, be[g], 0, 0))],
        out_specs=pl.BlockSpec((bm, d), lambda g, be, nv: (g, 0)),
        scratch_shapes=[pltpu.VMEM((d, 2 * D_FF), BF16), pltpu.VMEM((D_FF, d), BF16)],
    )
    return pl.pallas_call(
        _expert_kernel,
        grid_spec=grid_spec,
        out_shape=jax.ShapeDtypeStruct((n_pad, d), F32),
        compiler_params=_cparams(("arbitrary",)),
        name="moe_experts",
    )(block_e, nvalid, xs, wgu, bgu.reshape(depth, N_EXPERTS, 1, -1), wd, bd.reshape(depth, N_EXPERTS, 1, -1))


def _combine_kernel(tab_ref, tabn_ref, route_ref, ys_ref, xm_ref, gt_ref, g_ref, o_ref, buf_ref, sem, *, tm):
    i = pl.program_id(0)
    slot = i % 2

    def runs(s):
        return lambda loc, dst, sz: pltpu.make_async_copy(ys_ref.at[pl.ds(dst, sz)],
                                                          buf_ref.at[s, pl.ds(loc, sz)], sem.at[s])

    @pl.when(i == 0)
    def _():
        buf_ref[...] = jnp.zeros_like(buf_ref)
        _run_copies(tab_ref, tm, runs(0), wait=False)

    @pl.when(i + 1 < pl.num_programs(0))
    def _():
        _run_copies(tabn_ref, tm, runs(1 - slot), wait=False)

    route, build = _slot_onehot(route_ref, tm)
    gate_m = build([route[:, kk:kk + 1] for kk in range(TOP_K)])
    g_hi, g_lo = _split_bf16(gate_m)
    _wait_rows(tab_ref[0, 0, 3 * N_EXPERTS], runs(slot), _local_rows(tm))
    b_hi, b_lo = _split_bf16(buf_ref[slot])
    y = _dot(g_hi, b_hi) + _dot(g_lo, b_hi) + _dot(g_hi, b_lo)
    o_ref[...] = xm_ref[...] + gt_ref[0] * _rms(y, g_ref[...])


def _combine(tab, route, ys, xm, gt, g, seq, tm):
    n, d = xm.shape
    per_b = seq // tm
    nt = n // tm
    tok = lambda i: (i, 0)
    return pl.pallas_call(
        functools.partial(_combine_kernel, tm=tm),
        grid=(nt,),
        in_specs=[pl.BlockSpec((1, 1, LANE), lambda i: (i, 0, 0), memory_space=pltpu.SMEM),
                  pl.BlockSpec((1, 1, LANE), lambda i: (jnp.minimum(i + 1, nt - 1), 0, 0),
                               memory_space=pltpu.SMEM),
                  pl.BlockSpec((tm, LANE), tok), pl.BlockSpec(memory_space=pl.ANY),
                  pl.BlockSpec((tm, d), tok),
                  pl.BlockSpec((1, 1, d), lambda i: (i // per_b, 0, 0)),
                  pl.BlockSpec((1, d), lambda i: (0, 0))],
        out_specs=pl.BlockSpec((tm, d), tok),
        out_shape=jax.ShapeDtypeStruct((n, d), F32),
        scratch_shapes=[pltpu.VMEM((2, _local_rows(tm), d), F32), pltpu.SemaphoreType.DMA((2,))],
        compiler_params=_cparams(("arbitrary",)),
        name="moe_combine",
    )(tab, tab, route, ys, xm, gt, g.reshape(1, d))


def _pad_heads(w, width, slot=LANE):
    lead = w.shape[:-1]
    w = w.reshape(lead + (H_B, width))
    w = jnp.pad(w, [(0, 0)] * len(lead) + [(0, 0), (0, slot - width)])
    return w.reshape(lead + (H_B * slot,))


def _layer_weights(l, w_in, w_gla_gate, b_gla_gate, g_gla_norm, w_rg_a, w_rg_x, w_out, w_router, b_router):
    d = D_MODEL
    wi = w_in[l]
    o = 3 * W_A
    qb = wi[:, o:o + H_B * DK_B]
    kb = wi[:, o + 192:o + 384]
    vb = wi[:, o + 384:o + 768]
    gb = wi[:, o + 768:o + 1152]
    rb = wi[:, o + 1152:o + 1168]
    xy = wi[:, o + 1168:]
    w_in_p = jnp.concatenate(
        [wi[:, :o], _pad_heads(qb, DK_B, HK_B), _pad_heads(kb, DK_B, HK_B), _pad_heads(vb, DV_B),
         _pad_heads(gb, DV_B), jnp.pad(rb, ((0, 0), (0, LANE - GATE_RANK_B))), xy], axis=1).astype(BF16)
    wg_p = jnp.pad(_pad_heads(w_gla_gate[l], DK_B, HK_B), ((0, LANE - GATE_RANK_B), (0, 0))).astype(BF16)
    bg_p = _pad_heads(b_gla_gate[l].reshape(1, -1), DK_B, HK_B)
    gn_p = jnp.tile(jnp.pad(g_gla_norm[l], (0, LANE - DV_B)), H_B).reshape(1, HP_B)
    wa_bd = jax.scipy.linalg.block_diag(*[w_rg_a[l, i] for i in range(NBLK_C)]).astype(BF16)
    wx_bd = jax.scipy.linalg.block_diag(*[w_rg_x[l, i] for i in range(NBLK_C)]).astype(BF16)
    wo = w_out[l]
    wo_a = wo[:W_A].astype(BF16)
    wo_b = jnp.pad(wo[W_A:W_A + W_B].reshape(H_B, DV_B, d), ((0, 0), (0, LANE - DV_B), (0, 0)))
    wo_b = wo_b.reshape(HP_B, d).astype(BF16)
    wo_c = wo[W_A + W_B:].astype(BF16)
    wr = jnp.pad(w_router[l], ((0, 0), (0, LANE - N_EXPERTS)))
    wr_hi = wr.astype(BF16)
    wr_lo = (wr - wr_hi.astype(F32)).astype(BF16)
    br = jnp.pad(b_router[l], (0, LANE - N_EXPERTS), constant_values=ROUTER_PAD_BIAS).reshape(1, LANE)
    return dict(w_in_p=w_in_p, wg_p=wg_p, bg_p=bg_p, gn_p=gn_p, wa_bd=wa_bd, wx_bd=wx_bd,
                wo_a=wo_a, wo_b=wo_b, wo_c=wo_c, wr_hi=wr_hi, wr_lo=wr_lo, br=br)


def _pad_state(s):
    st = jnp.swapaxes(s, -1, -2)
    st = jnp.pad(st, ((0, 0), (0, 0), (0, LANE - DV_B), (0, HK_B - DK_B)))
    z = jnp.zeros_like(st)
    odd = (jnp.arange(H_B) % 2 == 1)[None, :, None, None]
    return jnp.concatenate([jnp.where(odd, z, st), jnp.where(odd, st, z)], axis=-1)


def _unpad_state(st):
    odd = (jnp.arange(H_B) % 2 == 1)[None, :, None, None]
    half = jnp.where(odd, st[..., HK_B:], st[..., :HK_B])
    return jnp.swapaxes(half[:, :, :DV_B, :DK_B], -1, -2)


def kernel(x_prompt, x_sample, c_prompt, c_sample, cache_k_sb, cache_v_sb, state_gla, state_conv, state_lru, w_ada, b_ada, g_pre_mix, g_post_mix, g_pre_ff, g_post_ff, w_in, w_gla_gate, b_gla_gate, g_gla_norm, w_conv, b_conv, w_rg_a, b_rg_a, w_rg_x, b_rg_x, lru_lambda, w_out, w_router, b_router, w_gate_up, b_gate_up, w_down, b_down):
    depth = w_ada.shape[0]
    d = D_MODEL
    groups = []
    for x, past in ((x_prompt, False), (x_sample, True)):
        b, t, _ = x.shape
        groups.append(dict(b=b, t=t, tm=min(TOKEN_TILE, t), n=b * t, past=past, x=x.reshape(b * t, d)))
    n_all = sum(g["n"] for g in groups)
    bm = MOE_BLOCK_ROWS
    n_tiles = sum(g["n"] // g["tm"] for g in groups)
    nb = -(-(n_all * TOP_K + n_tiles * N_EXPERTS * (SUBLANE - 1)) // bm) + N_EXPERTS
    n_pad = nb * bm

    nb_rows = sum(g["b"] for g in groups)
    c_all = jnp.concatenate([c_prompt, c_sample], axis=0)
    r_pad = -(-nb_rows // SUBLANE) * SUBLANE
    c_all = jnp.pad(c_all, ((0, r_pad - nb_rows), (0, 0)))
    mod = _modulation(c_all, w_ada, b_ada)

    st =[[[] for _ in range(5)] for _ in groups]
    for l in range(depth):
        lw = _layer_weights(l, w_in, w_gla_gate, b_gla_gate, g_gla_norm, w_rg_a, w_rg_x, w_out,
                            w_router, b_router)
        row0 = 0
        per_group = []
        for gi, g in enumerate(groups):
            b, t, tm, n = g["b"], g["t"], g["tm"], g["n"]
            m = mod[l, row0:row0 + b].reshape(b, 1, 6 * d)
            row0 += b
            sh_m, sc_m, gt_m, sh_f, sc_f, gt_f = [m[:, :, j * d:(j + 1) * d] for j in range(6)]
            ka, va, qab, kab, vab, gla_in, lru_in = _premix(g["x"], g_pre_mix[l], sc_m, sh_m, lw["w_in_p"], t, tm)
            q3 = qab.reshape(b, t, W_A)
            k3 = kab.reshape(b, t, W_A)
            v3 = vab.reshape(b, t, W_A)
            if g["past"]:
                n_past = cache_k_sb.shape[2]
                padk = ((0, 0), (0, -t % ATTN_KEY_TILE), (0, 0))
                oa = _attention(q3, jnp.pad(k3, padk), jnp.pad(v3, padk),
                                cache_k_sb.reshape(depth, b, n_past, W_A), cache_v_sb.reshape(depth, b, n_past, W_A),
                                layer=l)
                s0 = state_gla[l]
                buf0 = state_conv[l]
                h0 = state_lru[l]
            else:
                oa = _attention(q3, k3, v3)
                s0 = jnp.zeros((b, H_B, DK_B, DV_B), F32)
                buf0 = jnp.zeros((b, CONV_W - 1, W_C), F32)
                h0 = jnp.zeros((b, W_C), F32)
            oa = oa.reshape(n, W_A)
            ob, s_new = _gla(gla_in, lw["wg_p"], lw["bg_p"], lw["gn_p"], _pad_state(s0), t)
            buf0p = jnp.pad(buf0, ((0, 0), (SUBLANE - (CONV_W - 1), 0), (0, 0)))
            oc, conv_new, h_new = _lru(lru_in, w_conv[l], b_conv[l].reshape(1, -1), lw["wa_bd"],
                                       b_rg_a[l].reshape(1, -1), lw["wx_bd"], b_rg_x[l].reshape(1, -1),
                                       lru_lambda[l].reshape(1, -1), buf0p, h0.reshape(b, 1, W_C), t)
            xm, h2, route, stat = _postmix(g["x"], oa, ob, oc, lw["wo_a"], lw["wo_b"], lw["wo_c"],
                                           g_post_mix[l], g_pre_ff[l], gt_m, sc_f, sh_f,
                                           lw["wr_hi"], lw["wr_lo"], lw["br"], t, tm)
            per_group.append(dict(xm=xm, h2=h2, route=route, stat=stat, gt_f=gt_f))
            new = (ka.reshape(b, t, W_A // HD_A, HD_A), va.reshape(b, t, W_A // HD_A, HD_A),
                   _unpad_state(s_new), conv_new[:, SUBLANE - (CONV_W - 1):], h_new.reshape(b, W_C))
            for j in range(5):
                st[gi][j].append(new[j])

        stat = jnp.concatenate([p["stat"].reshape(-1, SUBLANE, LANE)[:, :2, :N_EXPERTS] for p in per_group],
                               axis=0).astype(jnp.int32)
        run_rows, run_loc = stat[:, 0], stat[:, 1]
        before = jnp.cumsum(run_rows, axis=0) - run_rows
        counts = jnp.sum(run_rows, axis=0)
        padded = (counts + bm - 1) // bm * bm
        pad_end = jnp.cumsum(padded)
        pad_start = pad_end - padded
        block_start = jnp.arange(nb, dtype=jnp.int32) * bm
        block_e = jnp.minimum(jnp.sum((pad_end[None, :] <= block_start[:, None]).astype(jnp.int32), axis=1),
                              N_EXPERTS - 1)
        nvalid = (pad_end[-1] // bm).astype(jnp.int32).reshape(1)
        tile_rows = jnp.broadcast_to(jnp.sum(run_rows, axis=1, keepdims=True), run_rows.shape)
        tab = jnp.concatenate([run_rows, run_loc, pad_start[None, :] + before, tile_rows],
                              axis=1).reshape(-1, 1, LANE)
        tile0 = 0
        for gi, g in enumerate(groups):
            nt = g["n"] // g["tm"]
            per_group[gi]["tab"] = tab[tile0:tile0 + nt]
            tile0 += nt
        zeros32 = jnp.zeros_like(counts)
        tail = jnp.concatenate([padded - counts, zeros32, pad_start + counts,
                                jnp.broadcast_to(nvalid, (N_EXPERTS,))]).reshape(1, 1, LANE)
        first, second = per_group
        xs = _dispatch(first["tab"], tail, first["route"], first["h2"], None, n_pad, groups[0]["tm"],
                       later=second["tab"], later_tm=groups[1]["tm"])
        xs = _dispatch(second["tab"], tail, second["route"], second["h2"], xs, n_pad, groups[1]["tm"])
        ys = _experts(l, block_e, nvalid, xs, w_gate_up, b_gate_up, w_down, b_down)
        for gi, g in enumerate(groups):
            p = per_group[gi]
            g["x"] = _combine(p["tab"], p["route"], ys, p["xm"], p["gt_f"], g_post_ff[l], g["t"], g["tm"])

    outs = [g["x"].reshape(g["b"], g["t"], d) for g in groups]
    for gi in range(len(groups)):
        outs.extend(jnp.stack(st[gi][j]) for j in range(5))
    return tuple(outs)
```

```python
import functools
import math

import numpy as np
import jax
import jax.numpy as jnp
from jax import lax
from jax.experimental import pallas as pl
from jax.experimental.pallas import tpu as pltpu

F32 = jnp.float32
BF16 = jnp.bfloat16

D_MODEL = 1024
EPS = 1e-6
HD_A = 64
W_A = 384
H_B = 4
DK_B = 48
DV_B = 96
W_B = 384
GATE_RANK_B = 16
GATE_TEMP_B = 16.0
W_C = 256
NBLK_C = 4
BW_C = 64
CONV_W = 4
RG_C = 8.0
N_EXPERTS = 32
TOP_K = 4
D_FF = 1024
SWIGLU_ALPHA = 1.702
SWIGLU_LIMIT = 7.0
LOG_WEIGHT_FLOOR = -120.0

LANE = 128
SUBLANE = 8
V7X_VMEM_BYTES = 64 * 1024 * 1024
VMEM_LIMIT = V7X_VMEM_BYTES * 7 // 8
ROUTER_PAD_BIAS = -1e30

HP_B = H_B * LANE
HK_B = LANE // 2
HPK_B = H_B * HK_B
OFF_GLA = 3 * W_A
GLA_W = 2 * HPK_B + 2 * HP_B + LANE
OFF_LRU = OFF_GLA + GLA_W
IN_PAD = OFF_LRU + 2 * W_C

TOKEN_TILE = 256
ATTN_TILE = 256
ATTN_KEY_TILE = 256
ATTN_CHAINS = 8
GLA_CHUNK = 128
GLA_CHUNKS_PER_STEP = 2
LRU_CHUNK = 256
MOE_BLOCK_ROWS = 512
RUN_BIG_ROWS = 64


def _cparams(sem):
    return pltpu.CompilerParams(dimension_semantics=sem, vmem_limit_bytes=VMEM_LIMIT)


def _sigmoid(x):
    return 1.0 / (1.0 + jnp.exp(-x))


def _log_sigmoid(x):
    return jnp.minimum(x, 0.0) - jnp.log(1.0 + jnp.exp(-jnp.abs(x)))


def _split_bf16(x):
    hi = x.astype(BF16)
    lo = (x - hi.astype(F32)).astype(BF16)
    return hi, lo


def _dot(a, b):
    return jnp.dot(a, b, preferred_element_type=F32)


def _dot_nt(a, b):
    return lax.dot_general(a, b, (((1,), (1,)), ((), ())), preferred_element_type=F32)


def _rms(x, g):
    return x * lax.rsqrt(jnp.mean(x * x, axis=-1, keepdims=True) + EPS) * g


def _per_batch(f, x, *m_refs):
    tm, d = x.shape
    nbb = m_refs[0].shape[0]
    return f(x.reshape(nbb, tm // nbb, d), *[m[...] for m in m_refs]).reshape(tm, d)


def _batch_spec(tm, seq, d):
    assert tm % seq == 0 or seq % tm == 0
    if tm >= seq:
        return pl.BlockSpec((tm // seq, 1, d), lambda i: (i, 0, 0))
    per_b = seq // tm
    return pl.BlockSpec((1, 1, d), lambda i: (i // per_b, 0, 0))


def _mod_kernel(c_ref, w_ref, b_ref, o_ref):
    c = c_ref[...]
    a = (c * _sigmoid(c)).astype(BF16)
    o_ref[0] = _dot(a, w_ref[0].astype(BF16)) + b_ref[0]


def _modulation(c_all, w_ada, b_ada):
    depth, d, n = w_ada.shape
    r = c_all.shape[0]
    tn = n // 4
    return pl.pallas_call(
        _mod_kernel,
        grid=(depth, n // tn),
        in_specs=[pl.BlockSpec((r, d), lambda l, j: (0, 0)),
                  pl.BlockSpec((1, d, tn), lambda l, j: (l, 0, j)),
                  pl.BlockSpec((1, 1, tn), lambda l, j: (l, 0, j))],
        out_specs=pl.BlockSpec((1, r, tn), lambda l, j: (l, 0, j)),
        out_shape=jax.ShapeDtypeStruct((depth, r, n), F32),
        compiler_params=_cparams(("arbitrary", "arbitrary")),
        name="adaln_mod",
    )(c_all, w_ada, b_ada.reshape(depth, 1, n))


def _premix_kernel(x_ref, g_ref, sc_ref, sh_ref, w_ref,
                   ka_ref, va_ref, qab_ref, kab_ref, vab_ref, gla_ref, lru_ref):
    h = _per_batch(lambda y, sc, sh: y * (1.0 + sc) + sh, _rms(x_ref[...], g_ref[...]), sc_ref, sh_ref)
    r = _dot(h.astype(BF16), w_ref[...])
    ka = r[:, W_A:2 * W_A]
    va = r[:, 2 * W_A:3 * W_A]
    ka_ref[...] = ka
    va_ref[...] = va
    qab_ref[...] = (r[:, 0:W_A] * (HD_A ** -0.5)).astype(BF16)
    kab_ref[...] = ka.astype(BF16)
    vab_ref[...] = va.astype(BF16)
    gla_ref[...] = r[:, OFF_GLA:OFF_LRU]
    lru_ref[...] = r[:, OFF_LRU:IN_PAD]


def _premix(x2, g, sc, sh, w_in_p, seq, tm):
    n, d = x2.shape
    tok = lambda i: (i, 0)
    bat = _batch_spec(tm, seq, d)
    const = lambda i: (0, 0)
    outs = [((n, W_A), F32), ((n, W_A), F32), ((n, W_A), BF16), ((n, W_A), BF16), ((n, W_A), BF16),
            ((n, GLA_W), F32), ((n, 2 * W_C), F32)]
    return pl.pallas_call(
        _premix_kernel,
        grid=(n // tm,),
        in_specs=[pl.BlockSpec((tm, d), tok), pl.BlockSpec((1, d), const), bat, bat,
                  pl.BlockSpec((d, IN_PAD), const)],
        out_specs=[pl.BlockSpec((tm, s[1]), tok) for s, _ in outs],
        out_shape=[jax.ShapeDtypeStruct(s, t) for s, t in outs],
        compiler_params=_cparams(("arbitrary",)),
        name="premix_proj",
    )(x2, g.reshape(1, d), sc, sh, w_in_p)


def _attn_kernel(q_ref, k_ref, v_ref, *rest, tq, tk, off, has_past, nsub, nbat):
    if has_past:
        kp_ref, vp_ref, u_ref, o_ref, acc_ref, c_ref = rest
    else:
        u_ref, o_ref, acc_ref, c_ref = rest
    i = pl.program_id(2)
    lane = lax.broadcasted_iota(jnp.int32, (1, LANE), 1)
    head_lanes = (lane < HD_A, lane >= HD_A)
    acc_ref[...] = jnp.zeros_like(acc_ref)
    c_ref[...] = jnp.zeros_like(c_ref)
    row = lax.broadcasted_iota(jnp.int32, (tq, tk), 0)
    col = lax.broadcasted_iota(jnp.int32, (tq, tk), 1)
    u = u_ref[...]
    chains = [(bi, a) for bi in range(nbat) for a in range(nsub)]
    qh, qpos0, nk = [], [], []
    for bi, a in chains:
        q = q_ref[bi, a * tq:(a + 1) * tq, :]
        qz = jnp.zeros_like(q)
        qh.append(tuple(jnp.where(m, q, qz) for m in head_lanes))
        qpos0.append(off + (i * nsub + a) * tq)
        nk.append((qpos0[-1] + tq - 1 + tk - 1) // tk)

    def step(jj, masked):
        cmax = jnp.float32(-jnp.inf)
        for a, (bi, _) in enumerate(chains):
            j = nk[a] - 1 - jj
            live = j >= 0
            ks = pl.multiple_of(jnp.maximum(j, 0) * tk, tk)
            if has_past and not masked:
                kb = kp_ref[0, bi, pl.ds(ks, tk), :].astype(BF16)
                vb = vp_ref[0, bi, pl.ds(ks, tk), :].astype(BF16)
            else:
                kb = k_ref[bi, pl.ds(pl.multiple_of(ks - off, tk), tk), :]
                vb = v_ref[bi, pl.ds(pl.multiple_of(ks - off, tk), tk), :]
            vz = jnp.zeros_like(vb)
            mask = (col + ks) < (row + qpos0[a])
            for h in range(2):
                s = _dot_nt(qh[a][h], kb)
                lf = -(jnp.maximum(s, 0.0) + jnp.log(1.0 + jnp.exp(-jnp.abs(s))))
                lb = s + lf
                if masked:
                    lf = jnp.where(mask, lf, 0.0)
                hi, lo = _split_bf16(lf)
                cr = _dot(jnp.concatenate([hi, lo], axis=1), u)
                c = c_ref[a, h]
                cfull = jnp.concatenate([c] * (tk // LANE), axis=1)
                w = jnp.exp(lb + cr[:, :tk] + cfull)
                if masked:
                    w = jnp.where(mask, w, 0.0)
                vh = jnp.where(head_lanes[h], vb, vz)
                pv = _dot(w.astype(BF16), vh)
                rs = cr[:, tk:]
                if not masked:
                    pv = jnp.where(live, pv, 0.0)
                    rs = jnp.where(live, rs, 0.0)
                acc_ref[a] += pv
                c_ref[a, h] = c + rs
            cmax = jnp.maximum(cmax, jnp.where(j > 0, jnp.max(c_ref[a]), -jnp.inf))
        return cmax > LOG_WEIGHT_FLOOR

    n_masked = max(1, tq // tk)
    for jj in range(n_masked):
        go = step(jj, True)

    def body(carry):
        jj, _ = carry
        return jj + 1, step(jj, False)

    lax.while_loop(lambda carry: jnp.logical_and(carry[0] < nk[-1], carry[1]), body, (jnp.int32(n_masked), go))
    for c, (bi, a) in enumerate(chains):
        o_ref[bi, a * tq:(a + 1) * tq, :] = acc_ref[c].astype(o_ref.dtype)


def _attn_umat(tk):
    jp = np.arange(tk)[:, None]
    j = np.arange(tk)[None, :]
    u = np.concatenate([(jp > j).astype(np.float32), np.ones((tk, LANE), np.float32)], axis=1)
    return jnp.asarray(np.concatenate([u, u], axis=0), BF16)


def _attention(q, k, v, kp=None, vp=None, layer=0):
    b, tq_all, _ = q.shape
    tc = k.shape[1]
    has_past = kp is not None
    off = kp.shape[2] if has_past else 0
    tq = min(ATTN_TILE, tq_all)
    tk = ATTN_KEY_TILE
    assert tc % tk == 0 and tq_all % tq == 0 and tq_all <= tc
    assert off % tk == 0 and (tk % tq == 0 or tq % tk == 0)
    assert not has_past or (tq == tq_all and tq_all <= tk)
    nsub = math.gcd(ATTN_CHAINS, tq_all // tq)
    nbat = math.gcd(ATTN_CHAINS // nsub, b)
    kern = functools.partial(_attn_kernel, tq=tq, tk=tk, off=off, has_past=has_past, nsub=nsub, nbat=nbat)
    cur = pl.BlockSpec((nbat, tc, LANE), lambda bi, hp, i: (bi, 0, hp))
    in_specs = [pl.BlockSpec((nbat, nsub * tq, LANE), lambda bi, hp, i: (bi, i, hp)), cur, cur]
    args = [q, k, v]
    if has_past:
        past = pl.BlockSpec((1, nbat, off, LANE), lambda bi, hp, i: (layer, bi, 0, hp))
        in_specs += [past, past]
        args += [kp, vp]
    in_specs.append(pl.BlockSpec((2 * tk, tk + LANE), lambda bi, hp, i: (0, 0)))
    args.append(_attn_umat(tk))
    return pl.pallas_call(
        kern,
        grid=(b // nbat, W_A // LANE, tq_all // (nsub * tq)),
        in_specs=in_specs,
        out_specs=pl.BlockSpec((nbat, nsub * tq, LANE), lambda bi, hp, i: (bi, i, hp)),
        out_shape=jax.ShapeDtypeStruct((b, tq_all, W_A), BF16),
        scratch_shapes=[pltpu.VMEM((nbat * nsub, tq, LANE), F32), pltpu.VMEM((nbat * nsub, 2, tq, LANE), F32)],
        compiler_params=_cparams(("arbitrary", "arbitrary", "arbitrary")),
        name="stick_breaking_attn",
    )(*args)


def _gla_levels(c):
    return int(math.log2(c))


def _gla_mats(c):
    t = np.arange(c)[:, None]
    s = np.arange(c)[None, :]
    mats = [(s <= t).astype(np.float32), (s > t).astype(np.float32)]
    for lv in range(_gla_levels(c)):
        m = 1 << lv
        ref = (t // (2 * m)) * (2 * m) + m - 1
        mats.append(((s > ref) & (s <= t)).astype(np.float32) - ((s > t) & (s <= ref)).astype(np.float32))
    m = np.concatenate(mats, axis=0)
    return jnp.asarray(np.concatenate([m, m], axis=1), BF16)


def _gla_kernel(in_ref, wg_ref, bg_ref, gn_ref, mall_ref, s0_ref, ob_ref, sout_ref, st_ref, *, C, G):
    ci = pl.program_id(1)

    @pl.when(ci == 0)
    def _():
        st_ref[...] = s0_ref[0]

    for sub in range(G):
        _gla_chunk(in_ref, wg_ref, bg_ref, gn_ref, mall_ref, ob_ref, st_ref, C, sub * C)

    @pl.when(ci == pl.num_programs(1) - 1)
    def _():
        sout_ref[0] = st_ref[...]


def _gla_chunk(in_ref, wg_ref, bg_ref, gn_ref, mall_ref, ob_ref, st_ref, C, r0):
    rows = slice(r0, r0 + C)
    q = in_ref[rows, 0:HPK_B] * (DK_B ** -0.5)
    k = in_ref[rows, HPK_B:2 * HPK_B]
    v = in_ref[rows, 2 * HPK_B:2 * HPK_B + HP_B]
    gb = in_ref[rows, 2 * HPK_B + HP_B:2 * HPK_B + 2 * HP_B]
    rb = in_ref[rows, 2 * HPK_B + 2 * HP_B:GLA_W]
    lg = _log_sigmoid(_dot(rb.astype(BF16), wg_ref[...]) + bg_ref[...]) * (1.0 / GATE_TEMP_B)
    hi, lo = _split_bf16(lg)
    dall = _dot(mall_ref[...], jnp.concatenate([hi, lo], axis=0))
    eb = jnp.exp(dall[0:C])
    elast = jnp.exp(dall[C:2 * C])
    eb_last = eb[C - 1:C, :]
    rowi = lax.broadcasted_iota(jnp.int32, (C, 1), 0)
    row = lax.broadcasted_iota(jnp.int32, (C, C), 0)
    col = lax.broadcasted_iota(jnp.int32, (C, C), 1)
    lane = lax.broadcasted_iota(jnp.int32, (1, LANE), 1)
    slot_half = (lane < HK_B, lane >= HK_B)
    qe = q * eb
    ke = k * elast
    kb = k.astype(BF16)
    vb = v.astype(BF16)
    zero = jnp.zeros_like(q)
    lv_q, lv_k = [], []
    for lv in range(_gla_levels(C)):
        e = jnp.exp(-jnp.abs(dall[(2 + lv) * C:(3 + lv) * C]))
        second = ((rowi >> lv) & 1) == 1
        lv_q.append(jnp.where(second, q * e, zero))
        lv_k.append(jnp.where(second, zero, k * e).astype(BF16))
    outs = []
    for h in range(H_B):
        sl = slice(h * LANE, (h + 1) * LANE)
        sk = slice((h // 2) * LANE, (h // 2 + 1) * LANE)

        def own(x):
            return jnp.where(slot_half[h % 2], x[:, sk], 0.0).astype(BF16)

        att = jnp.where(row == col, _dot_nt(own(q), kb[:, sk]), 0.0)
        for lv in range(_gla_levels(C)):
            same = (row >> (lv + 1)) == (col >> (lv + 1))
            att = att + jnp.where(same, _dot_nt(own(lv_q[lv]), lv_k[lv][:, sk]), 0.0)
        st = st_ref[h]
        o = _dot(att.astype(BF16), vb[:, sl]) + _dot_nt(own(qe), st.astype(BF16))
        st_ref[h] = st * eb_last[:, sk] + _dot(v[:, sl].T.astype(BF16), own(ke))
        ms = jnp.sum(o * o, axis=-1, keepdims=True) * (1.0 / DV_B)
        on = o * lax.rsqrt(ms + EPS) * gn_ref[:, sl]
        g = gb[:, sl]
        outs.append((on * (g * _sigmoid(g))).astype(BF16))
    ob_ref[rows, :] = jnp.concatenate(outs, axis=1)


def _gla(gla_in, wg_p, bg_p, gn_p, s0t, seq):
    n = gla_in.shape[0]
    b = n // seq
    c = min(GLA_CHUNK, seq)
    g = math.gcd(GLA_CHUNKS_PER_STEP, seq // c)
    per_b = seq // (c * g)
    mall = _gla_mats(c)
    const2 = lambda bi, ci: (0, 0)
    return pl.pallas_call(
        functools.partial(_gla_kernel, C=c, G=g),
        grid=(b, per_b),
        in_specs=[pl.BlockSpec((c * g, GLA_W), lambda bi, ci: (bi * per_b + ci, 0)),
                  pl.BlockSpec((LANE, HPK_B), const2), pl.BlockSpec((1, HPK_B), const2),
                  pl.BlockSpec((1, HP_B), const2), pl.BlockSpec(mall.shape, const2),
                  pl.BlockSpec((1, H_B, LANE, LANE), lambda bi, ci: (bi, 0, 0, 0))],
        out_specs=[pl.BlockSpec((c * g, HP_B), lambda bi, ci: (bi * per_b + ci, 0)),
                   pl.BlockSpec((1, H_B, LANE, LANE), lambda bi, ci: (bi, 0, 0, 0))],
        out_shape=[jax.ShapeDtypeStruct((n, HP_B), BF16),
                   jax.ShapeDtypeStruct((b, H_B, LANE, LANE), F32)],
        scratch_shapes=[pltpu.VMEM((H_B, LANE, LANE), F32)],
        compiler_params=_cparams(("arbitrary", "arbitrary")),
        name="gla_chunked",
    )(gla_in, wg_p, bg_p, gn_p, mall, s0t)


def _lru_kernel(in_ref, cw_ref, cb_ref, wa_ref, ba_ref, wx_ref, bx_ref, lam_ref, buf0_ref, h0_ref,
                oc_ref, conv_ref, hout_ref, xp_ref, hc_ref, *, C):
    ci = pl.program_id(1)

    @pl.when(ci == 0)
    def _():
        xp_ref[0:SUBLANE] = buf0_ref[0]
        hc_ref[...] = h0_ref[0]

    x = in_ref[:, 0:W_C]
    y = in_ref[:, W_C:2 * W_C]
    xp_ref[SUBLANE:SUBLANE + C] = x
    xc = cb_ref[...]
    for j in range(CONV_W):
        xc = xc + xp_ref[pl.ds(SUBLANE - (CONV_W - 1) + j, C), :] * cw_ref[j:j + 1, :]
    xcb = xc.astype(BF16)
    r = _sigmoid(_dot(xcb, wa_ref[...]) + ba_ref[...])
    gi = _sigmoid(_dot(xcb, wx_ref[...]) + bx_ref[...])
    log_a = RG_C * r * _log_sigmoid(lam_ref[...])
    a = jnp.exp(log_a)
    u = jnp.sqrt(1.0 - jnp.exp(2.0 * log_a)) * (gi * xc)
    rowi = lax.broadcasted_iota(jnp.int32, (C, 1), 0)
    d = 1
    while d < C:
        keep = rowi >= d
        a_s = pltpu.roll(a, d, axis=0)
        u_s = pltpu.roll(u, d, axis=0)
        u = jnp.where(keep, a * u_s + u, u)
        a = jnp.where(keep, a * a_s, a)
        d *= 2
    hseq = u + a * hc_ref[...]
    hc_ref[...] = hseq[C - 1:C, :]
    gelu = 0.5 * y * (1.0 + jnp.tanh(math.sqrt(2.0 / math.pi) * (y + 0.044715 * (y * y * y))))
    oc_ref[...] = (hseq * gelu).astype(BF16)
    tail = xp_ref[C:C + SUBLANE]
    xp_ref[0:SUBLANE] = tail

    @pl.when(ci == pl.num_programs(1) - 1)
    def _():
        conv_ref[0] = tail
        hout_ref[0] = hseq[C - 1:C, :]


def _lru(lru_in, cw, cb, wa_bd, ba, wx_bd, bx, lam, buf0, h0, seq):
    n = lru_in.shape[0]
    b = n // seq
    c = min(LRU_CHUNK, seq)
    per_b = seq // c
    const2 = lambda bi, ci: (0, 0)
    vec = pl.BlockSpec((1, W_C), const2)
    return pl.pallas_call(
        functools.partial(_lru_kernel, C=c),
        grid=(b, per_b),
        in_specs=[pl.BlockSpec((c, 2 * W_C), lambda bi, ci: (bi * per_b + ci, 0)),
                  pl.BlockSpec((CONV_W, W_C), const2), vec,
                  pl.BlockSpec((W_C, W_C), const2), vec, pl.BlockSpec((W_C, W_C), const2), vec, vec,
                  pl.BlockSpec((1, SUBLANE, W_C), lambda bi, ci: (bi, 0, 0)),
                  pl.BlockSpec((1, 1, W_C), lambda bi, ci: (bi, 0, 0))],
        out_specs=[pl.BlockSpec((c, W_C), lambda bi, ci: (bi * per_b + ci, 0)),
                   pl.BlockSpec((1, SUBLANE, W_C), lambda bi, ci: (bi, 0, 0)),
                   pl.BlockSpec((1, 1, W_C), lambda bi, ci: (bi, 0, 0))],
        out_shape=[jax.ShapeDtypeStruct((n, W_C), BF16),
                   jax.ShapeDtypeStruct((b, SUBLANE, W_C), F32),
                   jax.ShapeDtypeStruct((b, 1, W_C), F32)],
        scratch_shapes=[pltpu.VMEM((c + SUBLANE, W_C), F32), pltpu.VMEM((1, W_C), F32)],
        compiler_params=_cparams(("arbitrary", "arbitrary")),
        name="conv_rglru",
    )(lru_in, cw, cb, wa_bd, ba, wx_bd, bx, lam, buf0, h0)


def _postmix_kernel(x_ref, oa_ref, ob_ref, oc_ref, wa_ref, wb_ref, wc_ref, gpm_ref, gpf_ref,
                    gt_ref, sc_ref, sh_ref, wr_hi_ref, wr_lo_ref, br_ref, tri_ref, ustrict_ref,
                    xm_ref, h2_ref, route_ref, stat_ref, *, tm):
    y =_dot(oa_ref[...], wa_ref[...]) + _dot(ob_ref[...], wb_ref[...]) + _dot(oc_ref[...], wc_ref[...])
    xm = x_ref[...] + _per_batch(lambda r, gt: gt * r, _rms(y, gpm_ref[...]), gt_ref)
    xm_ref[...] = xm
    h2 = _per_batch(lambda y, sc, sh: y * (1.0 + sc) + sh, _rms(xm, gpf_ref[...]), sc_ref, sh_ref)
    h2_ref[...] = h2
    hi, lo = _split_bf16(h2)
    wh = wr_hi_ref[...]
    p_hi = _dot(hi, jnp.concatenate([wh, wr_lo_ref[...]], axis=1))
    logits = p_hi[:, :LANE] + _dot(lo, wh) + p_hi[:, LANE:] + br_ref[...]
    lane = lax.broadcasted_iota(jnp.int32, (tm, LANE), 1)
    lane_f = lane.astype(F32)
    neg = jnp.float32(-jnp.inf)
    vals, hots = [], []
    for _ in range(TOP_K):
        m = jnp.max(logits, axis=-1, keepdims=True)
        idx = jnp.min(jnp.where(logits == m, lane_f, float(LANE)), axis=-1, keepdims=True)
        hot = lane_f == idx
        logits = jnp.where(hot, neg, logits)
        vals.append(m)
        hots.append(hot)
    ex = [jnp.exp(vk - vals[0]) for vk in vals]
    inv = 1.0 / (ex[0] + ex[1] + ex[2] + ex[3])
    sel = jnp.zeros((tm, LANE), F32)
    for hot in hots:
        sel = jnp.where(hot, 1.0, sel)
    selb = sel.astype(BF16)
    tile_cnt = jnp.sum(sel, axis=0, keepdims=True)
    groups8 = jnp.floor((tile_cnt + (SUBLANE - 1.0)) * (1.0 / SUBLANE))
    g8b = jnp.broadcast_to(groups8, (SUBLANE, LANE)).astype(BF16)
    loc_start = _dot(g8b, ustrict_ref[...])[0:1] * float(SUBLANE)
    local = _dot(tri_ref[...], selb) + loc_start
    route = jnp.zeros((tm, LANE), F32)
    for kk in range(TOP_K):
        p_k = jnp.sum(jnp.where(hots[kk], local, 0.0), axis=-1, keepdims=True)
        route = jnp.where(lane == kk, ex[kk] * inv, route)
        route = jnp.where(lane == TOP_K + kk, p_k, route)
    route_ref[...] = route
    srow = lax.broadcasted_iota(jnp.int32, (SUBLANE, LANE), 0)
    stat_ref[...] = jnp.where(srow == 0, groups8 * float(SUBLANE), jnp.where(srow == 1, loc_start, 0.0))


def _postmix(x2, oa, ob, oc, wo_a, wo_b, wo_c, gpm, gpf, gt, sc, sh, wr_hi, wr_lo, br, seq, tm):
    n, d = x2.shape
    tok = lambda i: (i, 0)
    bat = _batch_spec(tm, seq, d)
    const = lambda i: (0, 0)
    tri = jnp.asarray(np.tril(np.ones((tm, tm), np.float32), -1), BF16)
    ustrict = jnp.asarray(np.triu(np.ones((LANE, LANE), np.float32), 1), BF16)
    full = lambda a: pl.BlockSpec(a.shape, const)
    nt = n // tm
    return pl.pallas_call(
        functools.partial(_postmix_kernel, tm=tm),
        grid=(nt,),
        in_specs=[pl.BlockSpec((tm, d), tok), pl.BlockSpec((tm, W_A), tok),
                  pl.BlockSpec((tm, HP_B), tok), pl.BlockSpec((tm, W_C), tok),
                  full(wo_a), full(wo_b), full(wo_c),
                  pl.BlockSpec((1, d), const), pl.BlockSpec((1, d), const),
                  bat, bat, bat,
                  full(wr_hi), full(wr_lo), pl.BlockSpec((1, LANE), const), full(tri), full(ustrict)],
        out_specs=[pl.BlockSpec((tm, d), tok), pl.BlockSpec((tm, d), tok),
                   pl.BlockSpec((tm, LANE), tok), pl.BlockSpec((SUBLANE, LANE), tok)],
        out_shape=[jax.ShapeDtypeStruct((n, d), F32), jax.ShapeDtypeStruct((n, d), F32),
                   jax.ShapeDtypeStruct((n, LANE), F32), jax.ShapeDtypeStruct((nt * SUBLANE, LANE), F32)],
        compiler_params=_cparams(("arbitrary",)),
        name="postmix_router",
    )(x2, oa, ob, oc, wo_a, wo_b, wo_c, gpm.reshape(1, d), gpf.reshape(1, d), gt, sc, sh,
      wr_hi, wr_lo, br, tri, ustrict)


def _run_copies(tab_ref, tm, make_copy, wait, tile=0):
    sizes = [1 << b for b in range(int(math.log2(tm)), int(math.log2(SUBLANE)) - 1, -1)]
    big = [sz for sz in sizes if sz >= RUN_BIG_ROWS]
    small = [sz for sz in sizes if sz < RUN_BIG_ROWS]

    def per_expert(e, carry):
        n = tab_ref[tile, 0, e]
        loc = tab_ref[tile, 0, N_EXPERTS + e]
        dst = tab_ref[tile, 0, 2 * N_EXPERTS + e]

        def pieces(szs):
            for sz in szs:
                done = n & ~(2 * sz - 1)

                @pl.when((n & sz) != 0)
                def _():
                    cp = make_copy(pl.multiple_of(loc + done, SUBLANE), pl.multiple_of(dst + done, SUBLANE), sz)
                    if wait:
                        cp.wait()
                    else:
                        cp.start()

        if big:
            pl.when(n >= RUN_BIG_ROWS)(lambda: pieces(big))
        pieces(small)
        return carry

    lax.fori_loop(0, N_EXPERTS, per_expert, 0)


def _wait_rows(total, make_copy, max_rows):
    top = int(math.log2(max_rows))
    for sz in [1 << b for b in range(top, int(math.log2(SUBLANE)) - 1, -1)]:
        @pl.when((total & sz) != 0)
        def _():
            make_copy(0, 0, sz).wait()


def _local_rows(tm):
    return tm * TOP_K + N_EXPERTS * SUBLANE


def _slot_onehot(route_ref, tm):
    nloc = _local_rows(tm)
    pos = lax.broadcasted_iota(jnp.int32, (tm, nloc), 1)
    route = route_ref[...]
    slot = [route[:, TOP_K + kk:TOP_K + kk + 1].astype(jnp.int32) for kk in range(TOP_K)]

    def build(values):
        m = jnp.zeros((tm, nloc), F32)
        for kk in range(TOP_K):
            m = jnp.where(slot[kk] == pos, values[kk], m)
        return m

    return route, build


def _dispatch_kernel(tab_ref, tabp_ref, tail_ref, route_ref, h_ref, *rest, tm, first, later_tile):
    i = pl.program_id(0)
    slot = i % 2
    if first:
        later_ref, xs_ref, sorted_ref, sem, zero_ref = rest
        bm = MOE_BLOCK_ROWS

        @pl.when(i == 0)
        def _():
            zero_ref[...] = jnp.zeros_like(zero_ref)
            fill = lambda loc, dst, sz: pltpu.make_async_copy(zero_ref.at[pl.ds(0, sz)],
                                                              xs_ref.at[pl.ds(dst, sz)], sem.at[0])
            n_later, later_tm = later_ref.shape[0], later_tile

            def block_fill(wait):
                def body(g, carry):
                    cp = fill(0, pl.multiple_of(g * bm, bm), bm)
                    cp.wait() if wait else cp.start()
                    return carry
                lax.fori_loop(tail_ref[0, 0, 3 * N_EXPERTS], xs_ref.shape[0] // bm, body, 0)

            for wait in (False, True):
                _run_copies(tail_ref, bm, fill, wait)
                block_fill(wait)
                for ti in range(n_later):
                    _run_copies(later_ref, later_tm, fill, wait, tile=ti)
    else:
        _, xs_ref, sorted_ref, sem = rest
    _, build = _slot_onehot(route_ref, tm)
    perm = build([1.0] * TOP_K).astype(BF16)
    sorted_ref[slot] = lax.dot_general(perm, h_ref[...].astype(BF16), (((0,), (0,)), ((), ())),
                                       preferred_element_type=F32)

    def runs(s):
        return lambda loc, dst, sz: pltpu.make_async_copy(sorted_ref.at[s, pl.ds(loc, sz)],
                                                          xs_ref.at[pl.ds(dst, sz)], sem.at[s])

    _run_copies(tab_ref, tm, runs(slot), wait=False)

    @pl.when(i > 0)
    def _():
        _wait_rows(tabp_ref[0, 0, 3 * N_EXPERTS], runs(1 - slot), _local_rows(tm))

    @pl.when(i == pl.num_programs(0) - 1)
    def _():
        _wait_rows(tab_ref[0, 0, 3 * N_EXPERTS], runs(slot), _local_rows(tm))


def _dispatch(tab, tail, route, h2, xs, n_pad, tm, later=None, later_tm=None):
    n, d = h2.shape
    tok = lambda i: (i, 0)
    first = xs is None
    in_specs = [pl.BlockSpec((1, 1, LANE), lambda i: (i, 0, 0), memory_space=pltpu.SMEM),
                pl.BlockSpec((1, 1, LANE), lambda i: (jnp.maximum(i - 1, 0), 0, 0), memory_space=pltpu.SMEM),
                pl.BlockSpec((1, 1, LANE), lambda i: (0, 0, 0), memory_space=pltpu.SMEM),
                pl.BlockSpec((tm, LANE), tok), pl.BlockSpec((tm, d), tok)]
    scratch = [pltpu.VMEM((2, _local_rows(tm), d), F32), pltpu.SemaphoreType.DMA((2,))]
    args = [tab, tab, tail, route, h2]
    if first:
        scratch.append(pltpu.VMEM((MOE_BLOCK_ROWS, d), F32))
        in_specs.append(pl.BlockSpec(later.shape, lambda i: (0, 0, 0), memory_space=pltpu.SMEM))
        args.append(later)
    else:
        in_specs.append(pl.BlockSpec(memory_space=pl.ANY))
        args.append(xs)
    return pl.pallas_call(
        functools.partial(_dispatch_kernel, tm=tm, first=first, later_tile=later_tm),
        grid=(n // tm,),
        in_specs=in_specs,
        out_specs=pl.BlockSpec(memory_space=pl.ANY),
        out_shape=jax.ShapeDtypeStruct((n_pad, d), F32),
        scratch_shapes=scratch,
        input_output_aliases={} if first else {5: 0},
        compiler_params=_cparams(("arbitrary",)),
        name="moe_dispatch",
    )(*args)


def _expert_kernel(be_ref, nv_ref, x_ref, wgu_ref, bgu_ref, wd_ref, bd_ref, y_ref, wgu_bf, wd_bf):
    g = pl.program_id(0)

    @pl.when(jnp.logical_or(g == 0, be_ref[g] != be_ref[jnp.maximum(g - 1, 0)]))
    def _():
        wgu_bf[...] = wgu_ref[0, 0].astype(BF16)
        wd_bf[...] = wd_ref[0, 0].astype(BF16)

    @pl.when(g < nv_ref[0])
    def _():
        gu = _dot(x_ref[...].astype(BF16), wgu_bf[...]) + bgu_ref[0, 0]
        gt = jnp.minimum(gu[:, :D_FF], SWIGLU_LIMIT)
        up = jnp.clip(gu[:, D_FF:], -SWIGLU_LIMIT, SWIGLU_LIMIT)
        act = (up + 1.0) * (gt * _sigmoid(SWIGLU_ALPHA * gt))
        y_ref[...] = _dot(act.astype(BF16), wd_bf[...]) + bd_ref[0, 0]

    @pl.when(g >= nv_ref[0])
    def _():
        y_ref[...] = jnp.zeros_like(y_ref)


def _experts(l, block_e, nvalid, xs, wgu, bgu, wd, bd):
    n_pad, d = xs.shape
    depth = wgu.shape[0]
    bm = MOE_BLOCK_ROWS
    nb = n_pad // bm
    grid_spec = pltpu.PrefetchScalarGridSpec(
        num_scalar_prefetch=2,
        grid=(nb,),
        in_specs=[pl.BlockSpec((bm, d), lambda g, be, nv: (jnp.minimum(g, nv[0] - 1), 0)),
                  pl.BlockSpec((1, 1, d, 2 * D_FF), lambda g, be, nv: (l, be[g], 0, 0)),
                  pl.BlockSpec((1, 1, 1, 2 * D_FF), lambda g, be, nv: (l, be[g], 0, 0)),
                  pl.BlockSpec((1, 1, D_FF, d), lambda g, be, nv: (l, be[g], 0, 0)),
                  pl.BlockSpec((1, 1, 1, d), lambda g, be, nv: (l, be[g], 0, 0))],
        out_specs=pl.BlockSpec((bm, d), lambda g, be, nv: (g, 0)),
        scratch_shapes=[pltpu.VMEM((d, 2 * D_FF), BF16), pltpu.VMEM((D_FF, d), BF16)],
    )
    return pl.pallas_call(
        _expert_kernel,
        grid_spec=grid_spec,
        out_shape=jax.ShapeDtypeStruct((n_pad, d), F32),
        compiler_params=_cparams(("arbitrary",)),
        name="moe_experts",
    )(block_e, nvalid, xs, wgu, bgu.reshape(depth, N_EXPERTS, 1, -1), wd, bd.reshape(depth, N_EXPERTS, 1, -1))


def _combine_kernel(tab_ref, tabn_ref, route_ref, ys_ref, xm_ref, gt_ref, g_ref, o_ref, buf_ref, sem, *, tm):
    i = pl.program_id(0)
    slot = i % 2

    def runs(s):
        return lambda loc, dst, sz: pltpu.make_async_copy(ys_ref.at[pl.ds(dst, sz)],
                                                          buf_ref.at[s, pl.ds(loc, sz)], sem.at[s])

    @pl.when(i == 0)
    def _():
        buf_ref[...] = jnp.zeros_like(buf_ref)
        _run_copies(tab_ref, tm, runs(0), wait=False)

    @pl.when(i + 1 < pl.num_programs(0))
    def _():
        _run_copies(tabn_ref, tm, runs(1 - slot), wait=False)

    route, build = _slot_onehot(route_ref, tm)
    gate_m = build([route[:, kk:kk + 1] for kk in range(TOP_K)])
    g_hi, g_lo = _split_bf16(gate_m)
    _wait_rows(tab_ref[0, 0, 3 * N_EXPERTS], runs(slot), _local_rows(tm))
    b_hi, b_lo = _split_bf16(buf_ref[slot])
    y = _dot(g_hi, b_hi) + _dot(g_lo, b_hi) + _dot(g_hi, b_lo)
    o_ref[...] = xm_ref[...] + _per_batch(lambda r, gt: gt * r, _rms(y, g_ref[...]), gt_ref)


def _combine(tab, route, ys, xm, gt, g, seq, tm):
    n, d = xm.shape
    nt = n // tm
    tok = lambda i: (i, 0)
    return pl.pallas_call(
        functools.partial(_combine_kernel, tm=tm),
        grid=(nt,),
        in_specs=[pl.BlockSpec((1, 1, LANE), lambda i: (i, 0, 0), memory_space=pltpu.SMEM),
                  pl.BlockSpec((1, 1, LANE), lambda i: (jnp.minimum(i + 1, nt - 1), 0, 0),
                               memory_space=pltpu.SMEM),
                  pl.BlockSpec((tm, LANE), tok), pl.BlockSpec(memory_space=pl.ANY),
                  pl.BlockSpec((tm, d), tok),
                  _batch_spec(tm, seq, d),
                  pl.BlockSpec((1, d), lambda i: (0, 0))],
        out_specs=pl.BlockSpec((tm, d), tok),
        out_shape=jax.ShapeDtypeStruct((n, d), F32),
        scratch_shapes=[pltpu.VMEM((2, _local_rows(tm), d), F32), pltpu.SemaphoreType.DMA((2,))],
        compiler_params=_cparams(("arbitrary",)),
        name="moe_combine",
    )(tab, tab, route, ys, xm, gt, g.reshape(1, d))


def _pad_heads(w, width, slot=LANE):
    lead = w.shape[:-1]
    w = w.reshape(lead + (H_B, width))
    w = jnp.pad(w, [(0, 0)] * len(lead) + [(0, 0), (0, slot - width)])
    return w.reshape(lead + (H_B * slot,))


def _layer_weights(l, w_in, w_gla_gate, b_gla_gate, g_gla_norm, w_rg_a, w_rg_x, w_out, w_router, b_router):
    d = D_MODEL
    wi = w_in[l]
    o = 3 * W_A
    qb = wi[:, o:o + H_B * DK_B]
    kb = wi[:, o + 192:o + 384]
    vb = wi[:, o + 384:o + 768]
    gb = wi[:, o + 768:o + 1152]
    rb = wi[:, o + 1152:o + 1168]
    xy = wi[:, o + 1168:]
    w_in_p = jnp.concatenate(
        [wi[:, :o], _pad_heads(qb, DK_B, HK_B), _pad_heads(kb, DK_B, HK_B), _pad_heads(vb, DV_B),
         _pad_heads(gb, DV_B), jnp.pad(rb, ((0, 0), (0, LANE - GATE_RANK_B))), xy], axis=1).astype(BF16)
    wg_p = jnp.pad(_pad_heads(w_gla_gate[l], DK_B, HK_B), ((0, LANE - GATE_RANK_B), (0, 0))).astype(BF16)
    bg_p = _pad_heads(b_gla_gate[l].reshape(1, -1), DK_B, HK_B)
    gn_p = jnp.tile(jnp.pad(g_gla_norm[l], (0, LANE - DV_B)), H_B).reshape(1, HP_B)
    wa_bd = jax.scipy.linalg.block_diag(*[w_rg_a[l, i] for i in range(NBLK_C)]).astype(BF16)
    wx_bd = jax.scipy.linalg.block_diag(*[w_rg_x[l, i] for i in range(NBLK_C)]).astype(BF16)
    wo = w_out[l]
    wo_a = wo[:W_A].astype(BF16)
    wo_b = jnp.pad(wo[W_A:W_A + W_B].reshape(H_B, DV_B, d), ((0, 0), (0, LANE - DV_B), (0, 0)))
    wo_b = wo_b.reshape(HP_B, d).astype(BF16)
    wo_c = wo[W_A + W_B:].astype(BF16)
    wr = jnp.pad(w_router[l], ((0, 0), (0, LANE - N_EXPERTS)))
    wr_hi = wr.astype(BF16)
    wr_lo = (wr - wr_hi.astype(F32)).astype(BF16)
    br = jnp.pad(b_router[l], (0, LANE - N_EXPERTS), constant_values=ROUTER_PAD_BIAS).reshape(1, LANE)
    return dict(w_in_p=w_in_p, wg_p=wg_p, bg_p=bg_p, gn_p=gn_p, wa_bd=wa_bd, wx_bd=wx_bd,
                wo_a=wo_a, wo_b=wo_b, wo_c=wo_c, wr_hi=wr_hi, wr_lo=wr_lo, br=br)


def _pad_state(s):
    st = jnp.swapaxes(s, -1, -2)
    st = jnp.pad(st, ((0, 0), (0, 0), (0, LANE - DV_B), (0, HK_B - DK_B)))
    z = jnp.zeros_like(st)
    odd = (jnp.arange(H_B) % 2 == 1)[None, :, None, None]
    return jnp.concatenate([jnp.where(odd, z, st), jnp.where(odd, st, z)], axis=-1)


def _unpad_state(st):
    odd = (jnp.arange(H_B) % 2 == 1)[None, :, None, None]
    half = jnp.where(odd, st[..., HK_B:], st[..., :HK_B])
    return jnp.swapaxes(half[:, :, :DV_B, :DK_B], -1, -2)


def kernel(x_prompt, x_sample, c_prompt, c_sample, cache_k_sb, cache_v_sb, state_gla, state_conv, state_lru, w_ada, b_ada, g_pre_mix, g_post_mix, g_pre_ff, g_post_ff, w_in, w_gla_gate, b_gla_gate, g_gla_norm, w_conv, b_conv, w_rg_a, b_rg_a, w_rg_x, b_rg_x, lru_lambda, w_out, w_router, b_router, w_gate_up, b_gate_up, w_down, b_down):
    depth = w_ada.shape[0]
    d = D_MODEL
    groups = []
    for x, past in ((x_prompt, False), (x_sample, True)):
        b, t, _ = x.shape
        groups.append(dict(b=b, t=t, tm=min(TOKEN_TILE, b * t), n=b * t, past=past, x=x.reshape(b * t, d)))
    n_all = sum(g["n"] for g in groups)
    bm = MOE_BLOCK_ROWS
    n_tiles = sum(g["n"] // g["tm"] for g in groups)
    nb = -(-(n_all * TOP_K + n_tiles * N_EXPERTS * (SUBLANE - 1)) // bm) + N_EXPERTS
    n_pad = nb * bm

    nb_rows = sum(g["b"] for g in groups)
    c_all = jnp.concatenate([c_prompt, c_sample], axis=0)
    r_pad = -(-nb_rows // SUBLANE) * SUBLANE
    c_all = jnp.pad(c_all, ((0, r_pad - nb_rows), (0, 0)))
    mod = _modulation(c_all, w_ada, b_ada)

    st =[[[] for _ in range(5)] for _ in groups]
    for l in range(depth):
        lw = _layer_weights(l, w_in, w_gla_gate, b_gla_gate, g_gla_norm, w_rg_a, w_rg_x, w_out,
                            w_router, b_router)
        row0 = 0
        per_group = []
        for gi, g in enumerate(groups):
            b, t, tm, n = g["b"], g["t"], g["tm"], g["n"]
            m = mod[l, row0:row0 + b].reshape(b, 1, 6 * d)
            row0 += b
            sh_m, sc_m, gt_m, sh_f, sc_f, gt_f = [m[:, :, j * d:(j + 1) * d] for j in range(6)]
            ka, va, qab, kab, vab, gla_in, lru_in = _premix(g["x"], g_pre_mix[l], sc_m, sh_m, lw["w_in_p"], t, tm)
            q3 = qab.reshape(b, t, W_A)
            k3 = kab.reshape(b, t, W_A)
            v3 = vab.reshape(b, t, W_A)
            if g["past"]:
                n_past = cache_k_sb.shape[2]
                padk = ((0, 0), (0, -t % ATTN_KEY_TILE), (0, 0))
                oa = _attention(q3, jnp.pad(k3, padk), jnp.pad(v3, padk),
                                cache_k_sb.reshape(depth, b, n_past, W_A), cache_v_sb.reshape(depth, b, n_past, W_A),
                                layer=l)
                s0 = state_gla[l]
                buf0 = state_conv[l]
                h0 = state_lru[l]
            else:
                oa = _attention(q3, k3, v3)
                s0 = jnp.zeros((b, H_B, DK_B, DV_B), F32)
                buf0 = jnp.zeros((b, CONV_W - 1, W_C), F32)
                h0 = jnp.zeros((b, W_C), F32)
            oa = oa.reshape(n, W_A)
            ob, s_new = _gla(gla_in, lw["wg_p"], lw["bg_p"], lw["gn_p"], _pad_state(s0), t)
            buf0p = jnp.pad(buf0, ((0, 0), (SUBLANE - (CONV_W - 1), 0), (0, 0)))
            oc, conv_new, h_new = _lru(lru_in, w_conv[l], b_conv[l].reshape(1, -1), lw["wa_bd"],
                                       b_rg_a[l].reshape(1, -1), lw["wx_bd"], b_rg_x[l].reshape(1, -1),
                                       lru_lambda[l].reshape(1, -1), buf0p, h0.reshape(b, 1, W_C), t)
            xm, h2, route, stat = _postmix(g["x"], oa, ob, oc, lw["wo_a"], lw["wo_b"], lw["wo_c"],
                                           g_post_mix[l], g_pre_ff[l], gt_m, sc_f, sh_f,
                                           lw["wr_hi"], lw["wr_lo"], lw["br"], t, tm)
            per_group.append(dict(xm=xm, h2=h2, route=route, stat=stat, gt_f=gt_f))
            new = (ka.reshape(b, t, W_A // HD_A, HD_A), va.reshape(b, t, W_A // HD_A, HD_A),
                   _unpad_state(s_new), conv_new[:, SUBLANE - (CONV_W - 1):], h_new.reshape(b, W_C))
            for j in range(5):
                st[gi][j].append(new[j])

        stat = jnp.concatenate([p["stat"].reshape(-1, SUBLANE, LANE)[:, :2, :N_EXPERTS] for p in per_group],
                               axis=0).astype(jnp.int32)
        run_rows, run_loc = stat[:, 0], stat[:, 1]
        before = jnp.cumsum(run_rows, axis=0) - run_rows
        counts = jnp.sum(run_rows, axis=0)
        padded = (counts + bm - 1) // bm * bm
        pad_end = jnp.cumsum(padded)
        pad_start = pad_end - padded
        block_start = jnp.arange(nb, dtype=jnp.int32) * bm
        block_e = jnp.minimum(jnp.sum((pad_end[None, :] <= block_start[:, None]).astype(jnp.int32), axis=1),
                              N_EXPERTS - 1)
        nvalid = (pad_end[-1] // bm).astype(jnp.int32).reshape(1)
        tile_rows = jnp.broadcast_to(jnp.sum(run_rows, axis=1, keepdims=True), run_rows.shape)
        tab = jnp.concatenate([run_rows, run_loc, pad_start[None, :] + before, tile_rows],
                              axis=1).reshape(-1, 1, LANE)
        tile0 = 0
        for gi, g in enumerate(groups):
            nt = g["n"] // g["tm"]
            per_group[gi]["tab"] = tab[tile0:tile0 + nt]
            tile0 += nt
        zeros32 = jnp.zeros_like(counts)
        tail = jnp.concatenate([padded - counts, zeros32, pad_start + counts,
                                jnp.broadcast_to(nvalid, (N_EXPERTS,))]).reshape(1, 1, LANE)
        first, second = per_group
        xs = _dispatch(first["tab"], tail, first["route"], first["h2"], None, n_pad, groups[0]["tm"],
                       later=second["tab"], later_tm=groups[1]["tm"])
        xs = _dispatch(second["tab"], tail, second["route"], second["h2"], xs, n_pad, groups[1]["tm"])
        ys = _experts(l, block_e, nvalid, xs, w_gate_up, b_gate_up, w_down, b_down)
        for gi, g in enumerate(groups):
            p = per_group[gi]
            g["x"] = _combine(p["tab"], p["route"], ys, p["xm"], p["gt_f"], g_post_ff[l], g["t"], g["tm"])

    outs = [g["x"].reshape(g["b"], g["t"], d) for g in groups]
    for gi in range(len(groups)):
        outs.extend(jnp.stack(st[gi][j]) for j in range(5))
    return tuple(outs)
```

```python
import functools
import math

import numpy as np
import jax
import jax.numpy as jnp
from jax import lax
from jax.experimental import pallas as pl
from jax.experimental.pallas import tpu as pltpu

F32 = jnp.float32
BF16 = jnp.bfloat16

D_MODEL = 1024
EPS = 1e-6
HD_A = 64
W_A = 384
H_B = 4
DK_B = 48
DV_B = 96
W_B = 384
GATE_RANK_B = 16
GATE_TEMP_B = 16.0
W_C = 256
NBLK_C = 4
BW_C = 64
CONV_W = 4
RG_C = 8.0
N_EXPERTS = 32
TOP_K = 4
D_FF = 1024
SWIGLU_ALPHA = 1.702
SWIGLU_LIMIT = 7.0
LOG_WEIGHT_FLOOR = -120.0

LANE = 128
SUBLANE = 8
V7X_VMEM_BYTES = 64 * 1024 * 1024
VMEM_LIMIT = V7X_VMEM_BYTES * 7 // 8
ROUTER_PAD_BIAS = -1e30

HP_B = H_B * LANE
HK_B = LANE // 2
HPK_B = H_B * HK_B
OFF_GLA = 3 * W_A
GLA_W = 2 * HPK_B + 2 * HP_B + LANE
OFF_LRU = OFF_GLA + GLA_W
IN_PAD = OFF_LRU + 2 * W_C

TOKEN_TILE = 256
PROJ_TILE = 512
ATTN_TILE = 256
ATTN_KEY_TILE = 256
ATTN_CHAINS = 8
GLA_CHUNK = 128
GLA_CHUNKS_PER_STEP = 4
LRU_CHUNK = 256
MOE_BLOCK_ROWS = 512
RUN_BIG_ROWS = 64


def _cparams(sem):
    return pltpu.CompilerParams(dimension_semantics=sem, vmem_limit_bytes=VMEM_LIMIT)


def _sigmoid(x):
    return 1.0 / (1.0 + jnp.exp(-x))


def _log_sigmoid(x):
    return jnp.minimum(x, 0.0) - jnp.log(1.0 + jnp.exp(-jnp.abs(x)))


def _split_bf16(x):
    hi = x.astype(BF16)
    lo = (x - hi.astype(F32)).astype(BF16)
    return hi, lo


def _dot(a, b):
    return jnp.dot(a, b, preferred_element_type=F32)


def _dot_nt(a, b):
    return lax.dot_general(a, b, (((1,), (1,)), ((), ())), preferred_element_type=F32)


def _rms(x, g):
    return x * lax.rsqrt(jnp.mean(x * x, axis=-1, keepdims=True) + EPS) * g


def _per_batch(f, x, *m_refs):
    tm, d = x.shape
    nbb = m_refs[0].shape[0]
    return f(x.reshape(nbb, tm // nbb, d), *[m[...] for m in m_refs]).reshape(tm, d)


def _batch_spec(tm, seq, d):
    assert tm % seq == 0 or seq % tm == 0
    if tm >= seq:
        return pl.BlockSpec((tm // seq, 1, d), lambda i: (i, 0, 0))
    per_b = seq // tm
    return pl.BlockSpec((1, 1, d), lambda i: (i // per_b, 0, 0))


def _mod_kernel(c_ref, w_ref, b_ref, o_ref):
    c = c_ref[...]
    a = (c * _sigmoid(c)).astype(BF16)
    o_ref[0] = _dot(a, w_ref[0].astype(BF16)) + b_ref[0]


def _modulation(c_all, w_ada, b_ada):
    depth, d, n = w_ada.shape
    r = c_all.shape[0]
    tn = n // 4
    return pl.pallas_call(
        _mod_kernel,
        grid=(depth, n // tn),
        in_specs=[pl.BlockSpec((r, d), lambda l, j: (0, 0)),
                  pl.BlockSpec((1, d, tn), lambda l, j: (l, 0, j)),
                  pl.BlockSpec((1, 1, tn), lambda l, j: (l, 0, j))],
        out_specs=pl.BlockSpec((1, r, tn), lambda l, j: (l, 0, j)),
        out_shape=jax.ShapeDtypeStruct((depth, r, n), F32),
        compiler_params=_cparams(("arbitrary", "arbitrary")),
        name="adaln_mod",
    )(c_all, w_ada, b_ada.reshape(depth, 1, n))


def _premix_kernel(x_ref, g_ref, sc_ref, sh_ref, w_ref,
                   ka_ref, va_ref, qab_ref, kab_ref, vab_ref, gla_ref, lru_ref):
    h = _per_batch(lambda y, sc, sh: y * (1.0 + sc) + sh, _rms(x_ref[...], g_ref[...]), sc_ref, sh_ref)
    r = _dot(h.astype(BF16), w_ref[...])
    ka = r[:, W_A:2 * W_A]
    va = r[:, 2 * W_A:3 * W_A]
    ka_ref[...] = ka
    va_ref[...] = va
    qab_ref[...] = (r[:, 0:W_A] * (HD_A ** -0.5)).astype(BF16)
    kab_ref[...] = ka.astype(BF16)
    vab_ref[...] = va.astype(BF16)
    gla_ref[...] = r[:, OFF_GLA:OFF_LRU]
    lru_ref[...] = r[:, OFF_LRU:IN_PAD]


def _premix(x2, g, sc, sh, w_in_p, seq, tm):
    n, d = x2.shape
    tok = lambda i: (i, 0)
    bat = _batch_spec(tm, seq, d)
    const = lambda i: (0, 0)
    outs = [((n, W_A), F32), ((n, W_A), F32), ((n, W_A), BF16), ((n, W_A), BF16), ((n, W_A), BF16),
            ((n, GLA_W), F32), ((n, 2 * W_C), F32)]
    return pl.pallas_call(
        _premix_kernel,
        grid=(n // tm,),
        in_specs=[pl.BlockSpec((tm, d), tok), pl.BlockSpec((1, d), const), bat, bat,
                  pl.BlockSpec((d, IN_PAD), const)],
        out_specs=[pl.BlockSpec((tm, s[1]), tok) for s, _ in outs],
        out_shape=[jax.ShapeDtypeStruct(s, t) for s, t in outs],
        compiler_params=_cparams(("arbitrary",)),
        name="premix_proj",
    )(x2, g.reshape(1, d), sc, sh, w_in_p)


def _attn_kernel(q_ref, k_ref, v_ref, *rest, tq, tk, off, has_past, nsub, nbat):
    if has_past:
        kp_ref, vp_ref, u_ref, o_ref, acc_ref, c_ref = rest
    else:
        u_ref, o_ref, acc_ref, c_ref = rest
    i = pl.program_id(2)
    lane = lax.broadcasted_iota(jnp.int32, (1, LANE), 1)
    head_lanes = (lane < HD_A, lane >= HD_A)
    acc_ref[...] = jnp.zeros_like(acc_ref)
    c_ref[...] = jnp.zeros_like(c_ref)
    row = lax.broadcasted_iota(jnp.int32, (tq, tk), 0)
    col = lax.broadcasted_iota(jnp.int32, (tq, tk), 1)
    u = u_ref[...]
    chains = [(bi, a) for bi in range(nbat) for a in range(nsub)]
    qh, qpos0, nk = [], [], []
    for bi, a in chains:
        q = q_ref[bi, a * tq:(a + 1) * tq, :]
        qz = jnp.zeros_like(q)
        qh.append(tuple(jnp.where(m, q, qz) for m in head_lanes))
        qpos0.append(off + (i * nsub + a) * tq)
        nk.append((qpos0[-1] + tq - 1 + tk - 1) // tk)

    def step(jj, masked):
        cmax = jnp.float32(-jnp.inf)
        for a, (bi, _) in enumerate(chains):
            j = nk[a] - 1 - jj
            live = j >= 0
            ks = pl.multiple_of(jnp.maximum(j, 0) * tk, tk)
            if has_past and not masked:
                kb = kp_ref[0, bi, pl.ds(ks, tk), :].astype(BF16)
                vb = vp_ref[0, bi, pl.ds(ks, tk), :].astype(BF16)
            else:
                kb = k_ref[bi, pl.ds(pl.multiple_of(ks - off, tk), tk), :]
                vb = v_ref[bi, pl.ds(pl.multiple_of(ks - off, tk), tk), :]
            vz = jnp.zeros_like(vb)
            mask = (col + ks) < (row + qpos0[a])
            for h in range(2):
                s = _dot_nt(qh[a][h], kb)
                lf = -(jnp.maximum(s, 0.0) + jnp.log(1.0 + jnp.exp(-jnp.abs(s))))
                lb = s + lf
                if masked:
                    lf = jnp.where(mask, lf, 0.0)
                hi, lo = _split_bf16(lf)
                cr = _dot(jnp.concatenate([hi, lo], axis=1), u)
                c = c_ref[a, h]
                cfull = jnp.concatenate([c] * (tk // LANE), axis=1)
                w = jnp.exp(lb + cr[:, :tk] + cfull)
                if masked:
                    w = jnp.where(mask, w, 0.0)
                vh = jnp.where(head_lanes[h], vb, vz)
                pv = _dot(w.astype(BF16), vh)
                rs = cr[:, tk:]
                if not masked:
                    pv = jnp.where(live, pv, 0.0)
                    rs = jnp.where(live, rs, 0.0)
                acc_ref[a] += pv
                c_ref[a, h] = c + rs
            cmax = jnp.maximum(cmax, jnp.where(j > 0, jnp.max(c_ref[a]), -jnp.inf))
        return cmax > LOG_WEIGHT_FLOOR

    n_masked = max(1, tq // tk)
    for jj in range(n_masked):
        go = step(jj, True)

    def body(carry):
        jj, _ = carry
        return jj + 1, step(jj, False)

    lax.while_loop(lambda carry: jnp.logical_and(carry[0] < nk[-1], carry[1]), body, (jnp.int32(n_masked), go))
    for c, (bi, a) in enumerate(chains):
        o_ref[bi, a * tq:(a + 1) * tq, :] = acc_ref[c].astype(o_ref.dtype)


def _attn_umat(tk):
    jp = np.arange(tk)[:, None]
    j = np.arange(tk)[None, :]
    u = np.concatenate([(jp > j).astype(np.float32), np.ones((tk, LANE), np.float32)], axis=1)
    return jnp.asarray(np.concatenate([u, u], axis=0), BF16)


def _attention(q, k, v, kp=None, vp=None, layer=0):
    b, tq_all, _ = q.shape
    tc = k.shape[1]
    has_past = kp is not None
    off = kp.shape[2] if has_past else 0
    tq = min(ATTN_TILE, tq_all)
    tk = ATTN_KEY_TILE
    assert tc % tk == 0 and tq_all % tq == 0 and tq_all <= tc
    assert off % tk == 0 and (tk % tq == 0 or tq % tk == 0)
    assert not has_past or (tq == tq_all and tq_all <= tk)
    nsub = math.gcd(ATTN_CHAINS, tq_all // tq)
    nbat = math.gcd(ATTN_CHAINS // nsub, b)
    kern = functools.partial(_attn_kernel, tq=tq, tk=tk, off=off, has_past=has_past, nsub=nsub, nbat=nbat)
    cur = pl.BlockSpec((nbat, tc, LANE), lambda bi, hp, i: (bi, 0, hp))
    in_specs = [pl.BlockSpec((nbat, nsub * tq, LANE), lambda bi, hp, i: (bi, i, hp)), cur, cur]
    args = [q, k, v]
    if has_past:
        past = pl.BlockSpec((1, nbat, off, LANE), lambda bi, hp, i: (layer, bi, 0, hp))
        in_specs += [past, past]
        args += [kp, vp]
    in_specs.append(pl.BlockSpec((2 * tk, tk + LANE), lambda bi, hp, i: (0, 0)))
    args.append(_attn_umat(tk))
    return pl.pallas_call(
        kern,
        grid=(b // nbat, W_A // LANE, tq_all // (nsub * tq)),
        in_specs=in_specs,
        out_specs=pl.BlockSpec((nbat, nsub * tq, LANE), lambda bi, hp, i: (bi, i, hp)),
        out_shape=jax.ShapeDtypeStruct((b, tq_all, W_A), BF16),
        scratch_shapes=[pltpu.VMEM((nbat * nsub, tq, LANE), F32), pltpu.VMEM((nbat * nsub, 2, tq, LANE), F32)],
        compiler_params=_cparams(("arbitrary", "arbitrary", "arbitrary")),
        name="stick_breaking_attn",
    )(*args)


def _gla_levels(c):
    return int(math.log2(c))


def _gla_mats(c):
    t = np.arange(c)[:, None]
    s = np.arange(c)[None, :]
    mats = [(s <= t).astype(np.float32), (s > t).astype(np.float32)]
    for lv in range(_gla_levels(c)):
        m = 1 << lv
        ref = (t // (2 * m)) * (2 * m) + m - 1
        mats.append(((s > ref) & (s <= t)).astype(np.float32) - ((s > t) & (s <= ref)).astype(np.float32))
    m = np.concatenate(mats, axis=0)
    return jnp.asarray(np.concatenate([m, m], axis=1), BF16)


def _gla_kernel(in_ref, wg_ref, bg_ref, gn_ref, mall_ref, s0_ref, ob_ref, sout_ref, st_ref, *, C, G):
    ci = pl.program_id(1)

    @pl.when(ci == 0)
    def _():
        st_ref[...] = s0_ref[0]

    for sub in range(G):
        _gla_chunk(in_ref, wg_ref, bg_ref, gn_ref, mall_ref, ob_ref, st_ref, C, sub * C)

    @pl.when(ci == pl.num_programs(1) - 1)
    def _():
        sout_ref[0] = st_ref[...]


def _gla_chunk(in_ref, wg_ref, bg_ref, gn_ref, mall_ref, ob_ref, st_ref, C, r0):
    rows = slice(r0, r0 + C)
    q = in_ref[rows, 0:HPK_B] * (DK_B ** -0.5)
    k = in_ref[rows, HPK_B:2 * HPK_B]
    v = in_ref[rows, 2 * HPK_B:2 * HPK_B + HP_B]
    gb = in_ref[rows, 2 * HPK_B + HP_B:2 * HPK_B + 2 * HP_B]
    rb = in_ref[rows, 2 * HPK_B + 2 * HP_B:GLA_W]
    lg = _log_sigmoid(_dot(rb.astype(BF16), wg_ref[...]) + bg_ref[...]) * (1.0 / GATE_TEMP_B)
    hi, lo = _split_bf16(lg)
    dall = _dot(mall_ref[...], jnp.concatenate([hi, lo], axis=0))
    eb = jnp.exp(dall[0:C])
    elast = jnp.exp(dall[C:2 * C])
    eb_last = eb[C - 1:C, :]
    rowi = lax.broadcasted_iota(jnp.int32, (C, 1), 0)
    row = lax.broadcasted_iota(jnp.int32, (C, C), 0)
    col = lax.broadcasted_iota(jnp.int32, (C, C), 1)
    lane = lax.broadcasted_iota(jnp.int32, (1, LANE), 1)
    slot_half = (lane < HK_B, lane >= HK_B)
    qe = q * eb
    ke = k * elast
    kb = k.astype(BF16)
    vb = v.astype(BF16)
    zero = jnp.zeros_like(q)
    lv_q, lv_k = [], []
    for lv in range(_gla_levels(C)):
        e = jnp.exp(-jnp.abs(dall[(2 + lv) * C:(3 + lv) * C]))
        second = ((rowi >> lv) & 1) == 1
        lv_q.append(jnp.where(second, q * e, zero))
        lv_k.append(jnp.where(second, zero, k * e).astype(BF16))
    outs = []
    for h in range(H_B):
        sl = slice(h * LANE, (h + 1) * LANE)
        sk = slice((h // 2) * LANE, (h // 2 + 1) * LANE)

        def own(x):
            return jnp.where(slot_half[h % 2], x[:, sk], 0.0).astype(BF16)

        att = jnp.where(row == col, _dot_nt(own(q), kb[:, sk]), 0.0)
        for lv in range(_gla_levels(C)):
            same = (row >> (lv + 1)) == (col >> (lv + 1))
            att = att + jnp.where(same, _dot_nt(own(lv_q[lv]), lv_k[lv][:, sk]), 0.0)
        st = st_ref[h]
        o = _dot(att.astype(BF16), vb[:, sl]) + _dot_nt(own(qe), st.astype(BF16))
        st_ref[h] = st * eb_last[:, sk] + _dot(v[:, sl].T.astype(BF16), own(ke))
        ms = jnp.sum(o * o, axis=-1, keepdims=True) * (1.0 / DV_B)
        on = o * lax.rsqrt(ms + EPS) * gn_ref[:, sl]
        g = gb[:, sl]
        outs.append((on * (g * _sigmoid(g))).astype(BF16))
    ob_ref[rows, :] = jnp.concatenate(outs, axis=1)


def _gla(gla_in, wg_p, bg_p, gn_p, s0t, seq):
    n = gla_in.shape[0]
    b = n // seq
    c = min(GLA_CHUNK, seq)
    g = math.gcd(GLA_CHUNKS_PER_STEP, seq // c)
    per_b = seq // (c * g)
    mall = _gla_mats(c)
    const2 = lambda bi, ci: (0, 0)
    return pl.pallas_call(
        functools.partial(_gla_kernel, C=c, G=g),
        grid=(b, per_b),
        in_specs=[pl.BlockSpec((c * g, GLA_W), lambda bi, ci: (bi * per_b + ci, 0)),
                  pl.BlockSpec((LANE, HPK_B), const2), pl.BlockSpec((1, HPK_B), const2),
                  pl.BlockSpec((1, HP_B), const2), pl.BlockSpec(mall.shape, const2),
                  pl.BlockSpec((1, H_B, LANE, LANE), lambda bi, ci: (bi, 0, 0, 0))],
        out_specs=[pl.BlockSpec((c * g, HP_B), lambda bi, ci: (bi * per_b + ci, 0)),
                   pl.BlockSpec((1, H_B, LANE, LANE), lambda bi, ci: (bi, 0, 0, 0))],
        out_shape=[jax.ShapeDtypeStruct((n, HP_B), BF16),
                   jax.ShapeDtypeStruct((b, H_B, LANE, LANE), F32)],
        scratch_shapes=[pltpu.VMEM((H_B, LANE, LANE), F32)],
        compiler_params=_cparams(("arbitrary", "arbitrary")),
        name="gla_chunked",
    )(gla_in, wg_p, bg_p, gn_p, mall, s0t)


def _lru_kernel(in_ref, cw_ref, cb_ref, wa_ref, ba_ref, wx_ref, bx_ref, lam_ref, buf0_ref, h0_ref,
                oc_ref, conv_ref, hout_ref, xp_ref, hc_ref, *, C):
    ci = pl.program_id(1)

    @pl.when(ci == 0)
    def _():
        xp_ref[0:SUBLANE] = buf0_ref[0]
        hc_ref[...] = h0_ref[0]

    x = in_ref[:, 0:W_C]
    y = in_ref[:, W_C:2 * W_C]
    xp_ref[SUBLANE:SUBLANE + C] = x
    xc = cb_ref[...]
    for j in range(CONV_W):
        xc = xc + xp_ref[pl.ds(SUBLANE - (CONV_W - 1) + j, C), :] * cw_ref[j:j + 1, :]
    xcb = xc.astype(BF16)
    r = _sigmoid(_dot(xcb, wa_ref[...]) + ba_ref[...])
    gi = _sigmoid(_dot(xcb, wx_ref[...]) + bx_ref[...])
    log_a = RG_C * r * _log_sigmoid(lam_ref[...])
    a = jnp.exp(log_a)
    u = jnp.sqrt(1.0 - jnp.exp(2.0 * log_a)) * (gi * xc)
    rowi = lax.broadcasted_iota(jnp.int32, (C, 1), 0)
    d = 1
    while d < C:
        keep = rowi >= d
        a_s = pltpu.roll(a, d, axis=0)
        u_s = pltpu.roll(u, d, axis=0)
        u = jnp.where(keep, a * u_s + u, u)
        a = jnp.where(keep, a * a_s, a)
        d *= 2
    hseq = u + a * hc_ref[...]
    hc_ref[...] = hseq[C - 1:C, :]
    gelu = 0.5 * y * (1.0 + jnp.tanh(math.sqrt(2.0 / math.pi) * (y + 0.044715 * (y * y * y))))
    oc_ref[...] = (hseq * gelu).astype(BF16)
    tail = xp_ref[C:C + SUBLANE]
    xp_ref[0:SUBLANE] = tail

    @pl.when(ci == pl.num_programs(1) - 1)
    def _():
        conv_ref[0] = tail
        hout_ref[0] = hseq[C - 1:C, :]


def _lru(lru_in, cw, cb, wa_bd, ba, wx_bd, bx, lam, buf0, h0, seq):
    n = lru_in.shape[0]
    b = n // seq
    c = min(LRU_CHUNK, seq)
    per_b = seq // c
    const2 = lambda bi, ci: (0, 0)
    vec = pl.BlockSpec((1, W_C), const2)
    return pl.pallas_call(
        functools.partial(_lru_kernel, C=c),
        grid=(b, per_b),
        in_specs=[pl.BlockSpec((c, 2 * W_C), lambda bi, ci: (bi * per_b + ci, 0)),
                  pl.BlockSpec((CONV_W, W_C), const2), vec,
                  pl.BlockSpec((W_C, W_C), const2), vec, pl.BlockSpec((W_C, W_C), const2), vec, vec,
                  pl.BlockSpec((1, SUBLANE, W_C), lambda bi, ci: (bi, 0, 0)),
                  pl.BlockSpec((1, 1, W_C), lambda bi, ci: (bi, 0, 0))],
        out_specs=[pl.BlockSpec((c, W_C), lambda bi, ci: (bi * per_b + ci, 0)),
                   pl.BlockSpec((1, SUBLANE, W_C), lambda bi, ci: (bi, 0, 0)),
                   pl.BlockSpec((1, 1, W_C), lambda bi, ci: (bi, 0, 0))],
        out_shape=[jax.ShapeDtypeStruct((n, W_C), BF16),
                   jax.ShapeDtypeStruct((b, SUBLANE, W_C), F32),
                   jax.ShapeDtypeStruct((b, 1, W_C), F32)],
        scratch_shapes=[pltpu.VMEM((c + SUBLANE, W_C), F32), pltpu.VMEM((1, W_C), F32)],
        compiler_params=_cparams(("arbitrary", "arbitrary")),
        name="conv_rglru",
    )(lru_in, cw, cb, wa_bd, ba, wx_bd, bx, lam, buf0, h0)


def _postmix_kernel(x_ref, oa_ref, ob_ref, oc_ref, wa_ref, wb_ref, wc_ref, gpm_ref, gpf_ref,
                    gt_ref, sc_ref, sh_ref, wr_hi_ref, wr_lo_ref, br_ref, tri_ref, ustrict_ref,
                    xm_ref, h2_ref, route_ref, stat_ref, *, tm):
    y =_dot(oa_ref[...], wa_ref[...]) + _dot(ob_ref[...], wb_ref[...]) + _dot(oc_ref[...], wc_ref[...])
    xm = x_ref[...] + _per_batch(lambda r, gt: gt * r, _rms(y, gpm_ref[...]), gt_ref)
    xm_ref[...] = xm
    h2 = _per_batch(lambda y, sc, sh: y * (1.0 + sc) + sh, _rms(xm, gpf_ref[...]), sc_ref, sh_ref)
    h2_ref[...] = h2
    hi, lo = _split_bf16(h2)
    wh = wr_hi_ref[...]
    p_hi = _dot(hi, jnp.concatenate([wh, wr_lo_ref[...]], axis=1))
    logits = p_hi[:, :LANE] + _dot(lo, wh) + p_hi[:, LANE:] + br_ref[...]
    lane = lax.broadcasted_iota(jnp.int32, (tm, LANE), 1)
    lane_f = lane.astype(F32)
    neg = jnp.float32(-jnp.inf)
    vals, hots = [], []
    for _ in range(TOP_K):
        m = jnp.max(logits, axis=-1, keepdims=True)
        idx = jnp.min(jnp.where(logits == m, lane_f, float(LANE)), axis=-1, keepdims=True)
        hot = lane_f == idx
        logits = jnp.where(hot, neg, logits)
        vals.append(m)
        hots.append(hot)
    ex = [jnp.exp(vk - vals[0]) for vk in vals]
    inv = 1.0 / (ex[0] + ex[1] + ex[2] + ex[3])
    sel = jnp.zeros((tm, LANE), F32)
    for hot in hots:
        sel = jnp.where(hot, 1.0, sel)
    selb = sel.astype(BF16)
    tile_cnt = jnp.sum(sel, axis=0, keepdims=True)
    groups8 = jnp.floor((tile_cnt + (SUBLANE - 1.0)) * (1.0 / SUBLANE))
    g8b = jnp.broadcast_to(groups8, (SUBLANE, LANE)).astype(BF16)
    loc_start = _dot(g8b, ustrict_ref[...])[0:1] * float(SUBLANE)
    local = _dot(tri_ref[...], selb) + loc_start
    route = jnp.zeros((tm, LANE), F32)
    for kk in range(TOP_K):
        p_k = jnp.sum(jnp.where(hots[kk], local, 0.0), axis=-1, keepdims=True)
        route = jnp.where(lane == kk, ex[kk] * inv, route)
        route = jnp.where(lane == TOP_K + kk, p_k, route)
    route_ref[...] = route
    srow = lax.broadcasted_iota(jnp.int32, (SUBLANE, LANE), 0)
    stat_ref[...] = jnp.where(srow == 0, groups8 * float(SUBLANE), jnp.where(srow == 1, loc_start, 0.0))


def _postmix(x2, oa, ob, oc, wo_a, wo_b, wo_c, gpm, gpf, gt, sc, sh, wr_hi, wr_lo, br, seq, tm):
    n, d = x2.shape
    tok = lambda i: (i, 0)
    bat = _batch_spec(tm, seq, d)
    const = lambda i: (0, 0)
    tri = jnp.asarray(np.tril(np.ones((tm, tm), np.float32), -1), BF16)
    ustrict = jnp.asarray(np.triu(np.ones((LANE, LANE), np.float32), 1), BF16)
    full = lambda a: pl.BlockSpec(a.shape, const)
    nt = n // tm
    return pl.pallas_call(
        functools.partial(_postmix_kernel, tm=tm),
        grid=(nt,),
        in_specs=[pl.BlockSpec((tm, d), tok), pl.BlockSpec((tm, W_A), tok),
                  pl.BlockSpec((tm, HP_B), tok), pl.BlockSpec((tm, W_C), tok),
                  full(wo_a), full(wo_b), full(wo_c),
                  pl.BlockSpec((1, d), const), pl.BlockSpec((1, d), const),
                  bat, bat, bat,
                  full(wr_hi), full(wr_lo), pl.BlockSpec((1, LANE), const), full(tri), full(ustrict)],
        out_specs=[pl.BlockSpec((tm, d), tok), pl.BlockSpec((tm, d), tok),
                   pl.BlockSpec((tm, LANE), tok), pl.BlockSpec((SUBLANE, LANE), tok)],
        out_shape=[jax.ShapeDtypeStruct((n, d), F32), jax.ShapeDtypeStruct((n, d), F32),
                   jax.ShapeDtypeStruct((n, LANE), F32), jax.ShapeDtypeStruct((nt * SUBLANE, LANE), F32)],
        compiler_params=_cparams(("arbitrary",)),
        name="postmix_router",
    )(x2, oa, ob, oc, wo_a, wo_b, wo_c, gpm.reshape(1, d), gpf.reshape(1, d), gt, sc, sh,
      wr_hi, wr_lo, br, tri, ustrict)


def _run_copies(tab_ref, tm, make_copy, wait, tile=0):
    sizes = [1 << b for b in range(int(math.log2(tm)), int(math.log2(SUBLANE)) - 1, -1)]
    big = [sz for sz in sizes if sz >= RUN_BIG_ROWS]
    small = [sz for sz in sizes if sz < RUN_BIG_ROWS]

    def per_expert(e, carry):
        n = tab_ref[tile, 0, e]
        loc = tab_ref[tile, 0, N_EXPERTS + e]
        dst = tab_ref[tile, 0, 2 * N_EXPERTS + e]

        def pieces(szs):
            for sz in szs:
                done = n & ~(2 * sz - 1)

                @pl.when((n & sz) != 0)
                def _():
                    cp = make_copy(pl.multiple_of(loc + done, SUBLANE), pl.multiple_of(dst + done, SUBLANE), sz)
                    if wait:
                        cp.wait()
                    else:
                        cp.start()

        if big:
            pl.when(n >= RUN_BIG_ROWS)(lambda: pieces(big))
        pieces(small)
        return carry

    lax.fori_loop(0, N_EXPERTS, per_expert, 0)


def _wait_rows(total, make_copy, max_rows):
    top = int(math.log2(max_rows))
    for sz in [1 << b for b in range(top, int(math.log2(SUBLANE)) - 1, -1)]:
        @pl.when((total & sz) != 0)
        def _():
            make_copy(0, 0, sz).wait()


def _local_rows(tm):
    return tm * TOP_K + N_EXPERTS * SUBLANE


def _slot_onehot(route_ref, tm):
    nloc = _local_rows(tm)
    pos = lax.broadcasted_iota(jnp.int32, (tm, nloc), 1)
    route = route_ref[...]
    slot = [route[:, TOP_K + kk:TOP_K + kk + 1].astype(jnp.int32) for kk in range(TOP_K)]

    def build(values):
        m = jnp.zeros((tm, nloc), F32)
        for kk in range(TOP_K):
            m = jnp.where(slot[kk] == pos, values[kk], m)
        return m

    return route, build


def _dispatch_kernel(tab_ref, tabp_ref, tail_ref, route_ref, h_ref, *rest, tm, first, later_tile):
    i = pl.program_id(0)
    slot = i % 2
    if first:
        later_ref, xs_ref, sorted_ref, sem, zero_ref = rest
        bm = MOE_BLOCK_ROWS

        @pl.when(i == 0)
        def _():
            zero_ref[...] = jnp.zeros_like(zero_ref)
            fill = lambda loc, dst, sz: pltpu.make_async_copy(zero_ref.at[pl.ds(0, sz)],
                                                              xs_ref.at[pl.ds(dst, sz)], sem.at[0])
            n_later, later_tm = later_ref.shape[0], later_tile

            def block_fill(wait):
                def body(g, carry):
                    cp = fill(0, pl.multiple_of(g * bm, bm), bm)
                    cp.wait() if wait else cp.start()
                    return carry
                lax.fori_loop(tail_ref[0, 0, 3 * N_EXPERTS], xs_ref.shape[0] // bm, body, 0)

            for wait in (False, True):
                _run_copies(tail_ref, bm, fill, wait)
                block_fill(wait)
                for ti in range(n_later):
                    _run_copies(later_ref, later_tm, fill, wait, tile=ti)
    else:
        _, xs_ref, sorted_ref, sem = rest
    _, build = _slot_onehot(route_ref, tm)
    perm = build([1.0] * TOP_K).astype(BF16)
    sorted_ref[slot] = lax.dot_general(perm, h_ref[...].astype(BF16), (((0,), (0,)), ((), ())),
                                       preferred_element_type=F32)

    def runs(s):
        return lambda loc, dst, sz: pltpu.make_async_copy(sorted_ref.at[s, pl.ds(loc, sz)],
                                                          xs_ref.at[pl.ds(dst, sz)], sem.at[s])

    _run_copies(tab_ref, tm, runs(slot), wait=False)

    @pl.when(i > 0)
    def _():
        _wait_rows(tabp_ref[0, 0, 3 * N_EXPERTS], runs(1 - slot), _local_rows(tm))

    @pl.when(i == pl.num_programs(0) - 1)
    def _():
        _wait_rows(tab_ref[0, 0, 3 * N_EXPERTS], runs(slot), _local_rows(tm))


def _dispatch(tab, tail, route, h2, xs, n_pad, tm, later=None, later_tm=None):
    n, d = h2.shape
    tok = lambda i: (i, 0)
    first = xs is None
    in_specs = [pl.BlockSpec((1, 1, LANE), lambda i: (i, 0, 0), memory_space=pltpu.SMEM),
                pl.BlockSpec((1, 1, LANE), lambda i: (jnp.maximum(i - 1, 0), 0, 0), memory_space=pltpu.SMEM),
                pl.BlockSpec((1, 1, LANE), lambda i: (0, 0, 0), memory_space=pltpu.SMEM),
                pl.BlockSpec((tm, LANE), tok), pl.BlockSpec((tm, d), tok)]
    scratch = [pltpu.VMEM((2, _local_rows(tm), d), F32), pltpu.SemaphoreType.DMA((2,))]
    args = [tab, tab, tail, route, h2]
    if first:
        scratch.append(pltpu.VMEM((MOE_BLOCK_ROWS, d), F32))
        in_specs.append(pl.BlockSpec(later.shape, lambda i: (0, 0, 0), memory_space=pltpu.SMEM))
        args.append(later)
    else:
        in_specs.append(pl.BlockSpec(memory_space=pl.ANY))
        args.append(xs)
    return pl.pallas_call(
        functools.partial(_dispatch_kernel, tm=tm, first=first, later_tile=later_tm),
        grid=(n // tm,),
        in_specs=in_specs,
        out_specs=pl.BlockSpec(memory_space=pl.ANY),
        out_shape=jax.ShapeDtypeStruct((n_pad, d), F32),
        scratch_shapes=scratch,
        input_output_aliases={} if first else {5: 0},
        compiler_params=_cparams(("arbitrary",)),
        name="moe_dispatch",
    )(*args)


def _expert_kernel(be_ref, nv_ref, x_ref, wgu_ref, bgu_ref, wd_ref, bd_ref, y_ref, wgu_bf, wd_bf):
    g = pl.program_id(0)

    @pl.when(jnp.logical_or(g == 0, be_ref[g] != be_ref[jnp.maximum(g - 1, 0)]))
    def _():
        wgu_bf[...] = wgu_ref[0, 0].astype(BF16)
        wd_bf[...] = wd_ref[0, 0].astype(BF16)

    @pl.when(g < nv_ref[0])
    def _():
        gu = _dot(x_ref[...].astype(BF16), wgu_bf[...]) + bgu_ref[0, 0]
        gt = jnp.minimum(gu[:, :D_FF], SWIGLU_LIMIT)
        up = jnp.clip(gu[:, D_FF:], -SWIGLU_LIMIT, SWIGLU_LIMIT)
        act = (up + 1.0) * (gt * _sigmoid(SWIGLU_ALPHA * gt))
        y_ref[...] = _dot(act.astype(BF16), wd_bf[...]) + bd_ref[0, 0]

    @pl.when(g >= nv_ref[0])
    def _():
        y_ref[...] = jnp.zeros_like(y_ref)


def _experts(l, block_e, nvalid, xs, wgu, bgu, wd, bd):
    n_pad, d = xs.shape
    depth = wgu.shape[0]
    bm = MOE_BLOCK_ROWS
    nb = n_pad // bm
    grid_spec = pltpu.PrefetchScalarGridSpec(
        num_scalar_prefetch=2,
        grid=(nb,),
        in_specs=[pl.BlockSpec((bm, d), lambda g, be, nv: (jnp.minimum(g, nv[0] - 1), 0)),
                  pl.BlockSpec((1, 1, d, 2 * D_FF), lambda g, be, nv: (l, be[g], 0, 0)),
                  pl.BlockSpec((1, 1, 1, 2 * D_FF), lambda g, be, nv: (l, be[g], 0, 0)),
                  pl.BlockSpec((1, 1, D_FF, d), lambda g, be, nv: (l, be[g], 0, 0)),
                  pl.BlockSpec((1, 1, 1, d), lambda g, be, nv: (l, be[g], 0, 0))],
        out_specs=pl.BlockSpec((bm, d), lambda g, be, nv: (g, 0)),
        scratch_shapes=[pltpu.VMEM((d, 2 * D_FF), BF16), pltpu.VMEM((D_FF, d), BF16)],
    )
    return pl.pallas_call(
        _expert_kernel,
        grid_spec=grid_spec,
        out_shape=jax.ShapeDtypeStruct((n_pad, d), F32),
        compiler_params=_cparams(("arbitrary",)),
        name="moe_experts",
    )(block_e, nvalid, xs, wgu, bgu.reshape(depth, N_EXPERTS, 1, -1), wd, bd.reshape(depth, N_EXPERTS, 1, -1))


def _combine_kernel(tab_ref, tabn_ref, route_ref, ys_ref, xm_ref, gt_ref, g_ref, o_ref, buf_ref, sem, *, tm):
    i = pl.program_id(0)
    slot = i % 2

    def runs(s):
        return lambda loc, dst, sz: pltpu.make_async_copy(ys_ref.at[pl.ds(dst, sz)],
                                                          buf_ref.at[s, pl.ds(loc, sz)], sem.at[s])

    @pl.when(i == 0)
    def _():
        buf_ref[...] = jnp.zeros_like(buf_ref)
        _run_copies(tab_ref, tm, runs(0), wait=False)

    @pl.when(i + 1 < pl.num_programs(0))
    def _():
        _run_copies(tabn_ref, tm, runs(1 - slot), wait=False)

    route, build = _slot_onehot(route_ref, tm)
    gate_m = build([route[:, kk:kk + 1] for kk in range(TOP_K)])
    g_hi, g_lo = _split_bf16(gate_m)
    _wait_rows(tab_ref[0, 0, 3 * N_EXPERTS], runs(slot), _local_rows(tm))
    b_hi, b_lo = _split_bf16(buf_ref[slot])
    y = _dot(g_hi, b_hi) + _dot(g_lo, b_hi) + _dot(g_hi, b_lo)
    o_ref[...] = xm_ref[...] + _per_batch(lambda r, gt: gt * r, _rms(y, g_ref[...]), gt_ref)


def _combine(tab, route, ys, xm, gt, g, seq, tm):
    n, d = xm.shape
    nt = n // tm
    tok = lambda i: (i, 0)
    return pl.pallas_call(
        functools.partial(_combine_kernel, tm=tm),
        grid=(nt,),
        in_specs=[pl.BlockSpec((1, 1, LANE), lambda i: (i, 0, 0), memory_space=pltpu.SMEM),
                  pl.BlockSpec((1, 1, LANE), lambda i: (jnp.minimum(i + 1, nt - 1), 0, 0),
                               memory_space=pltpu.SMEM),
                  pl.BlockSpec((tm, LANE), tok), pl.BlockSpec(memory_space=pl.ANY),
                  pl.BlockSpec((tm, d), tok),
                  _batch_spec(tm, seq, d),
                  pl.BlockSpec((1, d), lambda i: (0, 0))],
        out_specs=pl.BlockSpec((tm, d), tok),
        out_shape=jax.ShapeDtypeStruct((n, d), F32),
        scratch_shapes=[pltpu.VMEM((2, _local_rows(tm), d), F32), pltpu.SemaphoreType.DMA((2,))],
        compiler_params=_cparams(("arbitrary",)),
        name="moe_combine",
    )(tab, tab, route, ys, xm, gt, g.reshape(1, d))


def _pad_heads(w, width, slot=LANE):
    lead = w.shape[:-1]
    w = w.reshape(lead + (H_B, width))
    w = jnp.pad(w, [(0, 0)] * len(lead) + [(0, 0), (0, slot - width)])
    return w.reshape(lead + (H_B * slot,))


def _layer_weights(l, w_in, w_gla_gate, b_gla_gate, g_gla_norm, w_rg_a, w_rg_x, w_out, w_router, b_router):
    d = D_MODEL
    wi = w_in[l]
    o = 3 * W_A
    qb = wi[:, o:o + H_B * DK_B]
    kb = wi[:, o + 192:o + 384]
    vb = wi[:, o + 384:o + 768]
    gb = wi[:, o + 768:o + 1152]
    rb = wi[:, o + 1152:o + 1168]
    xy = wi[:, o + 1168:]
    w_in_p = jnp.concatenate(
        [wi[:, :o], _pad_heads(qb, DK_B, HK_B), _pad_heads(kb, DK_B, HK_B), _pad_heads(vb, DV_B),
         _pad_heads(gb, DV_B), jnp.pad(rb, ((0, 0), (0, LANE - GATE_RANK_B))), xy], axis=1).astype(BF16)
    wg_p = jnp.pad(_pad_heads(w_gla_gate[l], DK_B, HK_B), ((0, LANE - GATE_RANK_B), (0, 0))).astype(BF16)
    bg_p = _pad_heads(b_gla_gate[l].reshape(1, -1), DK_B, HK_B)
    gn_p = jnp.tile(jnp.pad(g_gla_norm[l], (0, LANE - DV_B)), H_B).reshape(1, HP_B)
    wa_bd = jax.scipy.linalg.block_diag(*[w_rg_a[l, i] for i in range(NBLK_C)]).astype(BF16)
    wx_bd = jax.scipy.linalg.block_diag(*[w_rg_x[l, i] for i in range(NBLK_C)]).astype(BF16)
    wo = w_out[l]
    wo_a = wo[:W_A].astype(BF16)
    wo_b = jnp.pad(wo[W_A:W_A + W_B].reshape(H_B, DV_B, d), ((0, 0), (0, LANE - DV_B), (0, 0)))
    wo_b = wo_b.reshape(HP_B, d).astype(BF16)
    wo_c = wo[W_A + W_B:].astype(BF16)
    wr = jnp.pad(w_router[l], ((0, 0), (0, LANE - N_EXPERTS)))
    wr_hi = wr.astype(BF16)
    wr_lo = (wr - wr_hi.astype(F32)).astype(BF16)
    br = jnp.pad(b_router[l], (0, LANE - N_EXPERTS), constant_values=ROUTER_PAD_BIAS).reshape(1, LANE)
    return dict(w_in_p=w_in_p, wg_p=wg_p, bg_p=bg_p, gn_p=gn_p, wa_bd=wa_bd, wx_bd=wx_bd,
                wo_a=wo_a, wo_b=wo_b, wo_c=wo_c, wr_hi=wr_hi, wr_lo=wr_lo, br=br)


def _pad_state(s):
    st = jnp.swapaxes(s, -1, -2)
    st = jnp.pad(st, ((0, 0), (0, 0), (0, LANE - DV_B), (0, HK_B - DK_B)))
    z = jnp.zeros_like(st)
    odd = (jnp.arange(H_B) % 2 == 1)[None, :, None, None]
    return jnp.concatenate([jnp.where(odd, z, st), jnp.where(odd, st, z)], axis=-1)


def _unpad_state(st):
    odd = (jnp.arange(H_B) % 2 == 1)[None, :, None, None]
    half = jnp.where(odd, st[..., HK_B:], st[..., :HK_B])
    return jnp.swapaxes(half[:, :, :DV_B, :DK_B], -1, -2)


def kernel(x_prompt, x_sample, c_prompt, c_sample, cache_k_sb, cache_v_sb, state_gla, state_conv, state_lru, w_ada, b_ada, g_pre_mix, g_post_mix, g_pre_ff, g_post_ff, w_in, w_gla_gate, b_gla_gate, g_gla_norm, w_conv, b_conv, w_rg_a, b_rg_a, w_rg_x, b_rg_x, lru_lambda, w_out, w_router, b_router, w_gate_up, b_gate_up, w_down, b_down):
    depth = w_ada.shape[0]
    d = D_MODEL
    groups = []
    for x, past in ((x_prompt, False), (x_sample, True)):
        b, t, _ = x.shape
        groups.append(dict(b=b, t=t, tm=min(TOKEN_TILE, b * t), n=b * t, past=past, x=x.reshape(b * t, d)))
    n_all = sum(g["n"] for g in groups)
    bm = MOE_BLOCK_ROWS
    n_tiles = sum(g["n"] // g["tm"] for g in groups)
    nb = -(-(n_all * TOP_K + n_tiles * N_EXPERTS * (SUBLANE - 1)) // bm) + N_EXPERTS
    n_pad = nb * bm

    nb_rows = sum(g["b"] for g in groups)
    c_all = jnp.concatenate([c_prompt, c_sample], axis=0)
    r_pad = -(-nb_rows // SUBLANE) * SUBLANE
    c_all = jnp.pad(c_all, ((0, r_pad - nb_rows), (0, 0)))
    mod = _modulation(c_all, w_ada, b_ada)

    st =[[[] for _ in range(5)] for _ in groups]
    for l in range(depth):
        lw = _layer_weights(l, w_in, w_gla_gate, b_gla_gate, g_gla_norm, w_rg_a, w_rg_x, w_out,
                            w_router, b_router)
        row0 = 0
        per_group = []
        for gi, g in enumerate(groups):
            b, t, tm, n = g["b"], g["t"], g["tm"], g["n"]
            m = mod[l, row0:row0 + b].reshape(b, 1, 6 * d)
            row0 += b
            sh_m, sc_m, gt_m, sh_f, sc_f, gt_f = [m[:, :, j * d:(j + 1) * d] for j in range(6)]
            tm_proj = PROJ_TILE if (n % PROJ_TILE == 0 and t % PROJ_TILE == 0) else tm
            ka, va, qab, kab, vab, gla_in, lru_in = _premix(g["x"], g_pre_mix[l], sc_m, sh_m, lw["w_in_p"], t,
                                                            tm_proj)
            q3 = qab.reshape(b, t, W_A)
            k3 = kab.reshape(b, t, W_A)
            v3 = vab.reshape(b, t, W_A)
            if g["past"]:
                n_past = cache_k_sb.shape[2]
                padk = ((0, 0), (0, -t % ATTN_KEY_TILE), (0, 0))
                oa = _attention(q3, jnp.pad(k3, padk), jnp.pad(v3, padk),
                                cache_k_sb.reshape(depth, b, n_past, W_A), cache_v_sb.reshape(depth, b, n_past, W_A),
                                layer=l)
                s0 = state_gla[l]
                buf0 = state_conv[l]
                h0 = state_lru[l]
            else:
                oa = _attention(q3, k3, v3)
                s0 = jnp.zeros((b, H_B, DK_B, DV_B), F32)
                buf0 = jnp.zeros((b, CONV_W - 1, W_C), F32)
                h0 = jnp.zeros((b, W_C), F32)
            oa = oa.reshape(n, W_A)
            ob, s_new = _gla(gla_in, lw["wg_p"], lw["bg_p"], lw["gn_p"], _pad_state(s0), t)
            buf0p = jnp.pad(buf0, ((0, 0), (SUBLANE - (CONV_W - 1), 0), (0, 0)))
            oc, conv_new, h_new = _lru(lru_in, w_conv[l], b_conv[l].reshape(1, -1), lw["wa_bd"],
                                       b_rg_a[l].reshape(1, -1), lw["wx_bd"], b_rg_x[l].reshape(1, -1),
                                       lru_lambda[l].reshape(1, -1), buf0p, h0.reshape(b, 1, W_C), t)
            xm, h2, route, stat = _postmix(g["x"], oa, ob, oc, lw["wo_a"], lw["wo_b"], lw["wo_c"],
                                           g_post_mix[l], g_pre_ff[l], gt_m, sc_f, sh_f,
                                           lw["wr_hi"], lw["wr_lo"], lw["br"], t, tm)
            per_group.append(dict(xm=xm, h2=h2, route=route, stat=stat, gt_f=gt_f))
            new = (ka.reshape(b, t, W_A // HD_A, HD_A), va.reshape(b, t, W_A // HD_A, HD_A),
                   _unpad_state(s_new), conv_new[:, SUBLANE - (CONV_W - 1):], h_new.reshape(b, W_C))
            for j in range(5):
                st[gi][j].append(new[j])

        stat = jnp.concatenate([p["stat"].reshape(-1, SUBLANE, LANE)[:, :2, :N_EXPERTS] for p in per_group],
                               axis=0).astype(jnp.int32)
        run_rows, run_loc = stat[:, 0], stat[:, 1]
        before = jnp.cumsum(run_rows, axis=0) - run_rows
        counts = jnp.sum(run_rows, axis=0)
        padded = (counts + bm - 1) // bm * bm
        pad_end = jnp.cumsum(padded)
        pad_start = pad_end - padded
        block_start = jnp.arange(nb, dtype=jnp.int32) * bm
        block_e = jnp.minimum(jnp.sum((pad_end[None, :] <= block_start[:, None]).astype(jnp.int32), axis=1),
                              N_EXPERTS - 1)
        nvalid = (pad_end[-1] // bm).astype(jnp.int32).reshape(1)
        tile_rows = jnp.broadcast_to(jnp.sum(run_rows, axis=1, keepdims=True), run_rows.shape)
        tab = jnp.concatenate([run_rows, run_loc, pad_start[None, :] + before, tile_rows],
                              axis=1).reshape(-1, 1, LANE)
        tile0 = 0
        for gi, g in enumerate(groups):
            nt = g["n"] // g["tm"]
            per_group[gi]["tab"] = tab[tile0:tile0 + nt]
            tile0 += nt
        zeros32 = jnp.zeros_like(counts)
        tail = jnp.concatenate([padded - counts, zeros32, pad_start + counts,
                                jnp.broadcast_to(nvalid, (N_EXPERTS,))]).reshape(1, 1, LANE)
        first, second = per_group
        xs = _dispatch(first["tab"], tail, first["route"], first["h2"], None, n_pad, groups[0]["tm"],
                       later=second["tab"], later_tm=groups[1]["tm"])
        xs = _dispatch(second["tab"], tail, second["route"], second["h2"], xs, n_pad, groups[1]["tm"])
        ys = _experts(l, block_e, nvalid, xs, w_gate_up, b_gate_up, w_down, b_down)
        for gi, g in enumerate(groups):
            p = per_group[gi]
            g["x"] = _combine(p["tab"], p["route"], ys, p["xm"], p["gt_f"], g_post_ff[l], g["t"], g["tm"])

    outs = [g["x"].reshape(g["b"], g["t"], d) for g in groups]
    for gi in range(len(groups)):
        outs.extend(jnp.stack(st[gi][j]) for j in range(5))
    return tuple(outs)
```
